```python
import jax, jax.numpy as jnp
from jax import lax
import numpy as np

D_MODEL = 2048
BATCH = 2
SEQ = 4096
DEPTH = 1

GRID_W = 64
CTX_LEN = 256
HG_HEADS = 8
HG_DK = 128
HG_DV = 128
HG_WIDTH = HG_HEADS * HG_DV
HG_CHUNK = 32
ATT_HEADS = 8
ATT_KV_HEADS = 2
HEAD_DIM = 128
ATT_WIDTH = ATT_HEADS * HEAD_DIM
WINDOW = 128
ATT_BLOCK = 128
ROPE_THETA = 10000.0
D_MIX = HG_WIDTH + ATT_WIDTH
HG_COLS = 3 * HG_HEADS * HG_DK + 2 * HG_HEADS * HG_DV
ATT_COLS = (ATT_HEADS + 2 * ATT_KV_HEADS) * HEAD_DIM
IN_COLS = HG_COLS + ATT_COLS
D_FF = ((8 * D_MODEL + 3 * 256 - 1) // (3 * 256)) * 256
N_MOD = 6
RMS_EPS = 1e-6
NEG_INF = -1e30
LB_SLACK = 1.0

kernel_name = "hymba_style_hgrn2_window_gqa_dit_block"


def rms_norm(x, g):
    xf = x.astype(jnp.float32)
    y = xf * lax.rsqrt(jnp.mean(xf * xf, axis=-1, keepdims=True) + RMS_EPS)
    return (y * g.astype(jnp.float32)).astype(x.dtype)


def modulate(h, shift, scale):
    return h * (1 + scale) + shift


def axial_rope(x, row, col):
    half = HEAD_DIM // 2
    nf = half // 2
    inv_freq = ROPE_THETA ** (-jnp.arange(nf, dtype=jnp.float32) / nf)
    extra = (1,) * (x.ndim - 3)

    def rotate(xa, pos):
        ang = pos.astype(jnp.float32)[:, None] * inv_freq
        cos = jnp.cos(ang).reshape((1, -1) + extra + (nf,))
        sin = jnp.sin(ang).reshape((1, -1) + extra + (nf,))
        x1 = xa[..., :nf].astype(jnp.float32)
        x2 = xa[..., nf:].astype(jnp.float32)
        return jnp.concatenate([x1 * cos - x2 * sin, x1 * sin + x2 * cos], axis=-1)

    out = jnp.concatenate([rotate(x[..., :half], row), rotate(x[..., half:], col)], axis=-1)
    return out.astype(x.dtype)


def chunked_gated_linear_scan(q, k, v, log_f, s0):
    B, H, T, DK = q.shape
    DV = v.shape[-1]
    N = T // HG_CHUNK
    q = q.reshape(B, H, N, HG_CHUNK, DK)
    k = k.reshape(B, H, N, HG_CHUNK, DK)
    v = v.reshape(B, H, N, HG_CHUNK, DV)
    b = jnp.cumsum(log_f.reshape(B, H, N, HG_CHUNK, DK), axis=3)
    b_last = b[:, :, :, -1:, :]
    q_dec = q * jnp.exp(b)
    k_inv = k * jnp.exp(-b)
    causal_in_chunk = jnp.tril(jnp.ones((HG_CHUNK, HG_CHUNK), dtype=bool))
    scores = jnp.einsum('bhnck,bhnsk->bhncs', q_dec, k_inv)
    scores = jnp.where(causal_in_chunk, scores, 0.0)
    o_intra = jnp.einsum('bhncs,bhnsv->bhncv', scores, v)
    kv_chunk = jnp.einsum('bhnsk,bhnsv->bhnkv', k * jnp.exp(b_last - b), v)
    decay = jnp.exp(b_last[:, :, :, 0, :])

    def step(s, inp):
        d, u = inp
        return d[..., None] * s + u, s

    s_final, s_starts = lax.scan(step, s0, (jnp.moveaxis(decay, 2, 0), jnp.moveaxis(kv_chunk, 2, 0)))
    s_starts = jnp.moveaxis(s_starts, 0, 2)
    o_inter = jnp.einsum('bhnck,bhnkv->bhncv', q_dec, s_starts)
    return (o_intra + o_inter).reshape(B, H, T, DV), s_final


def reverse_gated_linear_scan(q, k, v, log_f, s0):
    flip = lambda a: jnp.flip(a, axis=2)
    o, s = chunked_gated_linear_scan(flip(q), flip(k), flip(v), flip(log_f), s0)
    return flip(o), s


def hgrn2_mixer(p_lat, p_ctx, lb, norm_g, need_ctx_out):
    splits = [HG_HEADS * HG_DK, 2 * HG_HEADS * HG_DK, 3 * HG_HEADS * HG_DK,
              3 * HG_HEADS * HG_DK + HG_HEADS * HG_DV]

    def prep(p):
        B, T, _ = p.shape
        heads = lambda a: a.reshape(B, T, HG_HEADS, -1).transpose(0, 2, 1, 3).astype(jnp.float32)
        q, f_fwd, f_bwd, i, g = jnp.split(p, splits, axis=-1)
        q = jax.nn.silu(heads(q))
        gates = []
        for d, f_pre in enumerate((f_fwd, f_bwd)):
            lb_d = lb[d].reshape(HG_HEADS, 1, HG_DK)
            f = lb_d + (1.0 - lb_d) * jax.nn.sigmoid(heads(f_pre))
            gates.append((1.0 - f, jnp.log(f)))
        return q, heads(i), gates, g

    def readout(o, g):
        B, H, T, DV = o.shape
        o = rms_norm(o, norm_g).transpose(0, 2, 1, 3).reshape(B, T, H * DV)
        return (o * jax.nn.silu(g.astype(jnp.float32))).astype(g.dtype)

    qc, ic, (gc_f, gc_b), g_c = prep(p_ctx)
    ql, il, (gl_f, gl_b), g_l = prep(p_lat)
    B = p_lat.shape[0]
    zero = jnp.zeros((B, HG_HEADS, HG_DK, HG_DV), jnp.float32)
    oc_f, s_ctx_f = chunked_gated_linear_scan(qc, gc_f[0], ic, gc_f[1], zero)
    oc_b, s_ctx_b = reverse_gated_linear_scan(qc, gc_b[0], ic, gc_b[1], zero)
    ol_f, _ = chunked_gated_linear_scan(ql, gl_f[0], il, gl_f[1], s_ctx_f)
    ol_b, _ = reverse_gated_linear_scan(ql, gl_b[0], il, gl_b[1], s_ctx_b)
    lat_out = readout(ol_f + ol_b, g_l)
    ctx_out = readout(oc_f + oc_b, g_c) if need_ctx_out else None
    return lat_out, ctx_out


def window_attention(p_lat, p_ctx, q_g, k_g, sink, row, col, need_ctx_out):
    G = ATT_HEADS // ATT_KV_HEADS
    scale = HEAD_DIM ** -0.5

    def split_heads(p):
        B, T, _ = p.shape
        q, k, v = jnp.split(p, [ATT_HEADS * HEAD_DIM, (ATT_HEADS + ATT_KV_HEADS) * HEAD_DIM], axis=-1)
        q = rms_norm(q.reshape(B, T, ATT_KV_HEADS, G, HEAD_DIM), q_g)
        k = rms_norm(k.reshape(B, T, ATT_KV_HEADS, HEAD_DIM), k_g)
        return q, k, v.reshape(B, T, ATT_KV_HEADS, HEAD_DIM)

    ql, kl, vl = split_heads(p_lat)
    qc, kc, vc = split_heads(p_ctx)
    ql = axial_rope(ql, row, col)
    kl = axial_rope(kl, row, col)
    B, T = p_lat.shape[:2]
    L = p_ctx.shape[1]
    NB = T // ATT_BLOCK
    sink_f = sink.astype(jnp.float32).reshape(ATT_KV_HEADS, G, 1, 1)

    qb = ql.reshape(B, NB, ATT_BLOCK, ATT_KV_HEADS, G, HEAD_DIM)
    pad = ((0, 0), (ATT_BLOCK, ATT_BLOCK), (0, 0), (0, 0))
    kp = jnp.pad(kl, pad).reshape(B, NB + 2, ATT_BLOCK, ATT_KV_HEADS, HEAD_DIM)
    vp = jnp.pad(vl, pad).reshape(B, NB + 2, ATT_BLOCK, ATT_KV_HEADS, HEAD_DIM)
    k_band = jnp.concatenate([kp[:, :-2], kp[:, 1:-1], kp[:, 2:]], axis=2)
    v_band = jnp.concatenate([vp[:, :-2], vp[:, 1:-1], vp[:, 2:]], axis=2)
    s_band = jnp.einsum('bnqhgd,bnkhd->bhgnqk', qb, k_band).astype(jnp.float32) * scale
    s_ctx = jnp.einsum('bnqhgd,bkhd->bhgnqk', qb, kc).astype(jnp.float32) * scale
    qi = jnp.arange(ATT_BLOCK)[:, None]
    kj = jnp.arange(3 * ATT_BLOCK)[None, :]
    within = jnp.abs(kj - ATT_BLOCK - qi) <= WINDOW
    k_pos = (jnp.arange(NB)[:, None] - 1) * ATT_BLOCK + kj
    in_range = (k_pos >= 0) & (k_pos < T)
    mask = within[None, :, :] & in_range[:, None, :]
    s_band = jnp.where(mask, s_band, NEG_INF)
    sink_col = jnp.broadcast_to(sink_f[None, :, :, :, :, None], s_band.shape[:-1] + (1,))
    probs = jax.nn.softmax(jnp.concatenate([s_band, s_ctx, sink_col], axis=-1), axis=-1)
    p_band = probs[..., :3 * ATT_BLOCK].astype(vl.dtype)
    p_ctxk = probs[..., 3 * ATT_BLOCK:3 * ATT_BLOCK + L].astype(vc.dtype)
    o_lat = (jnp.einsum('bhgnqk,bnkhd->bnqhgd', p_band, v_band)
             + jnp.einsum('bhgnqk,bkhd->bnqhgd', p_ctxk, vc))
    lat_out = o_lat.reshape(B, T, ATT_WIDTH)

    ctx_out = None
    if need_ctx_out:
        sc = jnp.einsum('bqhgd,bkhd->bhgqk', qc, kc).astype(jnp.float32) * scale
        sink_c = jnp.broadcast_to(sink_f[None, :, :, :, :], sc.shape[:-1] + (1,))
        pc = jax.nn.softmax(jnp.concatenate([sc, sink_c], axis=-1), axis=-1)[..., :L]
        o_c = jnp.einsum('bhgqk,bkhd->bqhgd', pc.astype(vc.dtype), vc)
        ctx_out = o_c.reshape(B, L, ATT_WIDTH)
    return lat_out, ctx_out


def swiglu(h, w_gate_up, w_down):
    gate, up = jnp.split(h @ w_gate_up, 2, axis=-1)
    return (jax.nn.silu(gate) * up) @ w_down


def setup_inputs(seed: int = 0) -> dict:
    key = jax.random.key(seed)
    ks = jax.random.split(key, 17)
    nrm = lambda k, shape, s: s * jax.random.normal(k, shape, jnp.float32)
    hg_lb = nrm(ks[9], (DEPTH + 1, 2, HG_HEADS * HG_DK), 0.1).at[DEPTH].add(LB_SLACK)
    return {
        "x": nrm(ks[0], (BATCH, SEQ, D_MODEL), 1.0),
        "c": nrm(ks[1], (BATCH, D_MODEL), 1.0),
        "ctx": nrm(ks[2], (BATCH, CTX_LEN, D_MODEL), 1.0),
        "c_ctx": nrm(ks[3], (D_MODEL,), 1.0),
        "w_mod": nrm(ks[4], (DEPTH, D_MODEL, N_MOD * D_MODEL), 0.5 * D_MODEL ** -0.5),
        "b_mod": nrm(ks[5], (DEPTH, N_MOD * D_MODEL), 0.01),
        "norm_mix_g": 1.0 + nrm(ks[6], (DEPTH, D_MODEL), 0.05),
        "norm_ffn_g": 1.0 + nrm(ks[7], (DEPTH, D_MODEL), 0.05),
        "w_in": nrm(ks[8], (DEPTH, D_MODEL, IN_COLS), D_MODEL ** -0.5),
        "hg_lb": hg_lb,
        "hg_norm_g": 1.0 + nrm(ks[10], (DEPTH, HG_DV), 0.05),
        "q_norm_g": 1.0 + nrm(ks[11], (DEPTH, HEAD_DIM), 0.05),
        "k_norm_g": 1.0 + nrm(ks[12], (DEPTH, HEAD_DIM), 0.05),
        "attn_sink": nrm(ks[13], (DEPTH, ATT_HEADS), 1.0),
        "w_out": nrm(ks[14], (DEPTH, D_MIX, D_MODEL), D_MIX ** -0.5),
        "w_gate_up": nrm(ks[15], (DEPTH, D_MODEL, 2 * D_FF), D_MODEL ** -0.5),
        "w_down": nrm(ks[16], (DEPTH, D_FF, D_MODEL), D_FF ** -0.5),
    }


def reference(x, c, ctx, c_ctx, w_mod, b_mod, norm_mix_g, norm_ffn_g, w_in, hg_lb,
              hg_norm_g, q_norm_g, k_norm_g, attn_sink, w_out, w_gate_up, w_down):
    T = x.shape[1]
    rows = T // GRID_W
    row = jnp.broadcast_to(jnp.arange(rows)[:, None], (rows, GRID_W)).reshape(-1)
    col = jnp.broadcast_to(jnp.arange(GRID_W)[None, :], (rows, GRID_W)).reshape(-1)
    lb_all = jnp.cumsum(jax.nn.softmax(hg_lb.astype(jnp.float32), axis=0), axis=0)
    y = ctx
    for l in range(DEPTH):
        need_ctx_out = l < DEPTH - 1
        sh_m, sc_m, gt_m, sh_f, sc_f, gt_f = [m[:, None, :] for m in
            jnp.split(jax.nn.silu(c) @ w_mod[l] + b_mod[l], N_MOD, axis=-1)]
        csh_m, csc_m, cgt_m, csh_f, csc_f, cgt_f = jnp.split(
            jax.nn.silu(c_ctx) @ w_mod[l] + b_mod[l], N_MOD, axis=-1)

        h_lat = modulate(rms_norm(x, norm_mix_g[l]), sh_m, sc_m)
        h_ctx = modulate(rms_norm(y, norm_mix_g[l]), csh_m, csc_m)
        p_lat = h_lat @ w_in[l]
        p_ctx = h_ctx @ w_in[l]
        hg_lat, hg_ctx = hgrn2_mixer(p_lat[..., :HG_COLS], p_ctx[..., :HG_COLS],
                                     lb_all[l], hg_norm_g[l], need_ctx_out)
        at_lat, at_ctx = window_attention(p_lat[..., HG_COLS:], p_ctx[..., HG_COLS:],
                                          q_norm_g[l], k_norm_g[l], attn_sink[l],
                                          row, col, need_ctx_out)
        x = x + gt_m * (jnp.concatenate([hg_lat, at_lat], axis=-1) @ w_out[l])

        x = x + gt_f * swiglu(modulate(rms_norm(x, norm_ffn_g[l]), sh_f, sc_f),
                              w_gate_up[l], w_down[l])

        if need_ctx_out:
            y = y + cgt_m * (jnp.concatenate([hg_ctx, at_ctx], axis=-1) @ w_out[l])
            y = y + cgt_f * swiglu(modulate(rms_norm(y, norm_ffn_g[l]), csh_f, csc_f),
                                   w_gate_up[l], w_down[l])
    return x
```

```python
import functools

import jax
import jax.numpy as jnp
from jax import lax
from jax.experimental import pallas as pl
from jax.experimental.pallas import tpu as pltpu

F32 = jnp.float32
BF16 = jnp.bfloat16

GRID_W = 64
HG_HEADS = 8
HG_DK = 128
HG_DV = 128
HG_CHUNK = 32
ATT_HEADS = 8
ATT_KV_HEADS = 2
ATT_GROUP = ATT_HEADS // ATT_KV_HEADS
HEAD_DIM = 128
WINDOW = 128
ATT_BLOCK = 128
ROPE_THETA = 10000.0
N_MOD = 6
RMS_EPS = 1e-6
NEG_INF = -1e30
HG_WIDTH = HG_HEADS * HG_DV
ATT_WIDTH = ATT_HEADS * HEAD_DIM
HG_COLS = 3 * HG_HEADS * HG_DK + 2 * HG_HEADS * HG_DV
ATT_COLS = (ATT_HEADS + 2 * ATT_KV_HEADS) * HEAD_DIM
IN_COLS = HG_COLS + ATT_COLS

VMEM_LIMIT_BYTES = 58 * 1024 * 1024
MOD_ROWS = 8

_NT = (((1,), (1,)), ((), ()))
_TN = (((0,), (0,)), ((), ()))


def _dot(a, b):
    return jnp.dot(a, b, preferred_element_type=F32)


def _sigmoid(x):
    return 1.0 / (1.0 + jnp.exp(-x))


def _silu(x):
    return x * _sigmoid(x)


def _params(*sem):
    return pltpu.CompilerParams(dimension_semantics=sem, vmem_limit_bytes=VMEM_LIMIT_BYTES)


def _mod_kernel(c_ref, w_ref, b_ref, o_ref):
    a = _silu(c_ref[...]).astype(BF16)
    o_ref[...] = _dot(a, w_ref[...].astype(BF16)) + b_ref[...]


def _modulation(c_rows, w_mod, b_mod, tn=1024):
    d, n = w_mod.shape
    return pl.pallas_call(
        _mod_kernel,
        grid=(n // tn,),
        in_specs=[
            pl.BlockSpec((MOD_ROWS, d), lambda j: (0, 0)),
            pl.BlockSpec((d, tn), lambda j: (0, j)),
            pl.BlockSpec((1, tn), lambda j: (0, j)),
        ],
        out_specs=pl.BlockSpec((MOD_ROWS, tn), lambda j: (0, j)),
        out_shape=jax.ShapeDtypeStruct((MOD_ROWS, n), F32),
        compiler_params=_params("arbitrary"),
        name="adaln_mod",
    )(c_rows, w_mod, b_mod.reshape(1, n))


def _norm_modulate_into(x_ref, g_ref, shift, scale, h_scr, rows=256):
    tm = x_ref.shape[0]
    g = g_ref[...]

    def body(r, carry):
        sl = pl.ds(pl.multiple_of(r * rows, rows), rows)
        x = x_ref[sl, :]
        ms = jnp.mean(x * x, axis=-1, keepdims=True)
        y = x * lax.rsqrt(ms + RMS_EPS) * g
        h_scr[sl, :] = (y * (1.0 + scale) + shift).astype(BF16)
        return carry

    lax.fori_loop(0, tm // rows, body, 0)


def _inproj_kernel(x_ref, mod_ref, g_ref, w_ref, o_ref, h_scr):
    @pl.when(pl.program_id(1) == 0)
    def _():
        _norm_modulate_into(x_ref, g_ref, mod_ref[0, 0:1, :], mod_ref[0, 1:2, :], h_scr)

    o_ref[...] = _dot(h_scr[...], w_ref[...].astype(BF16)).astype(o_ref.dtype)


def _in_projection(x2d, mod, norm_g, w_in, tokens_per_mod, tm, tn=512):
    m, d = x2d.shape
    n = w_in.shape[1]
    tiles_per_mod = tokens_per_mod // tm
    return pl.pallas_call(
        _inproj_kernel,
        grid=(m // tm, n // tn),
        in_specs=[
            pl.BlockSpec((tm, d), lambda i, j: (i, 0)),
            pl.BlockSpec((1, N_MOD, d), lambda i, j: (i // tiles_per_mod, 0, 0)),
            pl.BlockSpec((1, d), lambda i, j: (0, 0)),
            pl.BlockSpec((d, tn), lambda i, j: (0, j)),
        ],
        out_specs=pl.BlockSpec((tm, tn), lambda i, j: (i, j)),
        out_shape=jax.ShapeDtypeStruct((m, n), F32),
        scratch_shapes=[pltpu.VMEM((tm, d), BF16)],
        compiler_params=_params("parallel", "arbitrary"),
        name="in_proj",
    )(x2d, mod, norm_g.reshape(1, d), w_in)


def _hgrn_kernel(q_ref, ff_ref, fb_ref, i_ref, g_ref, cff_ref, cfb_ref, ci_ref, lb_ref, ng_ref,
                 o_ref, of_scr, ob_scr):
    t_len = q_ref.shape[1]
    l_len = ci_ref.shape[1]
    c = HG_CHUNK
    n_lat = t_len // c
    n_ctx = l_len // c

    a = lb_ref[...]
    e = jnp.exp(a - jnp.max(a, axis=0, keepdims=True))
    lb = e[0] / jnp.sum(e, axis=0)
    lb_f = lb[0:1, :]
    lb_b = lb[1:2, :]

    row = lax.broadcasted_iota(jnp.int32, (c, c), 0)
    col = lax.broadcasted_iota(jnp.int32, (c, c), 1)
    lower = row >= col
    upper = row <= col
    lower_f = lower.astype(F32)
    upper_f = upper.astype(F32)

    def gates(f_pre, lb_d, tri_f, last):
        f = lb_d + (1.0 - lb_d) * _sigmoid(f_pre)
        key = 1.0 - f
        b = jnp.dot(tri_f, jnp.log(f), preferred_element_type=F32, precision=lax.Precision.HIGHEST)
        b_last = b[last:last + 1, :]
        return key, b, b_last

    def state_step(st, key, b, b_last, v):
        k_dec = (key * jnp.exp(b_last - b)).astype(BF16)
        upd = lax.dot_general(v.astype(BF16), k_dec, _TN, preferred_element_type=F32)
        return st * jnp.exp(b_last) + upd

    def out_step(st, q_pre, key, b, v, tri):
        q_dec = (_silu(q_pre) * jnp.exp(b)).astype(BF16)
        k_inv = (key * jnp.exp(-b)).astype(BF16)
        s = lax.dot_general(q_dec, k_inv, _NT, preferred_element_type=F32)
        s = jnp.where(tri, s, 0.0).astype(BF16)
        o = _dot(s, v.astype(BF16))
        return o + lax.dot_general(q_dec, st.astype(BF16), _NT, preferred_element_type=F32)

    def ctx_body(n, carry):
        st_f, st_b = carry
        sf = pl.ds(pl.multiple_of(n * c, c), c)
        sb = pl.ds(pl.multiple_of((n_ctx - 1 - n) * c, c), c)
        key, b, b_last = gates(cff_ref[0, sf, :], lb_f, lower_f, c - 1)
        st_f = state_step(st_f, key, b, b_last, ci_ref[0, sf, :])
        key, b, b_last = gates(cfb_ref[0, sb, :], lb_b, upper_f, 0)
        st_b = state_step(st_b, key, b, b_last, ci_ref[0, sb, :])
        return st_f, st_b

    def lat_body(n, carry):
        st_f, st_b = carry
        sf = pl.ds(pl.multiple_of(n * c, c), c)
        sb = pl.ds(pl.multiple_of((n_lat - 1 - n) * c, c), c)
        v = i_ref[0, sf, :]
        key, b, b_last = gates(ff_ref[0, sf, :], lb_f, lower_f, c - 1)
        of_scr[sf, :] = out_step(st_f, q_ref[0, sf, :], key, b, v, lower)
        st_f = state_step(st_f, key, b, b_last, v)
        v = i_ref[0, sb, :]
        key, b, b_last = gates(fb_ref[0, sb, :], lb_b, upper_f, 0)
        ob_scr[sb, :] = out_step(st_b, q_ref[0, sb, :], key, b, v, upper)
        st_b = state_step(st_b, key, b, b_last, v)
        return st_f, st_b

    zero = jnp.zeros((HG_DV, HG_DK), F32)
    carry = lax.fori_loop(0, n_ctx, ctx_body, (zero, zero))
    lax.fori_loop(0, n_lat, lat_body, carry)

    ng = ng_ref[...]
    rows = 256

    def fin_body(r, carry):
        sl = pl.ds(pl.multiple_of(r * rows, rows), rows)
        o = of_scr[sl, :] + ob_scr[sl, :]
        y = o * lax.rsqrt(jnp.mean(o * o, axis=-1, keepdims=True) + RMS_EPS) * ng
        o_ref[0, sl, :] = (y * _silu(g_ref[0, sl, :])).astype(o_ref.dtype)
        return carry

    lax.fori_loop(0, t_len // rows, fin_body, 0)


def _hgrn_mixer(p_lat, p_ctx, hg_lb, norm_g):
    bsz, t_len, _ = p_lat.shape
    l_len = p_ctx.shape[1]
    w = HG_DK

    def lat(group):
        return pl.BlockSpec((1, t_len, w), lambda b, h: (b, 0, group * HG_HEADS + h))

    def ctx(group):
        return pl.BlockSpec((1, l_len, w), lambda b, h: (b, 0, group * HG_HEADS + h))

    return pl.pallas_call(
        _hgrn_kernel,
        grid=(bsz, HG_HEADS),
        in_specs=[
            lat(0), lat(1), lat(2), lat(3), lat(4),
            ctx(1), ctx(2), ctx(3),
            pl.BlockSpec((hg_lb.shape[0], 2, w), lambda b, h: (0, 0, h)),
            pl.BlockSpec((1, HG_DV), lambda b, h: (0, 0)),
        ],
        out_specs=pl.BlockSpec((1, t_len, HG_DV), lambda b, h: (b, 0, h)),
        out_shape=jax.ShapeDtypeStruct((bsz, t_len, HG_WIDTH), BF16),
        scratch_shapes=[pltpu.VMEM((t_len, HG_DV), F32), pltpu.VMEM((t_len, HG_DV), F32)],
        compiler_params=_params("parallel", "parallel"),
        name="hgrn2_mixer",
    )(p_lat, p_lat, p_lat, p_lat, p_lat, p_ctx, p_ctx, p_ctx, hg_lb, norm_g.reshape(1, HG_DV))


def _rope_tables(t_len):
    half = HEAD_DIM // 2
    nf = half // 2
    rows = t_len // GRID_W
    row = jnp.broadcast_to(jnp.arange(rows)[:, None], (rows, GRID_W)).reshape(-1)
    col = jnp.broadcast_to(jnp.arange(GRID_W)[None, :], (rows, GRID_W)).reshape(-1)
    inv_freq = ROPE_THETA ** (-jnp.arange(nf, dtype=F32) / nf)
    ang_r = row.astype(F32)[:, None] * inv_freq
    ang_c = col.astype(F32)[:, None] * inv_freq
    cos = jnp.concatenate([jnp.cos(ang_r)] * 2 + [jnp.cos(ang_c)] * 2, axis=-1)
    sin = jnp.concatenate([-jnp.sin(ang_r), jnp.sin(ang_r), -jnp.sin(ang_c), jnp.sin(ang_c)], axis=-1)
    return cos, sin


def _head_norm(x, g):
    return x * lax.rsqrt(jnp.mean(x * x, axis=-1, keepdims=True) + RMS_EPS) * g


def _rope(x, cos, sin):
    quarter = HEAD_DIM // 4
    lane = lax.broadcasted_iota(jnp.int32, x.shape, 1)
    first = (lane % (2 * quarter)) < quarter
    partner = jnp.where(first, pltpu.roll(x, HEAD_DIM - quarter, 1), pltpu.roll(x, quarter, 1))
    return x * cos + partner * sin


def _attn_kernel(q_ref, kp_ref, kc_ref, kn_ref, vp_ref, vc_ref, vn_ref, ck_ref, cv_ref,
                 cosq_ref, sinq_ref, cosp_ref, sinp_ref, cosn_ref, sinn_ref,
                 qg_ref, kg_ref, sink_ref, o_ref):
    n = pl.program_id(1)
    nb = pl.num_programs(1)
    blk = ATT_BLOCK
    hd = HEAD_DIM
    l_len = ck_ref.shape[1]
    scale = hd ** -0.5
    qg = qg_ref[...]
    kg = kg_ref[...]
    cos_q = cosq_ref[...]
    sin_q = sinq_ref[...]

    qi = lax.broadcasted_iota(jnp.int32, (blk, 3 * blk), 0)
    kj = lax.broadcasted_iota(jnp.int32, (blk, 3 * blk), 1)
    k_pos = (n - 1) * blk + kj
    ok = (jnp.abs(kj - blk - qi) <= WINDOW) & (k_pos >= 0) & (k_pos < nb * blk)
    ok = jnp.concatenate([ok] * ATT_GROUP, axis=0)

    for j in range(ATT_KV_HEADS):
        hs = slice(j * hd, (j + 1) * hd)
        keys = [
            _rope(_head_norm(kp_ref[0, :, hs], kg), cosp_ref[...], sinp_ref[...]),
            _rope(_head_norm(kc_ref[0, :, hs], kg), cos_q, sin_q),
            _rope(_head_norm(kn_ref[0, :, hs], kg), cosn_ref[...], sinn_ref[...]),
            _head_norm(ck_ref[0, :, hs], kg),
        ]
        k_all = jnp.concatenate(keys, axis=0).astype(BF16)
        v_all = jnp.concatenate(
            [vp_ref[0, :, hs], vc_ref[0, :, hs], vn_ref[0, :, hs], cv_ref[0, :, hs]], axis=0).astype(BF16)
        qs = []
        for g in range(ATT_GROUP):
            h = j * ATT_GROUP + g
            q = _rope(_head_norm(q_ref[0, :, h * hd:(h + 1) * hd], qg), cos_q, sin_q)
            qs.append((q * scale).astype(BF16))
        q_all = jnp.concatenate(qs, axis=0)
        s = lax.dot_general(q_all, k_all, _NT, preferred_element_type=F32)
        s_band = jnp.where(ok, s[:, :3 * blk], NEG_INF)
        s_ctx = s[:, 3 * blk:]
        sink = jnp.concatenate(
            [jnp.broadcast_to(sink_ref[:, j * ATT_GROUP + g:j * ATT_GROUP + g + 1], (blk, 1))
             for g in range(ATT_GROUP)], axis=0)
        m = jnp.maximum(jnp.maximum(jnp.max(s_band, axis=-1, keepdims=True),
                                    jnp.max(s_ctx, axis=-1, keepdims=True)), sink)
        e_band = jnp.exp(s_band - m)
        e_ctx = jnp.exp(s_ctx - m)
        denom = (jnp.sum(e_band, axis=-1, keepdims=True) + jnp.sum(e_ctx, axis=-1, keepdims=True)
                 + jnp.exp(sink - m))
        o = _dot(e_band.astype(BF16), v_all[:3 * blk]) + _dot(e_ctx.astype(BF16), v_all[3 * blk:])
        o = o / denom
        for g in range(ATT_GROUP):
            h = j * ATT_GROUP + g
            o_ref[0, :, h * hd:(h + 1) * hd] = o[g * blk:(g + 1) * blk, :].astype(o_ref.dtype)


def _window_attention(p_lat, p_ctx, q_norm_g, k_norm_g, sink):
    bsz, t_len, _ = p_lat.shape
    l_len = p_ctx.shape[1]
    nb = t_len // ATT_BLOCK
    kvw = ATT_KV_HEADS * HEAD_DIM
    q_blk = HG_COLS // ATT_WIDTH
    k_blk = (HG_COLS + ATT_WIDTH) // kvw
    v_blk = k_blk + 1
    cos, sin = _rope_tables(t_len)

    def prev(n):
        return jnp.maximum(n - 1, 0)

    def nxt(n):
        return jnp.minimum(n + 1, nb - 1)

    def kv_spec(blk_idx, shift):
        return pl.BlockSpec((1, ATT_BLOCK, kvw), lambda b, n: (b, shift(n), blk_idx))

    def tab_spec(shift):
        return pl.BlockSpec((ATT_BLOCK, HEAD_DIM), lambda b, n: (shift(n), 0))

    same = lambda n: n
    vec = pl.BlockSpec((1, HEAD_DIM), lambda b, n: (0, 0))
    return pl.pallas_call(
        _attn_kernel,
        grid=(bsz, nb),
        in_specs=[
            pl.BlockSpec((1, ATT_BLOCK, ATT_WIDTH), lambda b, n: (b, n, q_blk)),
            kv_spec(k_blk, prev), kv_spec(k_blk, same), kv_spec(k_blk, nxt),
            kv_spec(v_blk, prev), kv_spec(v_blk, same), kv_spec(v_blk, nxt),
            pl.BlockSpec((1, l_len, kvw), lambda b, n: (b, 0, k_blk)),
            pl.BlockSpec((1, l_len, kvw), lambda b, n: (b, 0, v_blk)),
            tab_spec(same), tab_spec(same), tab_spec(prev), tab_spec(prev), tab_spec(nxt), tab_spec(nxt),
            vec, vec,
            pl.BlockSpec((1, ATT_HEADS), lambda b, n: (0, 0)),
        ],
        out_specs=pl.BlockSpec((1, ATT_BLOCK, ATT_WIDTH), lambda b, n: (b, n, 0)),
        out_shape=jax.ShapeDtypeStruct((bsz, t_len, ATT_WIDTH), BF16),
        compiler_params=_params("parallel", "parallel"),
        name="window_gqa",
    )(p_lat, p_lat, p_lat, p_lat, p_lat, p_lat, p_lat, p_ctx, p_ctx,
      cos, sin, cos, sin, cos, sin,
      q_norm_g.reshape(1, HEAD_DIM), k_norm_g.reshape(1, HEAD_DIM), sink.reshape(1, ATT_HEADS))


def _outproj_kernel(x_ref, mod_ref, hg_ref, at_ref, wt_ref, wb_ref, o_ref):
    acc = _dot(hg_ref[...], wt_ref[...].astype(BF16)) + _dot(at_ref[...], wb_ref[...].astype(BF16))
    o_ref[...] = x_ref[...] + mod_ref[0, 2:3, :] * acc


def _out_projection(x2d, mod, hg2d, at2d, w_out, tokens_per_mod, tm=1024, tn=512):
    m, d = x2d.shape
    kh = hg2d.shape[1]
    tiles_per_mod = tokens_per_mod // tm
    return pl.pallas_call(
        _outproj_kernel,
        grid=(m // tm, d // tn),
        in_specs=[
            pl.BlockSpec((tm, tn), lambda i, j: (i, j)),
            pl.BlockSpec((1, N_MOD, tn), lambda i, j: (i // tiles_per_mod, 0, j)),
            pl.BlockSpec((tm, kh), lambda i, j: (i, 0)),
            pl.BlockSpec((tm, kh), lambda i, j: (i, 0)),
            pl.BlockSpec((kh, tn), lambda i, j: (0, j)),
            pl.BlockSpec((kh, tn), lambda i, j: (1, j)),
        ],
        out_specs=pl.BlockSpec((tm, tn), lambda i, j: (i, j)),
        out_shape=jax.ShapeDtypeStruct((m, d), F32),
        compiler_params=_params("parallel", "arbitrary"),
        name="out_proj",
    )(x2d, mod, hg2d, at2d, w_out, w_out)


def _ffn_kernel(x_ref, mod_ref, g_ref, wg_ref, wu_ref, wd_ref, o_ref, h_scr):
    @pl.when(pl.program_id(1) == 0)
    def _():
        _norm_modulate_into(x_ref, g_ref, mod_ref[0, 3:4, :], mod_ref[0, 4:5, :], h_scr)
        o_ref[...] = x_ref[...]

    h = h_scr[...]
    gate = _dot(h, wg_ref[...].astype(BF16))
    up = _dot(h, wu_ref[...].astype(BF16))
    act = (_silu(gate) * up).astype(BF16)
    o_ref[...] += mod_ref[0, 5:6, :] * _dot(act, wd_ref[...].astype(BF16))


def _ffn(x2d, mod, norm_g, w_gate_up, w_down, tokens_per_mod, tm=1024, tf=256):
    m, d = x2d.shape
    d_ff = w_down.shape[0]
    nf = d_ff // tf
    tiles_per_mod = tokens_per_mod // tm
    return pl.pallas_call(
        _ffn_kernel,
        grid=(m // tm, nf),
        in_specs=[
            pl.BlockSpec((tm, d), lambda i, j: (i, 0)),
            pl.BlockSpec((1, N_MOD, d), lambda i, j: (i // tiles_per_mod, 0, 0)),
            pl.BlockSpec((1, d), lambda i, j: (0, 0)),
            pl.BlockSpec((d, tf), lambda i, j: (0, j)),
            pl.BlockSpec((d, tf), lambda i, j: (0, nf + j)),
            pl.BlockSpec((tf, d), lambda i, j: (j, 0)),
        ],
        out_specs=pl.BlockSpec((tm, d), lambda i, j: (i, 0)),
        out_shape=jax.ShapeDtypeStruct((m, d), F32),
        scratch_shapes=[pltpu.VMEM((tm, d), BF16)],
        compiler_params=_params("parallel", "arbitrary"),
        name="swiglu_ffn",
    )(x2d, mod, norm_g.reshape(1, d), w_gate_up, w_gate_up, w_down)


def kernel(x, c, ctx, c_ctx, w_mod, b_mod, norm_mix_g, norm_ffn_g, w_in, hg_lb, hg_norm_g,
           q_norm_g, k_norm_g, attn_sink, w_out, w_gate_up, w_down):
    bsz, t_len, d = x.shape
    l_len = ctx.shape[1]
    depth = w_mod.shape[0]
    assert depth == 1, "single-layer block: the context stream never needs its own outputs"
    assert bsz + 1 <= MOD_ROWS

    c_rows = jnp.zeros((MOD_ROWS, d), F32).at[:bsz].set(c).at[bsz].set(c_ctx)
    mod = _modulation(c_rows, w_mod[0], b_mod[0]).reshape(MOD_ROWS, N_MOD, d)
    mod_lat = mod[:bsz]
    mod_ctx = mod[bsz:bsz + 1]

    p_lat = _in_projection(x.reshape(bsz * t_len, d), mod_lat, norm_mix_g[0], w_in[0],
                           tokens_per_mod=t_len, tm=1024).reshape(bsz, t_len, IN_COLS)
    p_ctx = _in_projection(ctx.reshape(bsz * l_len, d), mod_ctx, norm_mix_g[0], w_in[0],
                           tokens_per_mod=bsz * l_len, tm=bsz * l_len).reshape(bsz, l_len, IN_COLS)

    hg = _hgrn_mixer(p_lat, p_ctx, hg_lb, hg_norm_g[0])
    at = _window_attention(p_lat, p_ctx, q_norm_g[0], k_norm_g[0], attn_sink[0])

    x1 = _out_projection(x.reshape(bsz * t_len, d), mod_lat, hg.reshape(bsz * t_len, HG_WIDTH),
                         at.reshape(bsz * t_len, ATT_WIDTH), w_out[0], tokens_per_mod=t_len)
    x2 = _ffn(x1, mod_lat, norm_ffn_g[0], w_gate_up[0], w_down[0], tokens_per_mod=t_len)
    return x2.reshape(bsz, t_len, d)
```

```python
import functools

import jax
import jax.numpy as jnp
from jax import lax
from jax.experimental import pallas as pl
from jax.experimental.pallas import tpu as pltpu

F32 = jnp.float32
BF16 = jnp.bfloat16

GRID_W = 64
HG_HEADS = 8
HG_DK = 128
HG_DV = 128
HG_CHUNK = 32
ATT_HEADS = 8
ATT_KV_HEADS = 2
ATT_GROUP = ATT_HEADS // ATT_KV_HEADS
HEAD_DIM = 128
WINDOW = 128
ATT_BLOCK = 128
ROPE_THETA = 10000.0
N_MOD = 6
RMS_EPS = 1e-6
NEG_INF = -1e30
HG_WIDTH = HG_HEADS * HG_DV
ATT_WIDTH = ATT_HEADS * HEAD_DIM
HG_COLS = 3 * HG_HEADS * HG_DK + 2 * HG_HEADS * HG_DV
ATT_COLS = (ATT_HEADS + 2 * ATT_KV_HEADS) * HEAD_DIM
IN_COLS = HG_COLS + ATT_COLS

VMEM_LIMIT_BYTES = 58 * 1024 * 1024
MOD_ROWS = 8

_NT = (((1,), (1,)), ((), ()))
_TN = (((0,), (0,)), ((), ()))


def _dot(a, b):
    return jnp.dot(a, b, preferred_element_type=F32)


def _sigmoid(x):
    return 1.0 / (1.0 + jnp.exp(-x))


def _silu(x):
    return x * _sigmoid(x)


def _params(*sem):
    return pltpu.CompilerParams(dimension_semantics=sem, vmem_limit_bytes=VMEM_LIMIT_BYTES)


def _mod_kernel(c_ref, w_ref, b_ref, o_ref):
    a = _silu(c_ref[...]).astype(BF16)
    o_ref[...] = _dot(a, w_ref[...].astype(BF16)) + b_ref[...]


def _modulation(c_rows, w_mod, b_mod, tn=1024):
    d, n = w_mod.shape
    return pl.pallas_call(
        _mod_kernel,
        grid=(n // tn,),
        in_specs=[
            pl.BlockSpec((MOD_ROWS, d), lambda j: (0, 0)),
            pl.BlockSpec((d, tn), lambda j: (0, j)),
            pl.BlockSpec((1, tn), lambda j: (0, j)),
        ],
        out_specs=pl.BlockSpec((MOD_ROWS, tn), lambda j: (0, j)),
        out_shape=jax.ShapeDtypeStruct((MOD_ROWS, n), F32),
        compiler_params=_params("arbitrary"),
        name="adaln_mod",
    )(c_rows, w_mod, b_mod.reshape(1, n))


def _norm_modulate_into(x_ref, g_ref, shift, scale, h_scr, rows=256):
    tm = x_ref.shape[0]
    g = g_ref[...]

    def body(r, carry):
        sl = pl.ds(pl.multiple_of(r * rows, rows), rows)
        x = x_ref[sl, :]
        ms = jnp.mean(x * x, axis=-1, keepdims=True)
        y = x * lax.rsqrt(ms + RMS_EPS) * g
        h_scr[sl, :] = (y * (1.0 + scale) + shift).astype(BF16)
        return carry

    lax.fori_loop(0, tm // rows, body, 0)


def _inproj_kernel(x_ref, mod_ref, g_ref, w_ref, o_ref, h_scr):
    @pl.when(pl.program_id(1) == 0)
    def _():
        _norm_modulate_into(x_ref, g_ref, mod_ref[0, 0:1, :], mod_ref[0, 1:2, :], h_scr)

    o_ref[...] = _dot(h_scr[...], w_ref[...].astype(BF16)).astype(o_ref.dtype)


def _in_projection(x2d, mod, norm_g, w_in, tokens_per_mod, tm, tn=512):
    m, d = x2d.shape
    n = w_in.shape[1]
    tiles_per_mod = tokens_per_mod // tm
    return pl.pallas_call(
        _inproj_kernel,
        grid=(m // tm, n // tn),
        in_specs=[
            pl.BlockSpec((tm, d), lambda i, j: (i, 0)),
            pl.BlockSpec((1, N_MOD, d), lambda i, j: (i // tiles_per_mod, 0, 0)),
            pl.BlockSpec((1, d), lambda i, j: (0, 0)),
            pl.BlockSpec((d, tn), lambda i, j: (0, j)),
        ],
        out_specs=pl.BlockSpec((tm, tn), lambda i, j: (i, j)),
        out_shape=jax.ShapeDtypeStruct((m, n), F32),
        scratch_shapes=[pltpu.VMEM((tm, d), BF16)],
        compiler_params=_params("parallel", "arbitrary"),
        name="in_proj",
    )(x2d, mod, norm_g.reshape(1, d), w_in)


HG_BLOCK = 256


def _chunk_masks():
    r = jnp.arange(HG_BLOCK)[:, None]
    c = jnp.arange(HG_BLOCK)[None, :]
    same = (r // HG_CHUNK) == (c // HG_CHUNK)
    return (same & (r >= c)), (same & (r <= c)), same


def _hgrn_kernel(q_ref, ff_ref, fb_ref, i_ref, g_ref, cff_ref, cfb_ref, ci_ref, lb_ref, ng_ref,
                 cumf_ref, cumb_ref, mskf_ref, mskb_ref, dmask_ref, o_ref, of_scr, ob_scr):
    t_len = q_ref.shape[1]
    l_len = ci_ref.shape[1]
    c = HG_CHUNK
    blk = HG_BLOCK
    nch = blk // c
    n_blk = t_len // blk
    assert l_len == blk

    a = lb_ref[...]
    e = jnp.exp(a - jnp.max(a, axis=0, keepdims=True))
    lb = e[0] / jnp.sum(e, axis=0)

    def block_step(st, f_pre, v, q_pre, lb_d, cum_ref, msk_ref, reverse):
        f = lb_d + (1.0 - lb_d) * _sigmoid(f_pre)
        key = 1.0 - f
        log_f = jnp.log(f)
        hi = log_f.astype(BF16)
        r1 = log_f - hi.astype(F32)
        mid = r1.astype(BF16)
        lo = (r1 - mid.astype(F32)).astype(BF16)
        b = _dot(cum_ref[...], jnp.concatenate([hi, mid, lo], axis=1))
        b = b[:, 0:128] + b[:, 128:256] + b[:, 256:384]
        b3 = b.reshape(nch, c, HG_DK)
        tot = b3[:, 0:1, :] if reverse else b3[:, c - 1:c, :]
        b_tot = jnp.broadcast_to(tot, (nch, c, HG_DK)).reshape(blk, HG_DK)
        dec = jnp.exp(tot)
        k_dec = (key * jnp.exp(b_tot - b)).astype(BF16)
        v_t = v.T.astype(BF16)
        upd = []
        for h in range(blk // 128):
            kd = k_dec[h * 128:(h + 1) * 128, :]
            kd4 = jnp.concatenate([kd] * 4, axis=1) * dmask_ref[...]
            u4 = _dot(v_t[:, h * 128:(h + 1) * 128], kd4)
            upd += [u4[:, j * HG_DK:(j + 1) * HG_DK] for j in range(4)]
        if q_pre is not None:
            q_dec = (_silu(q_pre) * jnp.exp(b)).astype(BF16)
            k_inv = (key * jnp.exp(-b)).astype(BF16)
            s = lax.dot_general(q_dec, k_inv, _NT, preferred_element_type=F32)
            s = jnp.where(msk_ref[...] != 0.0, s, 0.0).astype(BF16)
            o_intra = _dot(s, v.astype(BF16))
        outs = [None] * nch
        for n in (range(nch - 1, -1, -1) if reverse else range(nch)):
            if q_pre is not None:
                rows = slice(n * c, (n + 1) * c)
                outs[n] = o_intra[rows, :] + lax.dot_general(
                    q_dec[rows, :], st.astype(BF16), _NT, preferred_element_type=F32)
            st = st * dec[n] + upd[n]
        return (jnp.concatenate(outs, axis=0) if q_pre is not None else None), st

    lb_f = lb[0:1, :]
    lb_b = lb[1:2, :]
    zero = jnp.zeros((HG_DV, HG_DK), F32)
    cv = ci_ref[0]
    _, st_f = block_step(zero, cff_ref[0], cv, None, lb_f, cumf_ref, mskf_ref, False)
    _, st_b = block_step(zero, cfb_ref[0], cv, None, lb_b, cumb_ref, mskb_ref, True)

    def lat_body(r, carry):
        st_f, st_b = carry
        sf = pl.ds(pl.multiple_of(r * blk, blk), blk)
        sb = pl.ds(pl.multiple_of((n_blk - 1 - r) * blk, blk), blk)
        o, st_f = block_step(st_f, ff_ref[0, sf, :], i_ref[0, sf, :], q_ref[0, sf, :], lb_f,
                             cumf_ref, mskf_ref, False)
        of_scr[sf, :] = o
        o, st_b = block_step(st_b, fb_ref[0, sb, :], i_ref[0, sb, :], q_ref[0, sb, :], lb_b,
                             cumb_ref, mskb_ref, True)
        ob_scr[sb, :] = o
        return st_f, st_b

    lax.fori_loop(0, n_blk, lat_body, (st_f, st_b), unroll=2)

    ng = ng_ref[...]

    def fin_body(r, carry):
        sl = pl.ds(pl.multiple_of(r * blk, blk), blk)
        o = of_scr[sl, :] + ob_scr[sl, :]
        y = o * lax.rsqrt(jnp.mean(o * o, axis=-1, keepdims=True) + RMS_EPS) * ng
        o_ref[0, sl, :] = (y * _silu(g_ref[0, sl, :])).astype(o_ref.dtype)
        return carry

    lax.fori_loop(0, n_blk, fin_body, 0)


def _hgrn_mixer(p_lat, p_ctx, hg_lb, norm_g):
    bsz, t_len, _ = p_lat.shape
    l_len = p_ctx.shape[1]
    w = HG_DK

    def lat(group):
        return pl.BlockSpec((1, t_len, w), lambda b, h: (b, 0, group * HG_HEADS + h))

    def ctx(group):
        return pl.BlockSpec((1, l_len, w), lambda b, h: (b, 0, group * HG_HEADS + h))

    def const(shape):
        return pl.BlockSpec(shape, lambda b, h: (0,) * len(shape))

    lower, upper, same = _chunk_masks()
    diag4 = jnp.tile(same[:128, ::HG_CHUNK][:, :4, None], (1, 1, HG_DK)).reshape(128, 4 * HG_DK)
    seq_f32 = pltpu.VMEM((t_len, w), F32)
    return pl.pallas_call(
        _hgrn_kernel,
        grid=(bsz, HG_HEADS),
        in_specs=[
            lat(0), lat(1), lat(2), lat(3), lat(4),
            ctx(1), ctx(2), ctx(3),
            pl.BlockSpec((hg_lb.shape[0], 2, w), lambda b, h: (0, 0, h)),
            const((1, HG_DV)),
            const((HG_BLOCK, HG_BLOCK)), const((HG_BLOCK, HG_BLOCK)),
            const((HG_BLOCK, HG_BLOCK)), const((HG_BLOCK, HG_BLOCK)),
            const((128, 4 * HG_DK)),
        ],
        out_specs=pl.BlockSpec((1, t_len, HG_DV), lambda b, h: (b, 0, h)),
        out_shape=jax.ShapeDtypeStruct((bsz, t_len, HG_WIDTH), BF16),
        scratch_shapes=[seq_f32, seq_f32],
        compiler_params=_params("parallel", "parallel"),
        name="hgrn2_mixer",
    )(p_lat, p_lat, p_lat, p_lat, p_lat, p_ctx, p_ctx, p_ctx, hg_lb, norm_g.reshape(1, HG_DV),
      lower.astype(BF16), upper.astype(BF16), lower.astype(F32), upper.astype(F32), diag4.astype(BF16))


def _rope_tables(t_len):
    half = HEAD_DIM // 2
    nf = half // 2
    rows = t_len // GRID_W
    row = jnp.broadcast_to(jnp.arange(rows)[:, None], (rows, GRID_W)).reshape(-1)
    col = jnp.broadcast_to(jnp.arange(GRID_W)[None, :], (rows, GRID_W)).reshape(-1)
    inv_freq = ROPE_THETA ** (-jnp.arange(nf, dtype=F32) / nf)
    ang_r = row.astype(F32)[:, None] * inv_freq
    ang_c = col.astype(F32)[:, None] * inv_freq
    cos = jnp.concatenate([jnp.cos(ang_r)] * 2 + [jnp.cos(ang_c)] * 2, axis=-1)
    sin = jnp.concatenate([-jnp.sin(ang_r), jnp.sin(ang_r), -jnp.sin(ang_c), jnp.sin(ang_c)], axis=-1)
    return cos, sin


def _head_norm(x, g):
    return x * lax.rsqrt(jnp.mean(x * x, axis=-1, keepdims=True) + RMS_EPS) * g


def _rope(x, cos, sin):
    quarter = HEAD_DIM // 4
    lane = lax.broadcasted_iota(jnp.int32, x.shape, 1)
    first = (lane % (2 * quarter)) < quarter
    partner = jnp.where(first, pltpu.roll(x, HEAD_DIM - quarter, 1), pltpu.roll(x, quarter, 1))
    return x * cos + partner * sin


def _attn_kernel(q_ref, kp_ref, kc_ref, kn_ref, vp_ref, vc_ref, vn_ref, ck_ref, cv_ref,
                 cosq_ref, sinq_ref, cosp_ref, sinp_ref, cosn_ref, sinn_ref,
                 qg_ref, kg_ref, sink_ref, o_ref):
    n = pl.program_id(1)
    nb = pl.num_programs(1)
    blk = ATT_BLOCK
    hd = HEAD_DIM
    l_len = ck_ref.shape[1]
    scale = hd ** -0.5
    qg = qg_ref[...]
    kg = kg_ref[...]
    cos_q = cosq_ref[...]
    sin_q = sinq_ref[...]

    qi = lax.broadcasted_iota(jnp.int32, (blk, 3 * blk), 0)
    kj = lax.broadcasted_iota(jnp.int32, (blk, 3 * blk), 1)
    k_pos = (n - 1) * blk + kj
    ok = (jnp.abs(kj - blk - qi) <= WINDOW) & (k_pos >= 0) & (k_pos < nb * blk)
    ok = jnp.concatenate([ok] * ATT_GROUP, axis=0)

    for j in range(ATT_KV_HEADS):
        hs = slice(j * hd, (j + 1) * hd)
        keys = [
            _rope(_head_norm(kp_ref[0, :, hs], kg), cosp_ref[...], sinp_ref[...]),
            _rope(_head_norm(kc_ref[0, :, hs], kg), cos_q, sin_q),
            _rope(_head_norm(kn_ref[0, :, hs], kg), cosn_ref[...], sinn_ref[...]),
            _head_norm(ck_ref[0, :, hs], kg),
        ]
        k_all = jnp.concatenate(keys, axis=0).astype(BF16)
        v_all = jnp.concatenate(
            [vp_ref[0, :, hs], vc_ref[0, :, hs], vn_ref[0, :, hs], cv_ref[0, :, hs]], axis=0).astype(BF16)
        qs = []
        for g in range(ATT_GROUP):
            h = j * ATT_GROUP + g
            q = _rope(_head_norm(q_ref[0, :, h * hd:(h + 1) * hd], qg), cos_q, sin_q)
            qs.append((q * scale).astype(BF16))
        q_all = jnp.concatenate(qs, axis=0)
        s = lax.dot_general(q_all, k_all, _NT, preferred_element_type=F32)
        s_band = jnp.where(ok, s[:, :3 * blk], NEG_INF)
        s_ctx = s[:, 3 * blk:]
        sink = jnp.concatenate(
            [jnp.broadcast_to(sink_ref[:, j * ATT_GROUP + g:j * ATT_GROUP + g + 1], (blk, 1))
             for g in range(ATT_GROUP)], axis=0)
        m = jnp.maximum(jnp.maximum(jnp.max(s_band, axis=-1, keepdims=True),
                                    jnp.max(s_ctx, axis=-1, keepdims=True)), sink)
        e_band = jnp.exp(s_band - m)
        e_ctx = jnp.exp(s_ctx - m)
        denom = (jnp.sum(e_band, axis=-1, keepdims=True) + jnp.sum(e_ctx, axis=-1, keepdims=True)
                 + jnp.exp(sink - m))
        o = _dot(e_band.astype(BF16), v_all[:3 * blk]) + _dot(e_ctx.astype(BF16), v_all[3 * blk:])
        o = o / denom
        for g in range(ATT_GROUP):
            h = j * ATT_GROUP + g
            o_ref[0, :, h * hd:(h + 1) * hd] = o[g * blk:(g + 1) * blk, :].astype(o_ref.dtype)


def _window_attention(p_lat, p_ctx, q_norm_g, k_norm_g, sink):
    bsz, t_len, _ = p_lat.shape
    l_len = p_ctx.shape[1]
    nb = t_len // ATT_BLOCK
    kvw = ATT_KV_HEADS * HEAD_DIM
    q_blk = HG_COLS // ATT_WIDTH
    k_blk = (HG_COLS + ATT_WIDTH) // kvw
    v_blk = k_blk + 1
    cos, sin = _rope_tables(t_len)

    def prev(n):
        return jnp.maximum(n - 1, 0)

    def nxt(n):
        return jnp.minimum(n + 1, nb - 1)

    def kv_spec(blk_idx, shift):
        return pl.BlockSpec((1, ATT_BLOCK, kvw), lambda b, n: (b, shift(n), blk_idx))

    def tab_spec(shift):
        return pl.BlockSpec((ATT_BLOCK, HEAD_DIM), lambda b, n: (shift(n), 0))

    same = lambda n: n
    vec = pl.BlockSpec((1, HEAD_DIM), lambda b, n: (0, 0))
    return pl.pallas_call(
        _attn_kernel,
        grid=(bsz, nb),
        in_specs=[
            pl.BlockSpec((1, ATT_BLOCK, ATT_WIDTH), lambda b, n: (b, n, q_blk)),
            kv_spec(k_blk, prev), kv_spec(k_blk, same), kv_spec(k_blk, nxt),
            kv_spec(v_blk, prev), kv_spec(v_blk, same), kv_spec(v_blk, nxt),
            pl.BlockSpec((1, l_len, kvw), lambda b, n: (b, 0, k_blk)),
            pl.BlockSpec((1, l_len, kvw), lambda b, n: (b, 0, v_blk)),
            tab_spec(same), tab_spec(same), tab_spec(prev), tab_spec(prev), tab_spec(nxt), tab_spec(nxt),
            vec, vec,
            pl.BlockSpec((1, ATT_HEADS), lambda b, n: (0, 0)),
        ],
        out_specs=pl.BlockSpec((1, ATT_BLOCK, ATT_WIDTH), lambda b, n: (b, n, 0)),
        out_shape=jax.ShapeDtypeStruct((bsz, t_len, ATT_WIDTH), BF16),
        compiler_params=_params("parallel", "parallel"),
        name="window_gqa",
    )(p_lat, p_lat, p_lat, p_lat, p_lat, p_lat, p_lat, p_ctx, p_ctx,
      cos, sin, cos, sin, cos, sin,
      q_norm_g.reshape(1, HEAD_DIM), k_norm_g.reshape(1, HEAD_DIM), sink.reshape(1, ATT_HEADS))


def _outproj_kernel(x_ref, mod_ref, hg_ref, at_ref, wt_ref, wb_ref, o_ref):
    acc = _dot(hg_ref[...], wt_ref[...].astype(BF16)) + _dot(at_ref[...], wb_ref[...].astype(BF16))
    o_ref[...] = x_ref[...] + mod_ref[0, 2:3, :] * acc


def _out_projection(x2d, mod, hg2d, at2d, w_out, tokens_per_mod, tm=1024, tn=512):
    m, d = x2d.shape
    kh = hg2d.shape[1]
    tiles_per_mod = tokens_per_mod // tm
    return pl.pallas_call(
        _outproj_kernel,
        grid=(m // tm, d // tn),
        in_specs=[
            pl.BlockSpec((tm, tn), lambda i, j: (i, j)),
            pl.BlockSpec((1, N_MOD, tn), lambda i, j: (i // tiles_per_mod, 0, j)),
            pl.BlockSpec((tm, kh), lambda i, j: (i, 0)),
            pl.BlockSpec((tm, kh), lambda i, j: (i, 0)),
            pl.BlockSpec((kh, tn), lambda i, j: (0, j)),
            pl.BlockSpec((kh, tn), lambda i, j: (1, j)),
        ],
        out_specs=pl.BlockSpec((tm, tn), lambda i, j: (i, j)),
        out_shape=jax.ShapeDtypeStruct((m, d), F32),
        compiler_params=_params("parallel", "arbitrary"),
        name="out_proj",
    )(x2d, mod, hg2d, at2d, w_out, w_out)


def _ffn_kernel(x_ref, mod_ref, g_ref, wg_ref, wu_ref, wd_ref, o_ref, h_scr):
    @pl.when(pl.program_id(1) == 0)
    def _():
        _norm_modulate_into(x_ref, g_ref, mod_ref[0, 3:4, :], mod_ref[0, 4:5, :], h_scr)
        o_ref[...] = x_ref[...]

    h = h_scr[...]
    gate = _dot(h, wg_ref[...].astype(BF16))
    up = _dot(h, wu_ref[...].astype(BF16))
    act = (_silu(gate) * up).astype(BF16)
    o_ref[...] += mod_ref[0, 5:6, :] * _dot(act, wd_ref[...].astype(BF16))


def _ffn(x2d, mod, norm_g, w_gate_up, w_down, tokens_per_mod, tm=1024, tf=256):
    m, d = x2d.shape
    d_ff = w_down.shape[0]
    nf = d_ff // tf
    tiles_per_mod = tokens_per_mod // tm
    return pl.pallas_call(
        _ffn_kernel,
        grid=(m // tm, nf),
        in_specs=[
            pl.BlockSpec((tm, d), lambda i, j: (i, 0)),
            pl.BlockSpec((1, N_MOD, d), lambda i, j: (i // tiles_per_mod, 0, 0)),
            pl.BlockSpec((1, d), lambda i, j: (0, 0)),
            pl.BlockSpec((d, tf), lambda i, j: (0, j)),
            pl.BlockSpec((d, tf), lambda i, j: (0, nf + j)),
            pl.BlockSpec((tf, d), lambda i, j: (j, 0)),
        ],
        out_specs=pl.BlockSpec((tm, d), lambda i, j: (i, 0)),
        out_shape=jax.ShapeDtypeStruct((m, d), F32),
        scratch_shapes=[pltpu.VMEM((tm, d), BF16)],
        compiler_params=_params("parallel", "arbitrary"),
        name="swiglu_ffn",
    )(x2d, mod, norm_g.reshape(1, d), w_gate_up, w_gate_up, w_down)


def kernel(x, c, ctx, c_ctx, w_mod, b_mod, norm_mix_g, norm_ffn_g, w_in, hg_lb, hg_norm_g,
           q_norm_g, k_norm_g, attn_sink, w_out, w_gate_up, w_down):
    bsz, t_len, d = x.shape
    l_len = ctx.shape[1]
    depth = w_mod.shape[0]
    assert depth == 1, "single-layer block: the context stream never needs its own outputs"
    assert bsz + 1 <= MOD_ROWS

    c_rows = jnp.zeros((MOD_ROWS, d), F32).at[:bsz].set(c).at[bsz].set(c_ctx)
    mod = _modulation(c_rows, w_mod[0], b_mod[0]).reshape(MOD_ROWS, N_MOD, d)
    mod_lat = mod[:bsz]
    mod_ctx = mod[bsz:bsz + 1]

    p_lat = _in_projection(x.reshape(bsz * t_len, d), mod_lat, norm_mix_g[0], w_in[0],
                           tokens_per_mod=t_len, tm=1024).reshape(bsz, t_len, IN_COLS)
    p_ctx = _in_projection(ctx.reshape(bsz * l_len, d), mod_ctx, norm_mix_g[0], w_in[0],
                           tokens_per_mod=bsz * l_len, tm=bsz * l_len).reshape(bsz, l_len, IN_COLS)

    hg = _hgrn_mixer(p_lat, p_ctx, hg_lb, hg_norm_g[0])
    at = _window_attention(p_lat, p_ctx, q_norm_g[0], k_norm_g[0], attn_sink[0])

    x1 = _out_projection(x.reshape(bsz * t_len, d), mod_lat, hg.reshape(bsz * t_len, HG_WIDTH),
                         at.reshape(bsz * t_len, ATT_WIDTH), w_out[0], tokens_per_mod=t_len)
    x2 = _ffn(x1, mod_lat, norm_ffn_g[0], w_gate_up[0], w_down[0], tokens_per_mod=t_len)
    return x2.reshape(bsz, t_len, d)
```

```python
import functools

import jax
import jax.numpy as jnp
from jax import lax
from jax.experimental import pallas as pl
from jax.experimental.pallas import tpu as pltpu

F32 = jnp.float32
BF16 = jnp.bfloat16

GRID_W = 64
HG_HEADS = 8
HG_DK = 128
HG_DV = 128
HG_CHUNK = 32
ATT_HEADS = 8
ATT_KV_HEADS = 2
ATT_GROUP = ATT_HEADS // ATT_KV_HEADS
HEAD_DIM = 128
WINDOW = 128
ATT_BLOCK = 128
ROPE_THETA = 10000.0
N_MOD = 6
RMS_EPS = 1e-6
NEG_INF = -1e30
HG_WIDTH = HG_HEADS * HG_DV
ATT_WIDTH = ATT_HEADS * HEAD_DIM
HG_COLS = 3 * HG_HEADS * HG_DK + 2 * HG_HEADS * HG_DV
ATT_COLS = (ATT_HEADS + 2 * ATT_KV_HEADS) * HEAD_DIM
IN_COLS = HG_COLS + ATT_COLS

VMEM_LIMIT_BYTES = 58 * 1024 * 1024
MOD_ROWS = 8

_NT = (((1,), (1,)), ((), ()))
_TN = (((0,), (0,)), ((), ()))


def _dot(a, b):
    return jnp.dot(a, b, preferred_element_type=F32)


def _sigmoid(x):
    return 1.0 / (1.0 + jnp.exp(-x))


def _silu(x):
    return x * _sigmoid(x)


def _params(*sem):
    return pltpu.CompilerParams(dimension_semantics=sem, vmem_limit_bytes=VMEM_LIMIT_BYTES)


def _mod_kernel(c_ref, w_ref, b_ref, o_ref):
    a = _silu(c_ref[...]).astype(BF16)
    o_ref[...] = _dot(a, w_ref[...].astype(BF16)) + b_ref[...]


def _modulation(c_rows, w_mod, b_mod, tn=1024):
    d, n = w_mod.shape
    return pl.pallas_call(
        _mod_kernel,
        grid=(n // tn,),
        in_specs=[
            pl.BlockSpec((MOD_ROWS, d), lambda j: (0, 0)),
            pl.BlockSpec((d, tn), lambda j: (0, j)),
            pl.BlockSpec((1, tn), lambda j: (0, j)),
        ],
        out_specs=pl.BlockSpec((MOD_ROWS, tn), lambda j: (0, j)),
        out_shape=jax.ShapeDtypeStruct((MOD_ROWS, n), F32),
        compiler_params=_params("arbitrary"),
        name="adaln_mod",
    )(c_rows, w_mod, b_mod.reshape(1, n))


def _rope_tables(t_len):
    half = HEAD_DIM // 2
    nf = half // 2
    rows = t_len // GRID_W
    row = jnp.broadcast_to(jnp.arange(rows)[:, None], (rows, GRID_W)).reshape(-1)
    col = jnp.broadcast_to(jnp.arange(GRID_W)[None, :], (rows, GRID_W)).reshape(-1)
    inv_freq = ROPE_THETA ** (-jnp.arange(nf, dtype=F32) / nf)
    ang_r = row.astype(F32)[:, None] * inv_freq
    ang_c = col.astype(F32)[:, None] * inv_freq
    cos = jnp.concatenate([jnp.cos(ang_r)] * 2 + [jnp.cos(ang_c)] * 2, axis=-1)
    sin = jnp.concatenate([-jnp.sin(ang_r), jnp.sin(ang_r), -jnp.sin(ang_c), jnp.sin(ang_c)], axis=-1)
    return cos, sin


def _head_norm(x, g):
    return x * lax.rsqrt(jnp.mean(x * x, axis=-1, keepdims=True) + RMS_EPS) * g


def _rope(x, cos, sin):
    quarter = HEAD_DIM // 4
    lane = lax.broadcasted_iota(jnp.int32, x.shape, 1)
    first = (lane % (2 * quarter)) < quarter
    partner = jnp.where(first, pltpu.roll(x, HEAD_DIM - quarter, 1), pltpu.roll(x, quarter, 1))
    return x * cos + partner * sin


def _norm_modulate_into(x_ref, g_ref, shift, scale, h_scr, rows=128):
    tm = x_ref.shape[0]
    gain = g_ref[...] * (1.0 + scale)

    def body(r, carry):
        sl = pl.ds(pl.multiple_of(r * rows, rows), rows)
        x = x_ref[sl, :]
        ms = jnp.mean(x * x, axis=-1, keepdims=True)
        h_scr[sl, :] = (x * lax.rsqrt(ms + RMS_EPS) * gain + shift).astype(BF16)
        return carry

    lax.fori_loop(0, tm // rows, body, 0)


def _inproj_kernel(x_ref, mod_ref, g_ref, w_ref, cos_ref, sin_ref, qg_ref, kg_ref, o_ref, h_scr, *, rope):
    j = pl.program_id(1)
    tn = o_ref.shape[1]
    hd = HEAD_DIM
    q_tile0 = HG_COLS // tn
    kv_tile = (HG_COLS + ATT_WIDTH) // tn

    @pl.when(j == 0)
    def _():
        _norm_modulate_into(x_ref, g_ref, mod_ref[0, 0:1, :], mod_ref[0, 1:2, :], h_scr)

    acc = _dot(h_scr[...], w_ref[...].astype(BF16))

    def qk_prep(cols, g_row, scale):
        y = _head_norm(acc[:, cols], g_row)
        if rope:
            y = _rope(y, cos_ref[...], sin_ref[...])
        o_ref[:, cols] = (y * scale).astype(o_ref.dtype)

    @pl.when(j < q_tile0)
    def _():
        o_ref[...] = acc.astype(o_ref.dtype)

    @pl.when((j >= q_tile0) & (j < kv_tile))
    def _():
        for h in range(tn // hd):
            qk_prep(slice(h * hd, (h + 1) * hd), qg_ref[...], hd ** -0.5)

    @pl.when(j == kv_tile)
    def _():
        for h in range(ATT_KV_HEADS):
            qk_prep(slice(h * hd, (h + 1) * hd), kg_ref[...], 1.0)
        o_ref[:, ATT_KV_HEADS * hd:] = acc[:, ATT_KV_HEADS * hd:].astype(o_ref.dtype)


def _in_projection(x2d, mod, norm_g, w_in, q_norm_g, k_norm_g, cos, sin, tokens_per_mod, tm, rope, tn=512):
    m, d = x2d.shape
    n = w_in.shape[1]
    assert HG_COLS % tn == 0 and ATT_WIDTH % tn == 0 and 2 * ATT_KV_HEADS * HEAD_DIM == tn
    tiles_per_mod = tokens_per_mod // tm
    tiles_per_seq = cos.shape[0] // tm
    vec = pl.BlockSpec((1, HEAD_DIM), lambda i, j: (0, 0))
    tab = pl.BlockSpec((tm, HEAD_DIM), lambda i, j: (i % tiles_per_seq, 0))
    return pl.pallas_call(
        functools.partial(_inproj_kernel, rope=rope),
        grid=(m // tm, n // tn),
        in_specs=[
            pl.BlockSpec((tm, d), lambda i, j: (i, 0)),
            pl.BlockSpec((1, N_MOD, d), lambda i, j: (i // tiles_per_mod, 0, 0)),
            pl.BlockSpec((1, d), lambda i, j: (0, 0)),
            pl.BlockSpec((d, tn), lambda i, j: (0, j)),
            tab, tab, vec, vec,
        ],
        out_specs=pl.BlockSpec((tm, tn), lambda i, j: (i, j)),
        out_shape=jax.ShapeDtypeStruct((m, n), BF16),
        scratch_shapes=[pltpu.VMEM((tm, d), BF16)],
        compiler_params=_params("parallel", "arbitrary"),
        name="in_proj",
    )(x2d, mod, norm_g.reshape(1, d), w_in, cos, sin,
      q_norm_g.reshape(1, HEAD_DIM), k_norm_g.reshape(1, HEAD_DIM))


HG_BLOCK = 256


def _chunk_masks():
    r = jnp.arange(HG_BLOCK)[:, None]
    c = jnp.arange(HG_BLOCK)[None, :]
    same = (r // HG_CHUNK) == (c // HG_CHUNK)
    return (same & (r >= c)), (same & (r <= c)), same


def _hgrn_kernel(q_ref, ff_ref, fb_ref, i_ref, g_ref, cff_ref, cfb_ref, ci_ref, lb_ref, ng_ref,
                 cumf_ref, cumb_ref, mskf_ref, mskb_ref, dmask_ref, o_ref, of_scr, ob_scr):
    t_len = q_ref.shape[1]
    l_len = ci_ref.shape[1]
    c = HG_CHUNK
    blk = HG_BLOCK
    nch = blk // c
    n_blk = t_len // blk
    assert l_len == blk

    a = lb_ref[...]
    e = jnp.exp(a - jnp.max(a, axis=0, keepdims=True))
    lb = e[0] / jnp.sum(e, axis=0)

    def block_step(st, f_pre, v, q_pre, lb_d, cum_ref, msk_ref, reverse):
        f = lb_d + (1.0 - lb_d) * _sigmoid(f_pre.astype(F32))
        key = 1.0 - f
        log_f = jnp.log(f)
        hi = log_f.astype(BF16)
        r1 = log_f - hi.astype(F32)
        mid = r1.astype(BF16)
        lo = (r1 - mid.astype(F32)).astype(BF16)
        b = _dot(cum_ref[...], jnp.concatenate([hi, mid, lo], axis=1))
        b = b[:, 0:128] + b[:, 128:256] + b[:, 256:384]
        b3 = b.reshape(nch, c, HG_DK)
        tot = b3[:, 0:1, :] if reverse else b3[:, c - 1:c, :]
        b_tot = jnp.broadcast_to(tot, (nch, c, HG_DK)).reshape(blk, HG_DK)
        dec = jnp.exp(tot)
        k_dec = (key * jnp.exp(b_tot - b)).astype(BF16)
        v_t = v.astype(F32).T.astype(BF16)
        upd = []
        for h in range(blk // 128):
            kd = k_dec[h * 128:(h + 1) * 128, :]
            kd4 = jnp.concatenate([kd] * 4, axis=1) * dmask_ref[...]
            u4 = _dot(v_t[:, h * 128:(h + 1) * 128], kd4)
            upd += [u4[:, j * HG_DK:(j + 1) * HG_DK] for j in range(4)]
        if q_pre is not None:
            q_dec = (_silu(q_pre.astype(F32)) * jnp.exp(b)).astype(BF16)
            k_inv = (key * jnp.exp(-b)).astype(BF16)
            s = lax.dot_general(q_dec, k_inv, _NT, preferred_element_type=F32)
            s = jnp.where(msk_ref[...] != 0.0, s, 0.0).astype(BF16)
            o_intra = _dot(s, v.astype(BF16))
        outs = [None] * nch
        for n in (range(nch - 1, -1, -1) if reverse else range(nch)):
            if q_pre is not None:
                rows = slice(n * c, (n + 1) * c)
                outs[n] = o_intra[rows, :] + lax.dot_general(
                    q_dec[rows, :], st.astype(BF16), _NT, preferred_element_type=F32)
            st = st * dec[n] + upd[n]
        return (jnp.concatenate(outs, axis=0) if q_pre is not None else None), st

    lb_f = lb[0:1, :]
    lb_b = lb[1:2, :]
    zero = jnp.zeros((HG_DV, HG_DK), F32)
    cv = ci_ref[0]
    _, st_f = block_step(zero, cff_ref[0], cv, None, lb_f, cumf_ref, mskf_ref, False)
    _, st_b = block_step(zero, cfb_ref[0], cv, None, lb_b, cumb_ref, mskb_ref, True)

    def lat_body(r, carry):
        st_f, st_b = carry
        sf = pl.ds(pl.multiple_of(r * blk, blk), blk)
        sb = pl.ds(pl.multiple_of((n_blk - 1 - r) * blk, blk), blk)
        o, st_f = block_step(st_f, ff_ref[0, sf, :], i_ref[0, sf, :], q_ref[0, sf, :], lb_f,
                             cumf_ref, mskf_ref, False)
        of_scr[sf, :] = o
        o, st_b = block_step(st_b, fb_ref[0, sb, :], i_ref[0, sb, :], q_ref[0, sb, :], lb_b,
                             cumb_ref, mskb_ref, True)
        ob_scr[sb, :] = o
        return st_f, st_b

    lax.fori_loop(0, n_blk, lat_body, (st_f, st_b), unroll=2)

    ng = ng_ref[...]

    def fin_body(r, carry):
        sl = pl.ds(pl.multiple_of(r * blk, blk), blk)
        o = of_scr[sl, :] + ob_scr[sl, :]
        y = o * lax.rsqrt(jnp.mean(o * o, axis=-1, keepdims=True) + RMS_EPS) * ng
        o_ref[0, sl, :] = (y * _silu(g_ref[0, sl, :].astype(F32))).astype(o_ref.dtype)
        return carry

    lax.fori_loop(0, n_blk, fin_body, 0)


def _hgrn_mixer(p_lat, p_ctx, hg_lb, norm_g):
    bsz, t_len, _ = p_lat.shape
    l_len = p_ctx.shape[1]
    w = HG_DK

    def lat(group):
        return pl.BlockSpec((1, t_len, w), lambda b, h: (b, 0, group * HG_HEADS + h))

    def ctx(group):
        return pl.BlockSpec((1, l_len, w), lambda b, h: (b, 0, group * HG_HEADS + h))

    def const(shape):
        return pl.BlockSpec(shape, lambda b, h: (0,) * len(shape))

    lower, upper, same = _chunk_masks()
    diag4 = jnp.tile(same[:128, ::HG_CHUNK][:, :4, None], (1, 1, HG_DK)).reshape(128, 4 * HG_DK)
    seq_f32 = pltpu.VMEM((t_len, w), F32)
    return pl.pallas_call(
        _hgrn_kernel,
        grid=(bsz, HG_HEADS),
        in_specs=[
            lat(0), lat(1), lat(2), lat(3), lat(4),
            ctx(1), ctx(2), ctx(3),
            pl.BlockSpec((hg_lb.shape[0], 2, w), lambda b, h: (0, 0, h)),
            const((1, HG_DV)),
            const((HG_BLOCK, HG_BLOCK)), const((HG_BLOCK, HG_BLOCK)),
            const((HG_BLOCK, HG_BLOCK)), const((HG_BLOCK, HG_BLOCK)),
            const((128, 4 * HG_DK)),
        ],
        out_specs=pl.BlockSpec((1, t_len, HG_DV), lambda b, h: (b, 0, h)),
        out_shape=jax.ShapeDtypeStruct((bsz, t_len, HG_WIDTH), BF16),
        scratch_shapes=[seq_f32, seq_f32],
        compiler_params=_params("parallel", "parallel"),
        name="hgrn2_mixer",
    )(p_lat, p_lat, p_lat, p_lat, p_lat, p_ctx, p_ctx, p_ctx, hg_lb, norm_g.reshape(1, HG_DV),
      lower.astype(BF16), upper.astype(BF16), lower.astype(F32), upper.astype(F32), diag4.astype(BF16))


def _attn_kernel(q_ref, kp_ref, kc_ref, kn_ref, vp_ref, vc_ref, vn_ref, ck_ref, cv_ref, sink_ref, o_ref):
    n = pl.program_id(1)
    nb = pl.num_programs(1)
    blk = ATT_BLOCK
    hd = HEAD_DIM
    l_len = ck_ref.shape[1]
    n_keys = 3 * blk + l_len

    qi = lax.broadcasted_iota(jnp.int32, (blk, n_keys), 0)
    kj = lax.broadcasted_iota(jnp.int32, (blk, n_keys), 1)
    k_pos = (n - 1) * blk + kj
    ok = (kj >= 3 * blk) | ((jnp.abs(kj - blk - qi) <= WINDOW) & (k_pos >= 0) & (k_pos < nb * blk))
    bias = jnp.where(ok, 0.0, NEG_INF)
    ones = jnp.ones((n_keys, hd), BF16)

    heads = range(ATT_HEADS)
    k_all, v_aug = [], []
    for j in range(ATT_KV_HEADS):
        hs = slice(j * hd, (j + 1) * hd)
        k_all.append(jnp.concatenate(
            [kp_ref[0, :, hs], kc_ref[0, :, hs], kn_ref[0, :, hs], ck_ref[0, :, hs]], axis=0))
        v_all = jnp.concatenate([vp_ref[0, :, hs], vc_ref[0, :, hs], vn_ref[0, :, hs], cv_ref[0, :, hs]], axis=0)
        v_aug.append(jnp.concatenate([v_all, ones], axis=1))
    s = [lax.dot_general(q_ref[0, :, h * hd:(h + 1) * hd], k_all[h // ATT_GROUP], _NT,
                         preferred_element_type=F32) + bias for h in heads]
    sink = [sink_ref[:, h:h + 1] for h in heads]
    m = [jnp.maximum(jnp.max(s[h], axis=-1, keepdims=True), sink[h]) for h in heads]
    e = [jnp.exp(s[h] - m[h]).astype(BF16) for h in heads]
    oa = [_dot(e[h], v_aug[h // ATT_GROUP]) for h in heads]
    for h in heads:
        o = oa[h][:, :hd] / (oa[h][:, hd:] + jnp.exp(sink[h] - m[h]))
        o_ref[0, :, h * hd:(h + 1) * hd] = o.astype(o_ref.dtype)


def _window_attention(p_lat, p_ctx, sink):
    bsz, t_len, _ = p_lat.shape
    l_len = p_ctx.shape[1]
    nb = t_len // ATT_BLOCK
    kvw = ATT_KV_HEADS * HEAD_DIM
    q_blk = HG_COLS // ATT_WIDTH
    k_blk = (HG_COLS + ATT_WIDTH) // kvw
    v_blk = k_blk + 1

    def prev(n):
        return jnp.maximum(n - 1, 0)

    def nxt(n):
        return jnp.minimum(n + 1, nb - 1)

    def kv_spec(blk_idx, shift):
        return pl.BlockSpec((1, ATT_BLOCK, kvw), lambda b, n: (b, shift(n), blk_idx))

    same = lambda n: n
    return pl.pallas_call(
        _attn_kernel,
        grid=(bsz, nb),
        in_specs=[
            pl.BlockSpec((1, ATT_BLOCK, ATT_WIDTH), lambda b, n: (b, n, q_blk)),
            kv_spec(k_blk, prev), kv_spec(k_blk, same), kv_spec(k_blk, nxt),
            kv_spec(v_blk, prev), kv_spec(v_blk, same), kv_spec(v_blk, nxt),
            pl.BlockSpec((1, l_len, kvw), lambda b, n: (b, 0, k_blk)),
            pl.BlockSpec((1, l_len, kvw), lambda b, n: (b, 0, v_blk)),
            pl.BlockSpec((1, ATT_HEADS), lambda b, n: (0, 0)),
        ],
        out_specs=pl.BlockSpec((1, ATT_BLOCK, ATT_WIDTH), lambda b, n: (b, n, 0)),
        out_shape=jax.ShapeDtypeStruct((bsz, t_len, ATT_WIDTH), BF16),
        compiler_params=_params("parallel", "parallel"),
        name="window_gqa",
    )(p_lat, p_lat, p_lat, p_lat, p_lat, p_lat, p_lat, p_ctx, p_ctx, sink.reshape(1, ATT_HEADS))


def _outproj_kernel(x_ref, mod_ref, hg_ref, at_ref, wt_ref, wb_ref, o_ref):
    acc = _dot(hg_ref[...], wt_ref[...].astype(BF16)) + _dot(at_ref[...], wb_ref[...].astype(BF16))
    o_ref[...] = x_ref[...] + mod_ref[0, 2:3, :] * acc


def _out_projection(x2d, mod, hg2d, at2d, w_out, tokens_per_mod, tm=1024, tn=512):
    m, d = x2d.shape
    kh = hg2d.shape[1]
    tiles_per_mod = tokens_per_mod // tm
    return pl.pallas_call(
        _outproj_kernel,
        grid=(m // tm, d // tn),
        in_specs=[
            pl.BlockSpec((tm, tn), lambda i, j: (i, j)),
            pl.BlockSpec((1, N_MOD, tn), lambda i, j: (i // tiles_per_mod, 0, j)),
            pl.BlockSpec((tm, kh), lambda i, j: (i, 0)),
            pl.BlockSpec((tm, kh), lambda i, j: (i, 0)),
            pl.BlockSpec((kh, tn), lambda i, j: (0, j)),
            pl.BlockSpec((kh, tn), lambda i, j: (1, j)),
        ],
        out_specs=pl.BlockSpec((tm, tn), lambda i, j: (i, j)),
        out_shape=jax.ShapeDtypeStruct((m, d), F32),
        compiler_params=_params("parallel", "arbitrary"),
        name="out_proj",
    )(x2d, mod, hg2d, at2d, w_out, w_out)


def _ffn_kernel(x_ref, mod_ref, g_ref, wg_ref, wu_ref, wd_ref, o_ref, h_scr):
    @pl.when(pl.program_id(1) == 0)
    def _():
        _norm_modulate_into(x_ref, g_ref, mod_ref[0, 3:4, :], mod_ref[0, 4:5, :], h_scr)
        o_ref[...] = x_ref[...]

    h = h_scr[...]
    gate = _dot(h, wg_ref[...].astype(BF16))
    up = _dot(h, wu_ref[...].astype(BF16))
    act = (_silu(gate) * up).astype(BF16)
    o_ref[...] += mod_ref[0, 5:6, :] * _dot(act, wd_ref[...].astype(BF16))


def _ffn(x2d, mod, norm_g, w_gate_up, w_down, tokens_per_mod, tm=1024, tf=256):
    m, d = x2d.shape
    d_ff = w_down.shape[0]
    nf = d_ff // tf
    tiles_per_mod = tokens_per_mod // tm
    return pl.pallas_call(
        _ffn_kernel,
        grid=(m // tm, nf),
        in_specs=[
            pl.BlockSpec((tm, d), lambda i, j: (i, 0)),
            pl.BlockSpec((1, N_MOD, d), lambda i, j: (i // tiles_per_mod, 0, 0)),
            pl.BlockSpec((1, d), lambda i, j: (0, 0)),
            pl.BlockSpec((d, tf), lambda i, j: (0, j)),
            pl.BlockSpec((d, tf), lambda i, j: (0, nf + j)),
            pl.BlockSpec((tf, d), lambda i, j: (j, 0)),
        ],
        out_specs=pl.BlockSpec((tm, d), lambda i, j: (i, 0)),
        out_shape=jax.ShapeDtypeStruct((m, d), F32),
        scratch_shapes=[pltpu.VMEM((tm, d), BF16)],
        compiler_params=_params("parallel", "arbitrary"),
        name="swiglu_ffn",
    )(x2d, mod, norm_g.reshape(1, d), w_gate_up, w_gate_up, w_down)


def kernel(x, c, ctx, c_ctx, w_mod, b_mod, norm_mix_g, norm_ffn_g, w_in, hg_lb, hg_norm_g,
           q_norm_g, k_norm_g, attn_sink, w_out, w_gate_up, w_down):
    bsz, t_len, d = x.shape
    l_len = ctx.shape[1]
    depth = w_mod.shape[0]
    assert depth == 1, "single-layer block: the context stream never needs its own outputs"
    assert bsz + 1 <= MOD_ROWS

    c_rows = jnp.zeros((MOD_ROWS, d), F32).at[:bsz].set(c).at[bsz].set(c_ctx)
    mod = _modulation(c_rows, w_mod[0], b_mod[0]).reshape(MOD_ROWS, N_MOD, d)
    mod_lat = mod[:bsz]
    mod_ctx = mod[bsz:bsz + 1]

    cos, sin = _rope_tables(t_len)
    p_lat = _in_projection(x.reshape(bsz * t_len, d), mod_lat, norm_mix_g[0], w_in[0], q_norm_g[0], k_norm_g[0],
                           cos, sin, tokens_per_mod=t_len, tm=1024, rope=True).reshape(bsz, t_len, IN_COLS)
    p_ctx = _in_projection(ctx.reshape(bsz * l_len, d), mod_ctx, norm_mix_g[0], w_in[0], q_norm_g[0], k_norm_g[0],
                           cos, sin, tokens_per_mod=bsz * l_len, tm=bsz * l_len,
                           rope=False).reshape(bsz, l_len, IN_COLS)

    hg = _hgrn_mixer(p_lat, p_ctx, hg_lb, hg_norm_g[0])
    at = _window_attention(p_lat, p_ctx, attn_sink[0])

    x1 = _out_projection(x.reshape(bsz * t_len, d), mod_lat, hg.reshape(bsz * t_len, HG_WIDTH),
                         at.reshape(bsz * t_len, ATT_WIDTH), w_out[0], tokens_per_mod=t_len)
    x2 = _ffn(x1, mod_lat, norm_ffn_g[0], w_gate_up[0], w_down[0], tokens_per_mod=t_len)
    return x2.reshape(bsz, t_len, d)
```

```python
import functools

import jax
import jax.numpy as jnp
from jax import lax
from jax.experimental import pallas as pl
from jax.experimental.pallas import tpu as pltpu

F32 = jnp.float32
BF16 = jnp.bfloat16

GRID_W = 64
HG_HEADS = 8
HG_DK = 128
HG_DV = 128
HG_CHUNK = 32
ATT_HEADS = 8
ATT_KV_HEADS = 2
ATT_GROUP = ATT_HEADS // ATT_KV_HEADS
HEAD_DIM = 128
WINDOW = 128
ATT_BLOCK = 128
ROPE_THETA = 10000.0
N_MOD = 6
RMS_EPS = 1e-6
NEG_INF = -1e30
HG_WIDTH = HG_HEADS * HG_DV
ATT_WIDTH = ATT_HEADS * HEAD_DIM
HG_COLS = 3 * HG_HEADS * HG_DK + 2 * HG_HEADS * HG_DV
ATT_COLS = (ATT_HEADS + 2 * ATT_KV_HEADS) * HEAD_DIM
IN_COLS = HG_COLS + ATT_COLS

VMEM_LIMIT_BYTES = 58 * 1024 * 1024
MOD_ROWS = 8

_NT = (((1,), (1,)), ((), ()))
_TN = (((0,), (0,)), ((), ()))


def _dot(a, b):
    return jnp.dot(a, b, preferred_element_type=F32)


def _sigmoid(x):
    return 1.0 / (1.0 + jnp.exp(-x))


def _silu(x):
    return x * _sigmoid(x)


def _params(*sem):
    return pltpu.CompilerParams(dimension_semantics=sem, vmem_limit_bytes=VMEM_LIMIT_BYTES)


def _mod_kernel(c_ref, w_ref, b_ref, o_ref):
    a = _silu(c_ref[...]).astype(BF16)
    o_ref[...] = _dot(a, w_ref[...].astype(BF16)) + b_ref[...]


def _modulation(c_rows, w_mod, b_mod, tn=1024):
    d, n = w_mod.shape
    return pl.pallas_call(
        _mod_kernel,
        grid=(n // tn,),
        in_specs=[
            pl.BlockSpec((MOD_ROWS, d), lambda j: (0, 0)),
            pl.BlockSpec((d, tn), lambda j: (0, j)),
            pl.BlockSpec((1, tn), lambda j: (0, j)),
        ],
        out_specs=pl.BlockSpec((MOD_ROWS, tn), lambda j: (0, j)),
        out_shape=jax.ShapeDtypeStruct((MOD_ROWS, n), F32),
        compiler_params=_params("arbitrary"),
        name="adaln_mod",
    )(c_rows, w_mod, b_mod.reshape(1, n))


def _rope_tables(t_len):
    half = HEAD_DIM // 2
    nf = half // 2
    rows = t_len // GRID_W
    row = jnp.broadcast_to(jnp.arange(rows)[:, None], (rows, GRID_W)).reshape(-1)
    col = jnp.broadcast_to(jnp.arange(GRID_W)[None, :], (rows, GRID_W)).reshape(-1)
    inv_freq = ROPE_THETA ** (-jnp.arange(nf, dtype=F32) / nf)
    ang_r = row.astype(F32)[:, None] * inv_freq
    ang_c = col.astype(F32)[:, None] * inv_freq
    cos = jnp.concatenate([jnp.cos(ang_r)] * 2 + [jnp.cos(ang_c)] * 2, axis=-1)
    sin = jnp.concatenate([-jnp.sin(ang_r), jnp.sin(ang_r), -jnp.sin(ang_c), jnp.sin(ang_c)], axis=-1)
    return cos, sin


def _head_norm(x, g):
    return x * lax.rsqrt(jnp.mean(x * x, axis=-1, keepdims=True) + RMS_EPS) * g


def _rope(x, cos, sin):
    quarter = HEAD_DIM // 4
    lane = lax.broadcasted_iota(jnp.int32, x.shape, 1)
    first = (lane % (2 * quarter)) < quarter
    partner = jnp.where(first, pltpu.roll(x, HEAD_DIM - quarter, 1), pltpu.roll(x, quarter, 1))
    return x * cos + partner * sin


def _norm_modulate_into(x_ref, g_ref, shift, scale, h_scr, rows=128):
    tm = x_ref.shape[0]
    gain = g_ref[...] * (1.0 + scale)

    def body(r, carry):
        sl = pl.ds(pl.multiple_of(r * rows, rows), rows)
        x = x_ref[sl, :]
        ms = jnp.mean(x * x, axis=-1, keepdims=True)
        h_scr[sl, :] = (x * lax.rsqrt(ms + RMS_EPS) * gain + shift).astype(BF16)
        return carry

    lax.fori_loop(0, tm // rows, body, 0)


def _inproj_kernel(tiles_ref, x_ref, mod_ref, g_ref, w_ref, cos_ref, sin_ref, qg_ref, kg_ref, o_ref, h_scr, *,
                   rope):
    j = pl.program_id(1)
    t = tiles_ref[j]
    tn = o_ref.shape[1]
    hd = HEAD_DIM
    q_tile0 = HG_COLS // tn
    kv_tile = (HG_COLS + ATT_WIDTH) // tn

    @pl.when(j == 0)
    def _():
        _norm_modulate_into(x_ref, g_ref, mod_ref[0, 0:1, :], mod_ref[0, 1:2, :], h_scr)

    acc = _dot(h_scr[...], w_ref[...].astype(BF16))

    def qk_prep(cols, g_row, scale):
        y = _head_norm(acc[:, cols], g_row)
        if rope:
            y = _rope(y, cos_ref[...], sin_ref[...])
        o_ref[:, cols] = (y * scale).astype(o_ref.dtype)

    @pl.when(t < q_tile0)
    def _():
        o_ref[...] = acc.astype(o_ref.dtype)

    @pl.when((t >= q_tile0) & (t < kv_tile))
    def _():
        for h in range(tn // hd):
            qk_prep(slice(h * hd, (h + 1) * hd), qg_ref[...], hd ** -0.5)

    @pl.when(t == kv_tile)
    def _():
        for h in range(ATT_KV_HEADS):
            qk_prep(slice(h * hd, (h + 1) * hd), kg_ref[...], 1.0)
        o_ref[:, ATT_KV_HEADS * hd:] = acc[:, ATT_KV_HEADS * hd:].astype(o_ref.dtype)


IN_TILE = 512


def _tiles_covering(col_ranges):
    return tuple(t for t in range(IN_COLS // IN_TILE)
                 if any(lo < (t + 1) * IN_TILE and t * IN_TILE < hi for lo, hi in col_ranges))


def _packed_col(col_tiles, col):
    return col_tiles.index(col // IN_TILE) * IN_TILE + col % IN_TILE


def _in_projection(x2d, mod, norm_g, w_in, q_norm_g, k_norm_g, cos, sin, tokens_per_mod, tm, rope, col_tiles):
    m, d = x2d.shape
    tn = IN_TILE
    assert HG_COLS % tn == 0 and ATT_WIDTH % tn == 0 and 2 * ATT_KV_HEADS * HEAD_DIM == tn
    tiles_per_mod = tokens_per_mod // tm
    tiles_per_seq = cos.shape[0] // tm
    vec = pl.BlockSpec((1, HEAD_DIM), lambda i, j, tiles: (0, 0))
    tab = pl.BlockSpec((tm, HEAD_DIM), lambda i, j, tiles: (i % tiles_per_seq, 0))
    grid_spec = pltpu.PrefetchScalarGridSpec(
        num_scalar_prefetch=1,
        grid=(m // tm, len(col_tiles)),
        in_specs=[
            pl.BlockSpec((tm, d), lambda i, j, tiles: (i, 0)),
            pl.BlockSpec((1, N_MOD, d), lambda i, j, tiles: (i // tiles_per_mod, 0, 0)),
            pl.BlockSpec((1, d), lambda i, j, tiles: (0, 0)),
            pl.BlockSpec((d, tn), lambda i, j, tiles: (0, tiles[j])),
            tab, tab, vec, vec,
        ],
        out_specs=pl.BlockSpec((tm, tn), lambda i, j, tiles: (i, j)),
        scratch_shapes=[pltpu.VMEM((tm, d), BF16)],
    )
    return pl.pallas_call(
        functools.partial(_inproj_kernel, rope=rope),
        grid_spec=grid_spec,
        out_shape=jax.ShapeDtypeStruct((m, len(col_tiles) * tn), BF16),
        compiler_params=_params("parallel", "arbitrary"),
        name="in_proj",
    )(jnp.asarray(col_tiles, jnp.int32), x2d, mod, norm_g.reshape(1, d), w_in, cos, sin,
      q_norm_g.reshape(1, HEAD_DIM), k_norm_g.reshape(1, HEAD_DIM))


HG_BLOCK = 256


def _chunk_masks():
    r = jnp.arange(HG_BLOCK)[:, None]
    c = jnp.arange(HG_BLOCK)[None, :]
    same = (r // HG_CHUNK) == (c // HG_CHUNK)
    return (same & (r >= c)), (same & (r <= c)), same


def _hgrn_kernel(q_ref, ff_ref, fb_ref, i_ref, g_ref, cff_ref, cfb_ref, ci_ref, lb_ref, ng_ref,
                 cumf_ref, cumb_ref, dmask_ref, o_ref, of_scr, ob_scr):
    t_len = q_ref.shape[1]
    l_len = ci_ref.shape[1]
    c = HG_CHUNK
    blk = HG_BLOCK
    nch = blk // c
    n_blk = t_len // blk
    assert l_len == blk

    a = lb_ref[...]
    e = jnp.exp(a - jnp.max(a, axis=0, keepdims=True))
    lb = e[0] / jnp.sum(e, axis=0)

    def gates(f_pre, lb_d):
        f = lb_d + (1.0 - lb_d) * _sigmoid(f_pre.astype(F32))
        log_f = jnp.log(f)
        hi = log_f.astype(BF16)
        lo = (log_f - hi.astype(F32)).astype(BF16)
        return 1.0 - f, jnp.concatenate([hi, lo], axis=1)

    def decays(parts, cum_ref, reverse):
        b = _dot(cum_ref[...], parts)
        b = b[:, 0:128] + b[:, 128:256]
        b3 = b.reshape(nch, c, HG_DK)
        tot = b3[:, 0:1, :] if reverse else b3[:, c - 1:c, :]
        b_tot = jnp.broadcast_to(tot, (nch, c, HG_DK)).reshape(blk, HG_DK)
        return b, b_tot, jnp.exp(tot)

    def increments(key, b, b_tot, v):
        k_dec = (key * jnp.exp(b_tot - b)).astype(BF16)
        v_t = v.astype(F32).T.astype(BF16)
        upd = []
        for h in range(blk // 128):
            kd = k_dec[h * 128:(h + 1) * 128, :]
            kd4 = jnp.concatenate([kd] * 4, axis=1) * dmask_ref[...]
            u4 = _dot(v_t[:, h * 128:(h + 1) * 128], kd4)
            upd += [u4[:, j * HG_DK:(j + 1) * HG_DK] for j in range(4)]
        return upd

    def scores(q_pre, key, b):
        q_dec = (_silu(q_pre.astype(F32)) * jnp.exp(b)).astype(BF16)
        k_inv = (key * jnp.exp(-b)).astype(BF16)
        return q_dec, lax.dot_general(q_dec, k_inv, _NT, preferred_element_type=F32)

    def intra(s, v, cum_ref):
        return _dot(s.astype(BF16) * cum_ref[...], v.astype(BF16))

    def recur(st, dec, upd, q_dec, o_intra, reverse):
        outs = [None] * nch
        for n in (range(nch - 1, -1, -1) if reverse else range(nch)):
            if q_dec is not None:
                rows = slice(n * c, (n + 1) * c)
                outs[n] = o_intra[rows, :] + _dot(q_dec[rows, :], st.T.astype(BF16))
            st = st * dec[n] + upd[n]
        return (jnp.concatenate(outs, axis=0) if q_dec is not None else None), st

    lb_d = (lb[0:1, :], lb[1:2, :])
    cum = (cumf_ref, cumb_ref)
    f_lat = (ff_ref, fb_ref)
    o_scr = (of_scr, ob_scr)

    cv = ci_ref[0]
    st = []
    for d, f_ref in enumerate((cff_ref, cfb_ref)):
        key, parts = gates(f_ref[0], lb_d[d])
        b, b_tot, dec = decays(parts, cum[d], d == 1)
        _, st_d = recur(jnp.zeros((HG_DV, HG_DK), F32), dec, increments(key, b, b_tot, cv), None, None, d == 1)
        st.append(st_d)

    blocks_per_step = 2

    def lat_body(r, carry):
        chains = []
        for u in range(blocks_per_step):
            up = r * blocks_per_step + u
            chains.append((0, pl.ds(pl.multiple_of(up * blk, blk), blk)))
            chains.append((1, pl.ds(pl.multiple_of((n_blk - 1 - up) * blk, blk), blk)))
        g = [gates(f_lat[d][0, sl, :], lb_d[d]) for d, sl in chains]
        dk = [decays(g[i][1], cum[d], d == 1) for i, (d, sl) in enumerate(chains)]
        v = [i_ref[0, sl, :] for d, sl in chains]
        upd = [increments(g[i][0], dk[i][0], dk[i][1], v[i]) for i in range(len(chains))]
        qs = [scores(q_ref[0, sl, :], g[i][0], dk[i][0]) for i, (d, sl) in enumerate(chains)]
        o_in = [intra(qs[i][1], v[i], cum[d]) for i, (d, sl) in enumerate(chains)]
        st = list(carry)
        for i, (d, sl) in enumerate(chains):
            o, st[d] = recur(st[d], dk[i][2], upd[i], qs[i][0], o_in[i], d == 1)
            o_scr[d][sl, :] = o
        return tuple(st)

    lax.fori_loop(0, n_blk // blocks_per_step, lat_body, tuple(st))

    ng = ng_ref[...]

    def fin_body(r, carry):
        sl = pl.ds(pl.multiple_of(r * blk, blk), blk)
        o = of_scr[sl, :] + ob_scr[sl, :]
        y = o * lax.rsqrt(jnp.mean(o * o, axis=-1, keepdims=True) + RMS_EPS) * ng
        o_ref[0, sl, :] = (y * _silu(g_ref[0, sl, :].astype(F32))).astype(o_ref.dtype)
        return carry

    lax.fori_loop(0, n_blk, fin_body, 0)


def _hgrn_mixer(p_lat, p_ctx, ctx_tiles, hg_lb, norm_g):
    bsz, t_len, _ = p_lat.shape
    l_len = p_ctx.shape[1]
    w = HG_DK

    def lat(group):
        return pl.BlockSpec((1, t_len, w), lambda b, h: (b, 0, group * HG_HEADS + h))

    def ctx(group):
        blk0 = _packed_col(ctx_tiles, group * HG_HEADS * w) // w
        return pl.BlockSpec((1, l_len, w), lambda b, h: (b, 0, blk0 + h))

    def const(shape):
        return pl.BlockSpec(shape, lambda b, h: (0,) * len(shape))

    lower, upper, same = _chunk_masks()
    diag4 = jnp.tile(same[:128, ::HG_CHUNK][:, :4, None], (1, 1, HG_DK)).reshape(128, 4 * HG_DK)
    seq_f32 = pltpu.VMEM((t_len, w), F32)
    return pl.pallas_call(
        _hgrn_kernel,
        grid=(bsz, HG_HEADS),
        in_specs=[
            lat(0), lat(1), lat(2), lat(3), lat(4),
            ctx(1), ctx(2), ctx(3),
            pl.BlockSpec((hg_lb.shape[0], 2, w), lambda b, h: (0, 0, h)),
            const((1, HG_DV)),
            const((HG_BLOCK, HG_BLOCK)), const((HG_BLOCK, HG_BLOCK)),
            const((128, 4 * HG_DK)),
        ],
        out_specs=pl.BlockSpec((1, t_len, HG_DV), lambda b, h: (b, 0, h)),
        out_shape=jax.ShapeDtypeStruct((bsz, t_len, HG_WIDTH), BF16),
        scratch_shapes=[seq_f32, seq_f32],
        compiler_params=_params("parallel", "parallel"),
        name="hgrn2_mixer",
    )(p_lat, p_lat, p_lat, p_lat, p_lat, p_ctx, p_ctx, p_ctx, hg_lb, norm_g.reshape(1, HG_DV),
      lower.astype(BF16), upper.astype(BF16), diag4.astype(BF16))


def _attn_kernel(q_ref, kp_ref, kc_ref, kn_ref, vp_ref, vc_ref, vn_ref, ck_ref, cv_ref, sink_ref, o_ref):
    n = pl.program_id(1)
    nb = pl.num_programs(1)
    blk = ATT_BLOCK
    hd = HEAD_DIM
    l_len = ck_ref.shape[1]
    n_keys = 3 * blk + l_len

    qi = lax.broadcasted_iota(jnp.int32, (blk, n_keys), 0)
    kj = lax.broadcasted_iota(jnp.int32, (blk, n_keys), 1)
    k_pos = (n - 1) * blk + kj
    ok = (kj >= 3 * blk) | ((jnp.abs(kj - blk - qi) <= WINDOW) & (k_pos >= 0) & (k_pos < nb * blk))
    bias = jnp.where(ok, 0.0, NEG_INF)
    ones = jnp.ones((n_keys, hd), BF16)

    heads = range(ATT_HEADS)
    k_all, v_aug = [], []
    for j in range(ATT_KV_HEADS):
        hs = slice(j * hd, (j + 1) * hd)
        k_all.append(jnp.concatenate(
            [kp_ref[0, :, hs], kc_ref[0, :, hs], kn_ref[0, :, hs], ck_ref[0, :, hs]], axis=0))
        v_all = jnp.concatenate([vp_ref[0, :, hs], vc_ref[0, :, hs], vn_ref[0, :, hs], cv_ref[0, :, hs]], axis=0)
        v_aug.append(jnp.concatenate([v_all, ones], axis=1))
    s = [lax.dot_general(q_ref[0, :, h * hd:(h + 1) * hd], k_all[h // ATT_GROUP], _NT,
                         preferred_element_type=F32) + bias for h in heads]
    sink = [sink_ref[:, h:h + 1] for h in heads]
    m = [jnp.maximum(jnp.max(s[h], axis=-1, keepdims=True), sink[h]) for h in heads]
    e = [jnp.exp(s[h] - m[h]).astype(BF16) for h in heads]
    oa = [_dot(e[h], v_aug[h // ATT_GROUP]) for h in heads]
    for h in heads:
        o = oa[h][:, :hd] / (oa[h][:, hd:] + jnp.exp(sink[h] - m[h]))
        o_ref[0, :, h * hd:(h + 1) * hd] = o.astype(o_ref.dtype)


def _window_attention(p_lat, p_ctx, ctx_tiles, sink):
    bsz, t_len, _ = p_lat.shape
    l_len = p_ctx.shape[1]
    nb = t_len // ATT_BLOCK
    kvw = ATT_KV_HEADS * HEAD_DIM
    q_blk = HG_COLS // ATT_WIDTH
    k_blk = (HG_COLS + ATT_WIDTH) // kvw
    v_blk = k_blk + 1
    ck_blk = _packed_col(ctx_tiles, HG_COLS + ATT_WIDTH) // kvw
    cv_blk = _packed_col(ctx_tiles, HG_COLS + ATT_WIDTH + kvw) // kvw

    def prev(n):
        return jnp.maximum(n - 1, 0)

    def nxt(n):
        return jnp.minimum(n + 1, nb - 1)

    def kv_spec(blk_idx, shift):
        return pl.BlockSpec((1, ATT_BLOCK, kvw), lambda b, n: (b, shift(n), blk_idx))

    same = lambda n: n
    return pl.pallas_call(
        _attn_kernel,
        grid=(bsz, nb),
        in_specs=[
            pl.BlockSpec((1, ATT_BLOCK, ATT_WIDTH), lambda b, n: (b, n, q_blk)),
            kv_spec(k_blk, prev), kv_spec(k_blk, same), kv_spec(k_blk, nxt),
            kv_spec(v_blk, prev), kv_spec(v_blk, same), kv_spec(v_blk, nxt),
            pl.BlockSpec((1, l_len, kvw), lambda b, n: (b, 0, ck_blk)),
            pl.BlockSpec((1, l_len, kvw), lambda b, n: (b, 0, cv_blk)),
            pl.BlockSpec((1, ATT_HEADS), lambda b, n: (0, 0)),
        ],
        out_specs=pl.BlockSpec((1, ATT_BLOCK, ATT_WIDTH), lambda b, n: (b, n, 0)),
        out_shape=jax.ShapeDtypeStruct((bsz, t_len, ATT_WIDTH), BF16),
        compiler_params=_params("parallel", "parallel"),
        name="window_gqa",
    )(p_lat, p_lat, p_lat, p_lat, p_lat, p_lat, p_lat, p_ctx, p_ctx, sink.reshape(1, ATT_HEADS))


def _outproj_kernel(x_ref, mod_ref, hg_ref, at_ref, w_ref, o_ref, w_scr):
    kh = hg_ref.shape[1]

    @pl.when(pl.program_id(0) == 0)
    def _():
        rows = 256

        def body(r, carry):
            sl = pl.ds(pl.multiple_of(r * rows, rows), rows)
            w_scr[sl, :] = w_ref[sl, :].astype(BF16)
            return carry

        lax.fori_loop(0, w_ref.shape[0] // rows, body, 0)

    acc = _dot(hg_ref[...], w_scr[:kh, :]) + _dot(at_ref[...], w_scr[kh:, :])
    o_ref[...] = x_ref[...] + mod_ref[0, 2:3, :] * acc


def _out_projection(x2d, mod, hg2d, at2d, w_out, tokens_per_mod, tm=512):
    m, d = x2d.shape
    kh = hg2d.shape[1]
    tiles_per_mod = tokens_per_mod // tm
    return pl.pallas_call(
        _outproj_kernel,
        grid=(m // tm,),
        in_specs=[
            pl.BlockSpec((tm, d), lambda i: (i, 0)),
            pl.BlockSpec((1, N_MOD, d), lambda i: (i // tiles_per_mod, 0, 0)),
            pl.BlockSpec((tm, kh), lambda i: (i, 0)),
            pl.BlockSpec((tm, kh), lambda i: (i, 0)),
            pl.BlockSpec(w_out.shape, lambda i: (0, 0), pipeline_mode=pl.Buffered(1)),
        ],
        out_specs=pl.BlockSpec((tm, d), lambda i: (i, 0)),
        out_shape=jax.ShapeDtypeStruct((m, d), F32),
        scratch_shapes=[pltpu.VMEM(w_out.shape, BF16)],
        compiler_params=_params("arbitrary"),
        name="out_proj",
    )(x2d, mod, hg2d, at2d, w_out)


def _ffn_kernel(x_ref, mod_ref, g_ref, wg_ref, wu_ref, wd_ref, o_ref, h_scr):
    @pl.when(pl.program_id(1) == 0)
    def _():
        _norm_modulate_into(x_ref, g_ref, mod_ref[0, 3:4, :], mod_ref[0, 4:5, :], h_scr)
        o_ref[...] = x_ref[...]

    h = h_scr[...]
    gate = _dot(h, wg_ref[...].astype(BF16))
    up = _dot(h, wu_ref[...].astype(BF16))
    act = (_silu(gate) * up).astype(BF16)
    o_ref[...] += mod_ref[0, 5:6, :] * _dot(act, wd_ref[...].astype(BF16))


def _ffn(x2d, mod, norm_g, w_gate_up, w_down, tokens_per_mod, tm=1024, tf=256):
    m, d = x2d.shape
    d_ff = w_down.shape[0]
    nf = d_ff // tf
    tiles_per_mod = tokens_per_mod // tm
    return pl.pallas_call(
        _ffn_kernel,
        grid=(m // tm, nf),
        in_specs=[
            pl.BlockSpec((tm, d), lambda i, j: (i, 0)),
            pl.BlockSpec((1, N_MOD, d), lambda i, j: (i // tiles_per_mod, 0, 0)),
            pl.BlockSpec((1, d), lambda i, j: (0, 0)),
            pl.BlockSpec((d, tf), lambda i, j: (0, j)),
            pl.BlockSpec((d, tf), lambda i, j: (0, nf + j)),
            pl.BlockSpec((tf, d), lambda i, j: (j, 0)),
        ],
        out_specs=pl.BlockSpec((tm, d), lambda i, j: (i, 0)),
        out_shape=jax.ShapeDtypeStruct((m, d), F32),
        scratch_shapes=[pltpu.VMEM((tm, d), BF16)],
        compiler_params=_params("parallel", "arbitrary"),
        name="swiglu_ffn",
    )(x2d, mod, norm_g.reshape(1, d), w_gate_up, w_gate_up, w_down)


def kernel(x, c, ctx, c_ctx, w_mod, b_mod, norm_mix_g, norm_ffn_g, w_in, hg_lb, hg_norm_g,
           q_norm_g, k_norm_g, attn_sink, w_out, w_gate_up, w_down):
    bsz, t_len, d = x.shape
    l_len = ctx.shape[1]
    depth = w_mod.shape[0]
    assert depth == 1, "single-layer block: the context stream never needs its own outputs"
    assert bsz + 1 <= MOD_ROWS

    c_rows = jnp.zeros((MOD_ROWS, d), F32).at[:bsz].set(c).at[bsz].set(c_ctx)
    mod = _modulation(c_rows, w_mod[0], b_mod[0]).reshape(MOD_ROWS, N_MOD, d)
    mod_lat = mod[:bsz]
    mod_ctx = mod[bsz:bsz + 1]

    cos, sin = _rope_tables(t_len)
    lat_tiles = _tiles_covering([(0, IN_COLS)])
    hk = HG_HEADS * HG_DK
    ctx_tiles = _tiles_covering([(hk, 3 * hk + HG_WIDTH), (HG_COLS + ATT_WIDTH, IN_COLS)])
    p_lat = _in_projection(x.reshape(bsz * t_len, d), mod_lat, norm_mix_g[0], w_in[0], q_norm_g[0], k_norm_g[0],
                           cos, sin, tokens_per_mod=t_len, tm=1024, rope=True, col_tiles=lat_tiles)
    p_ctx = _in_projection(ctx.reshape(bsz * l_len, d), mod_ctx, norm_mix_g[0], w_in[0], q_norm_g[0], k_norm_g[0],
                           cos, sin, tokens_per_mod=bsz * l_len, tm=bsz * l_len, rope=False, col_tiles=ctx_tiles)
    p_lat = p_lat.reshape(bsz, t_len, -1)
    p_ctx = p_ctx.reshape(bsz, l_len, -1)

    hg = _hgrn_mixer(p_lat, p_ctx, ctx_tiles, hg_lb, hg_norm_g[0])
    at = _window_attention(p_lat, p_ctx, ctx_tiles, attn_sink[0])

    x1 = _out_projection(x.reshape(bsz * t_len, d), mod_lat, hg.reshape(bsz * t_len, HG_WIDTH),
                         at.reshape(bsz * t_len, ATT_WIDTH), w_out[0], tokens_per_mod=t_len)
    x2 = _ffn(x1, mod_lat, norm_ffn_g[0], w_gate_up[0], w_down[0], tokens_per_mod=t_len)
    return x2.reshape(bsz, t_len, d)
```

```python
import functools

import jax
import jax.numpy as jnp
import numpy as np
from jax import lax
from jax.experimental import pallas as pl
from jax.experimental.pallas import tpu as pltpu

F32 = jnp.float32
BF16 = jnp.bfloat16

GRID_W = 64
HG_HEADS = 8
HG_DK = 128
HG_DV = 128
HG_CHUNK = 32
ATT_HEADS = 8
ATT_KV_HEADS = 2
ATT_GROUP = ATT_HEADS // ATT_KV_HEADS
HEAD_DIM = 128
WINDOW = 128
ATT_BLOCK = 128
ROPE_THETA = 10000.0
N_MOD = 6
RMS_EPS = 1e-6
NEG_INF = -1e30
HG_WIDTH = HG_HEADS * HG_DV
ATT_WIDTH = ATT_HEADS * HEAD_DIM
HG_COLS = 3 * HG_HEADS * HG_DK + 2 * HG_HEADS * HG_DV
ATT_COLS = (ATT_HEADS + 2 * ATT_KV_HEADS) * HEAD_DIM
IN_COLS = HG_COLS + ATT_COLS

VMEM_LIMIT_BYTES = 58 * 1024 * 1024
MOD_ROWS = 8

_NT = (((1,), (1,)), ((), ()))
_TN = (((0,), (0,)), ((), ()))


def _dot(a, b):
    return jnp.dot(a, b, preferred_element_type=F32)


def _silu(x):
    h = 0.5 * x
    return h * jnp.tanh(h) + h


def _params(*sem):
    return pltpu.CompilerParams(dimension_semantics=sem, vmem_limit_bytes=VMEM_LIMIT_BYTES)


def _mod_kernel(c_ref, w_ref, b_ref, o_ref):
    a = _silu(c_ref[...]).astype(BF16)
    o_ref[...] = _dot(a, w_ref[...].astype(BF16)) + b_ref[...]


def _modulation(c_rows, w_mod, b_mod, tn=1024):
    d, n = w_mod.shape
    return pl.pallas_call(
        _mod_kernel,
        grid=(n // tn,),
        in_specs=[
            pl.BlockSpec((MOD_ROWS, d), lambda j: (0, 0)),
            pl.BlockSpec((d, tn), lambda j: (0, j)),
            pl.BlockSpec((1, tn), lambda j: (0, j)),
        ],
        out_specs=pl.BlockSpec((MOD_ROWS, tn), lambda j: (0, j)),
        out_shape=jax.ShapeDtypeStruct((MOD_ROWS, n), F32),
        compiler_params=_params("arbitrary"),
        name="adaln_mod",
    )(c_rows, w_mod, b_mod.reshape(1, n))


def _rope_tables(t_len):
    half = HEAD_DIM // 2
    nf = half // 2
    pos = np.arange(t_len)
    inv_freq = ROPE_THETA ** (-np.arange(nf, dtype=np.float64) / nf)
    ang_r = (pos // GRID_W)[:, None] * inv_freq
    ang_c = (pos % GRID_W)[:, None] * inv_freq
    cos = np.concatenate([np.cos(ang_r)] * 2 + [np.cos(ang_c)] * 2, axis=-1)
    sin = np.concatenate([-np.sin(ang_r), np.sin(ang_r), -np.sin(ang_c), np.sin(ang_c)], axis=-1)
    return jnp.asarray(cos, F32), jnp.asarray(sin, F32)


def _head_norm(x, g):
    return x * lax.rsqrt(jnp.mean(x * x, axis=-1, keepdims=True) + RMS_EPS) * g


def _rope(x, cos, sin):
    quarter = HEAD_DIM // 4
    lane = lax.broadcasted_iota(jnp.int32, x.shape, 1)
    first = (lane % (2 * quarter)) < quarter
    partner = jnp.where(first, pltpu.roll(x, HEAD_DIM - quarter, 1), pltpu.roll(x, quarter, 1))
    return x * cos + partner * sin


def _norm_modulate_into(x_ref, g_ref, shift, scale, h_scr, rows=128):
    tm = x_ref.shape[0]
    gain = g_ref[...] * (1.0 + scale)

    def body(r, carry):
        sl = pl.ds(pl.multiple_of(r * rows, rows), rows)
        x = x_ref[sl, :]
        ms = jnp.mean(x * x, axis=-1, keepdims=True)
        h_scr[sl, :] = (x * lax.rsqrt(ms + RMS_EPS) * gain + shift).astype(BF16)
        return carry

    lax.fori_loop(0, tm // rows, body, 0)


def _inproj_kernel(tiles_ref, x_ref, mod_ref, g_ref, w_ref, cos_ref, sin_ref, qg_ref, kg_ref, o_ref, h_scr, *,
                   rope):
    j = pl.program_id(1)
    t = tiles_ref[j]
    tn = o_ref.shape[1]
    hd = HEAD_DIM
    q_tile0 = HG_COLS // tn
    kv_tile = (HG_COLS + ATT_WIDTH) // tn

    @pl.when(j == 0)
    def _():
        _norm_modulate_into(x_ref, g_ref, mod_ref[0, 0:1, :], mod_ref[0, 1:2, :], h_scr)

    acc = _dot(h_scr[...], w_ref[...].astype(BF16))

    def qk_prep(cols, g_row, scale):
        y = _head_norm(acc[:, cols], g_row)
        if rope:
            y = _rope(y, cos_ref[...], sin_ref[...])
        o_ref[:, cols] = (y * scale).astype(o_ref.dtype)

    @pl.when(t < q_tile0)
    def _():
        o_ref[...] = acc.astype(o_ref.dtype)

    @pl.when((t >= q_tile0) & (t < kv_tile))
    def _():
        for h in range(tn // hd):
            qk_prep(slice(h * hd, (h + 1) * hd), qg_ref[...], hd ** -0.5)

    @pl.when(t == kv_tile)
    def _():
        for h in range(ATT_KV_HEADS):
            qk_prep(slice(h * hd, (h + 1) * hd), kg_ref[...], 1.0)
        o_ref[:, ATT_KV_HEADS * hd:] = acc[:, ATT_KV_HEADS * hd:].astype(o_ref.dtype)


IN_TILE = 512


def _tiles_covering(col_ranges):
    return tuple(t for t in range(IN_COLS // IN_TILE)
                 if any(lo < (t + 1) * IN_TILE and t * IN_TILE < hi for lo, hi in col_ranges))


def _packed_col(col_tiles, col):
    return col_tiles.index(col // IN_TILE) * IN_TILE + col % IN_TILE


def _in_projection(x2d, mod, norm_g, w_in, q_norm_g, k_norm_g, cos, sin, tokens_per_mod, tm, rope, col_tiles):
    m, d = x2d.shape
    tn = IN_TILE
    assert HG_COLS % tn == 0 and ATT_WIDTH % tn == 0 and 2 * ATT_KV_HEADS * HEAD_DIM == tn
    tiles_per_mod = tokens_per_mod // tm
    tiles_per_seq = cos.shape[0] // tm
    vec = pl.BlockSpec((1, HEAD_DIM), lambda i, j, tiles: (0, 0))
    tab = pl.BlockSpec((tm, HEAD_DIM), lambda i, j, tiles: (i % tiles_per_seq, 0))
    grid_spec = pltpu.PrefetchScalarGridSpec(
        num_scalar_prefetch=1,
        grid=(m // tm, len(col_tiles)),
        in_specs=[
            pl.BlockSpec((tm, d), lambda i, j, tiles: (i, 0)),
            pl.BlockSpec((1, N_MOD, d), lambda i, j, tiles: (i // tiles_per_mod, 0, 0)),
            pl.BlockSpec((1, d), lambda i, j, tiles: (0, 0)),
            pl.BlockSpec((d, tn), lambda i, j, tiles: (0, tiles[j])),
            tab, tab, vec, vec,
        ],
        out_specs=pl.BlockSpec((tm, tn), lambda i, j, tiles: (i, j)),
        scratch_shapes=[pltpu.VMEM((tm, d), BF16)],
    )
    return pl.pallas_call(
        functools.partial(_inproj_kernel, rope=rope),
        grid_spec=grid_spec,
        out_shape=jax.ShapeDtypeStruct((m, len(col_tiles) * tn), BF16),
        compiler_params=_params("parallel", "arbitrary"),
        name="in_proj",
    )(jnp.asarray(col_tiles, jnp.int32), x2d, mod, norm_g.reshape(1, d), w_in, cos, sin,
      q_norm_g.reshape(1, HEAD_DIM), k_norm_g.reshape(1, HEAD_DIM))


HG_BLOCK = 256


def _chunk_masks():
    r = np.arange(HG_BLOCK)[:, None]
    c = np.arange(HG_BLOCK)[None, :]
    same = (r // HG_CHUNK) == (c // HG_CHUNK)
    return (same & (r >= c)), (same & (r <= c)), same


def _hgrn_kernel(q_ref, ff_ref, fb_ref, i_ref, g_ref, cff_ref, cfb_ref, ci_ref, lb_ref, ng_ref,
                 cumf_ref, cumb_ref, dmask_ref, o_ref, of_scr, ob_scr):
    t_len = q_ref.shape[1]
    l_len = ci_ref.shape[1]
    c = HG_CHUNK
    blk = HG_BLOCK
    nch = blk // c
    n_blk = t_len // blk
    assert l_len == blk

    a = lb_ref[...]
    e = jnp.exp(a - jnp.max(a, axis=0, keepdims=True))
    lb = e[0] / jnp.sum(e, axis=0)

    def gates(f_pre, lb_d):
        f = 0.5 * (1.0 + lb_d) + (0.5 * (1.0 - lb_d)) * jnp.tanh(0.5 * f_pre.astype(F32))
        log_f = jnp.log2(f)
        hi = lax.bitcast_convert_type(lax.bitcast_convert_type(log_f, jnp.uint32) & jnp.uint32(0xFFFF0000), F32)
        return 1.0 - f, jnp.concatenate([hi.astype(BF16), (log_f - hi).astype(BF16)], axis=1)

    def decays(parts, cum_ref, reverse):
        b = _dot(cum_ref[...], parts)
        b = b[:, 0:128] + b[:, 128:256]
        b3 = b.reshape(nch, c, HG_DK)
        tot = b3[:, 0:1, :] if reverse else b3[:, c - 1:c, :]
        b_tot = jnp.broadcast_to(tot, (nch, c, HG_DK)).reshape(blk, HG_DK)
        return b, b_tot, jnp.exp2(tot)

    def increments(key, b, b_tot, v):
        k_dec = (key * jnp.exp2(b_tot - b)).astype(BF16)
        v_t = v.astype(BF16).T
        upd = []
        for h in range(blk // 128):
            kd = k_dec[h * 128:(h + 1) * 128, :]
            kd4 = jnp.concatenate([kd] * 4, axis=1) * dmask_ref[...]
            u4 = _dot(v_t[:, h * 128:(h + 1) * 128], kd4)
            upd += [u4[:, j * HG_DK:(j + 1) * HG_DK] for j in range(4)]
        return upd

    def scores(q_pre, key, b):
        q_dec = (_silu(q_pre.astype(F32)) * jnp.exp2(b)).astype(BF16)
        k_inv = (key * jnp.exp2(-b)).astype(BF16)
        return q_dec, lax.dot_general(q_dec, k_inv, _NT, preferred_element_type=F32)

    def intra(s, v, cum_ref):
        return _dot(s.astype(BF16) * cum_ref[...], v.astype(BF16))

    def recur(st, dec, upd, q_dec, o_intra, reverse):
        outs = [None] * nch
        for n in (range(nch - 1, -1, -1) if reverse else range(nch)):
            if q_dec is not None:
                rows = slice(n * c, (n + 1) * c)
                outs[n] = o_intra[rows, :] + _dot(q_dec[rows, :], st.T.astype(BF16))
            st = st * dec[n] + upd[n]
        return (jnp.concatenate(outs, axis=0) if q_dec is not None else None), st

    lb_d = (lb[0:1, :], lb[1:2, :])
    cum = (cumf_ref, cumb_ref)
    f_lat = (ff_ref, fb_ref)
    o_scr = (of_scr, ob_scr)

    cv = ci_ref[0]
    st = []
    for d, f_ref in enumerate((cff_ref, cfb_ref)):
        key, parts = gates(f_ref[0], lb_d[d])
        b, b_tot, dec = decays(parts, cum[d], d == 1)
        _, st_d = recur(jnp.zeros((HG_DV, HG_DK), F32), dec, increments(key, b, b_tot, cv), None, None, d == 1)
        st.append(st_d)

    blocks_per_step = 2

    def lat_body(r, carry):
        chains = []
        for u in range(blocks_per_step):
            up = r * blocks_per_step + u
            chains.append((0, pl.ds(pl.multiple_of(up * blk, blk), blk)))
            chains.append((1, pl.ds(pl.multiple_of((n_blk - 1 - up) * blk, blk), blk)))
        g = [gates(f_lat[d][0, sl, :], lb_d[d]) for d, sl in chains]
        dk = [decays(g[i][1], cum[d], d == 1) for i, (d, sl) in enumerate(chains)]
        v = [i_ref[0, sl, :] for d, sl in chains]
        upd = [increments(g[i][0], dk[i][0], dk[i][1], v[i]) for i in range(len(chains))]
        qs = [scores(q_ref[0, sl, :], g[i][0], dk[i][0]) for i, (d, sl) in enumerate(chains)]
        o_in = [intra(qs[i][1], v[i], cum[d]) for i, (d, sl) in enumerate(chains)]
        st = list(carry)
        for i, (d, sl) in enumerate(chains):
            o, st[d] = recur(st[d], dk[i][2], upd[i], qs[i][0], o_in[i], d == 1)
            o_scr[d][sl, :] = o
        return tuple(st)

    lax.fori_loop(0, n_blk // blocks_per_step, lat_body, tuple(st))

    ng = ng_ref[...]

    def fin_body(r, carry):
        sl = pl.ds(pl.multiple_of(r * blk, blk), blk)
        o = of_scr[sl, :] + ob_scr[sl, :]
        y = o * lax.rsqrt(jnp.mean(o * o, axis=-1, keepdims=True) + RMS_EPS) * ng
        o_ref[0, sl, :] = (y * _silu(g_ref[0, sl, :].astype(F32))).astype(o_ref.dtype)
        return carry

    lax.fori_loop(0, n_blk, fin_body, 0)


def _hgrn_mixer(p_lat, p_ctx, ctx_tiles, hg_lb, norm_g):
    bsz, t_len, _ = p_lat.shape
    l_len = p_ctx.shape[1]
    w = HG_DK

    def lat(group):
        return pl.BlockSpec((1, t_len, w), lambda b, h: (b, 0, group * HG_HEADS + h))

    def ctx(group):
        blk0 = _packed_col(ctx_tiles, group * HG_HEADS * w) // w
        return pl.BlockSpec((1, l_len, w), lambda b, h: (b, 0, blk0 + h))

    def const(shape):
        return pl.BlockSpec(shape, lambda b, h: (0,) * len(shape))

    lower, upper, same = _chunk_masks()
    diag4 = np.tile(same[:128, ::HG_CHUNK][:, :4, None], (1, 1, HG_DK)).reshape(128, 4 * HG_DK)
    seq_f32 = pltpu.VMEM((t_len, w), F32)
    return pl.pallas_call(
        _hgrn_kernel,
        grid=(bsz, HG_HEADS),
        in_specs=[
            lat(0), lat(1), lat(2), lat(3), lat(4),
            ctx(1), ctx(2), ctx(3),
            pl.BlockSpec((hg_lb.shape[0], 2, w), lambda b, h: (0, 0, h)),
            const((1, HG_DV)),
            const((HG_BLOCK, HG_BLOCK)), const((HG_BLOCK, HG_BLOCK)),
            const((128, 4 * HG_DK)),
        ],
        out_specs=pl.BlockSpec((1, t_len, HG_DV), lambda b, h: (b, 0, h)),
        out_shape=jax.ShapeDtypeStruct((bsz, t_len, HG_WIDTH), BF16),
        scratch_shapes=[seq_f32, seq_f32],
        compiler_params=_params("parallel", "parallel"),
        name="hgrn2_mixer",
    )(p_lat, p_lat, p_lat, p_lat, p_lat, p_ctx, p_ctx, p_ctx, hg_lb, norm_g.reshape(1, HG_DV),
      jnp.asarray(lower, BF16), jnp.asarray(upper, BF16), jnp.asarray(diag4, BF16))


def _attn_kernel(q_ref, kp_ref, kc_ref, kn_ref, vp_ref, vc_ref, vn_ref, ck_ref, cv_ref, sink_ref, o_ref):
    n = pl.program_id(1)
    nb = pl.num_programs(1)
    blk = ATT_BLOCK
    hd = HEAD_DIM
    l_len = ck_ref.shape[1]
    n_keys = 3 * blk + l_len

    qi = lax.broadcasted_iota(jnp.int32, (blk, n_keys), 0)
    kj = lax.broadcasted_iota(jnp.int32, (blk, n_keys), 1)
    k_pos = (n - 1) * blk + kj
    ok = (kj >= 3 * blk) | ((jnp.abs(kj - blk - qi) <= WINDOW) & (k_pos >= 0) & (k_pos < nb * blk))
    bias = jnp.where(ok, 0.0, NEG_INF)
    ones = jnp.ones((n_keys, hd), BF16)

    heads = range(ATT_HEADS)
    k_all, v_aug = [], []
    for j in range(ATT_KV_HEADS):
        hs = slice(j * hd, (j + 1) * hd)
        k_all.append(jnp.concatenate(
            [kp_ref[0, :, hs], kc_ref[0, :, hs], kn_ref[0, :, hs], ck_ref[0, :, hs]], axis=0))
        v_all = jnp.concatenate([vp_ref[0, :, hs], vc_ref[0, :, hs], vn_ref[0, :, hs], cv_ref[0, :, hs]], axis=0)
        v_aug.append(jnp.concatenate([v_all, ones], axis=1))
    s = [lax.dot_general(q_ref[0, :, h * hd:(h + 1) * hd], k_all[h // ATT_GROUP], _NT,
                         preferred_element_type=F32) + bias for h in heads]
    sink = [sink_ref[:, h:h + 1] for h in heads]
    m = [jnp.maximum(jnp.max(s[h], axis=-1, keepdims=True), sink[h]) for h in heads]
    e = [jnp.exp(s[h] - m[h]).astype(BF16) for h in heads]
    oa = [_dot(e[h], v_aug[h // ATT_GROUP]) for h in heads]
    for h in heads:
        o = oa[h][:, :hd] / (oa[h][:, hd:] + jnp.exp(sink[h] - m[h]))
        o_ref[0, :, h * hd:(h + 1) * hd] = o.astype(o_ref.dtype)


def _window_attention(p_lat, p_ctx, ctx_tiles, sink):
    bsz, t_len, _ = p_lat.shape
    l_len = p_ctx.shape[1]
    nb = t_len // ATT_BLOCK
    kvw = ATT_KV_HEADS * HEAD_DIM
    q_blk = HG_COLS // ATT_WIDTH
    k_blk = (HG_COLS + ATT_WIDTH) // kvw
    v_blk = k_blk + 1
    ck_blk = _packed_col(ctx_tiles, HG_COLS + ATT_WIDTH) // kvw
    cv_blk = _packed_col(ctx_tiles, HG_COLS + ATT_WIDTH + kvw) // kvw

    def prev(n):
        return jnp.maximum(n - 1, 0)

    def nxt(n):
        return jnp.minimum(n + 1, nb - 1)

    def kv_spec(blk_idx, shift):
        return pl.BlockSpec((1, ATT_BLOCK, kvw), lambda b, n: (b, shift(n), blk_idx))

    same = lambda n: n
    return pl.pallas_call(
        _attn_kernel,
        grid=(bsz, nb),
        in_specs=[
            pl.BlockSpec((1, ATT_BLOCK, ATT_WIDTH), lambda b, n: (b, n, q_blk)),
            kv_spec(k_blk, prev), kv_spec(k_blk, same), kv_spec(k_blk, nxt),
            kv_spec(v_blk, prev), kv_spec(v_blk, same), kv_spec(v_blk, nxt),
            pl.BlockSpec((1, l_len, kvw), lambda b, n: (b, 0, ck_blk)),
            pl.BlockSpec((1, l_len, kvw), lambda b, n: (b, 0, cv_blk)),
            pl.BlockSpec((1, ATT_HEADS), lambda b, n: (0, 0)),
        ],
        out_specs=pl.BlockSpec((1, ATT_BLOCK, ATT_WIDTH), lambda b, n: (b, n, 0)),
        out_shape=jax.ShapeDtypeStruct((bsz, t_len, ATT_WIDTH), BF16),
        compiler_params=_params("parallel", "parallel"),
        name="window_gqa",
    )(p_lat, p_lat, p_lat, p_lat, p_lat, p_lat, p_lat, p_ctx, p_ctx, sink.reshape(1, ATT_HEADS))


def _outproj_kernel(x_ref, mod_ref, hg_ref, at_ref, w_ref, o_ref, w_scr):
    kh = hg_ref.shape[1]

    @pl.when(pl.program_id(0) == 0)
    def _():
        rows = 256

        def body(r, carry):
            sl = pl.ds(pl.multiple_of(r * rows, rows), rows)
            w_scr[sl, :] = w_ref[sl, :].astype(BF16)
            return carry

        lax.fori_loop(0, w_ref.shape[0] // rows, body, 0)

    acc = _dot(hg_ref[...], w_scr[:kh, :]) + _dot(at_ref[...], w_scr[kh:, :])
    o_ref[...] = x_ref[...] + mod_ref[0, 2:3, :] * acc


def _out_projection(x2d, mod, hg2d, at2d, w_out, tokens_per_mod, tm=512):
    m, d = x2d.shape
    kh = hg2d.shape[1]
    tiles_per_mod = tokens_per_mod // tm
    return pl.pallas_call(
        _outproj_kernel,
        grid=(m // tm,),
        in_specs=[
            pl.BlockSpec((tm, d), lambda i: (i, 0)),
            pl.BlockSpec((1, N_MOD, d), lambda i: (i // tiles_per_mod, 0, 0)),
            pl.BlockSpec((tm, kh), lambda i: (i, 0)),
            pl.BlockSpec((tm, kh), lambda i: (i, 0)),
            pl.BlockSpec(w_out.shape, lambda i: (0, 0), pipeline_mode=pl.Buffered(1)),
        ],
        out_specs=pl.BlockSpec((tm, d), lambda i: (i, 0)),
        out_shape=jax.ShapeDtypeStruct((m, d), F32),
        scratch_shapes=[pltpu.VMEM(w_out.shape, BF16)],
        compiler_params=_params("arbitrary"),
        name="out_proj",
    )(x2d, mod, hg2d, at2d, w_out)


def _ffn_kernel(x_ref, mod_ref, g_ref, wg_ref, wu_ref, wd_ref, o_ref, h_scr):
    @pl.when(pl.program_id(1) == 0)
    def _():
        _norm_modulate_into(x_ref, g_ref, mod_ref[0, 3:4, :], mod_ref[0, 4:5, :], h_scr)
        o_ref[...] = x_ref[...]

    h = h_scr[...]
    gate = _dot(h, wg_ref[...].astype(BF16))
    up = _dot(h, wu_ref[...].astype(BF16))
    act = (_silu(gate) * up).astype(BF16)
    o_ref[...] += mod_ref[0, 5:6, :] * _dot(act, wd_ref[...].astype(BF16))


def _ffn(x2d, mod, norm_g, w_gate_up, w_down, tokens_per_mod, tm=1024, tf=256):
    m, d = x2d.shape
    d_ff = w_down.shape[0]
    nf = d_ff // tf
    tiles_per_mod = tokens_per_mod // tm
    return pl.pallas_call(
        _ffn_kernel,
        grid=(m // tm, nf),
        in_specs=[
            pl.BlockSpec((tm, d), lambda i, j: (i, 0)),
            pl.BlockSpec((1, N_MOD, d), lambda i, j: (i // tiles_per_mod, 0, 0)),
            pl.BlockSpec((1, d), lambda i, j: (0, 0)),
            pl.BlockSpec((d, tf), lambda i, j: (0, j)),
            pl.BlockSpec((d, tf), lambda i, j: (0, nf + j)),
            pl.BlockSpec((tf, d), lambda i, j: (j, 0)),
        ],
        out_specs=pl.BlockSpec((tm, d), lambda i, j: (i, 0)),
        out_shape=jax.ShapeDtypeStruct((m, d), F32),
        scratch_shapes=[pltpu.VMEM((tm, d), BF16)],
        compiler_params=_params("parallel", "arbitrary"),
        name="swiglu_ffn",
    )(x2d, mod, norm_g.reshape(1, d), w_gate_up, w_gate_up, w_down)


def kernel(x, c, ctx, c_ctx, w_mod, b_mod, norm_mix_g, norm_ffn_g, w_in, hg_lb, hg_norm_g,
           q_norm_g, k_norm_g, attn_sink, w_out, w_gate_up, w_down):
    bsz, t_len, d = x.shape
    l_len = ctx.shape[1]
    depth = w_mod.shape[0]
    assert depth == 1, "single-layer block: the context stream never needs its own outputs"
    assert bsz + 1 <= MOD_ROWS

    c_rows = jnp.concatenate([c, c_ctx[None, :], jnp.zeros((MOD_ROWS - bsz - 1, d), F32)], axis=0)
    mod = _modulation(c_rows, w_mod[0], b_mod[0]).reshape(MOD_ROWS, N_MOD, d)
    mod_lat = mod[:bsz]
    mod_ctx = mod[bsz:bsz + 1]

    cos, sin = _rope_tables(t_len)
    lat_tiles = _tiles_covering([(0, IN_COLS)])
    hk = HG_HEADS * HG_DK
    ctx_tiles = _tiles_covering([(hk, 3 * hk + HG_WIDTH), (HG_COLS + ATT_WIDTH, IN_COLS)])
    p_lat = _in_projection(x.reshape(bsz * t_len, d), mod_lat, norm_mix_g[0], w_in[0], q_norm_g[0], k_norm_g[0],
                           cos, sin, tokens_per_mod=t_len, tm=1024, rope=True, col_tiles=lat_tiles)
    p_ctx = _in_projection(ctx.reshape(bsz * l_len, d), mod_ctx, norm_mix_g[0], w_in[0], q_norm_g[0], k_norm_g[0],
                           cos, sin, tokens_per_mod=bsz * l_len, tm=bsz * l_len, rope=False, col_tiles=ctx_tiles)
    p_lat = p_lat.reshape(bsz, t_len, -1)
    p_ctx = p_ctx.reshape(bsz, l_len, -1)

    hg = _hgrn_mixer(p_lat, p_ctx, ctx_tiles, hg_lb, hg_norm_g[0])
    at = _window_attention(p_lat, p_ctx, ctx_tiles, attn_sink[0])

    x1 = _out_projection(x.reshape(bsz * t_len, d), mod_lat, hg.reshape(bsz * t_len, HG_WIDTH),
                         at.reshape(bsz * t_len, ATT_WIDTH), w_out[0], tokens_per_mod=t_len)
    x2 = _ffn(x1, mod_lat, norm_ffn_g[0], w_gate_up[0], w_down[0], tokens_per_mod=t_len)
    return x2.reshape(bsz, t_len, d)
```

```python
import functools

import jax
import jax.numpy as jnp
import numpy as np
from jax import lax
from jax.experimental import pallas as pl
from jax.experimental.pallas import tpu as pltpu

F32 = jnp.float32
BF16 = jnp.bfloat16

GRID_W = 64
HG_HEADS = 8
HG_DK = 128
HG_DV = 128
HG_CHUNK = 32
ATT_HEADS = 8
ATT_KV_HEADS = 2
ATT_GROUP = ATT_HEADS // ATT_KV_HEADS
HEAD_DIM = 128
WINDOW = 128
ATT_BLOCK = 128
ROPE_THETA = 10000.0
N_MOD = 6
RMS_EPS = 1e-6
NEG_INF = -1e30
HG_WIDTH = HG_HEADS * HG_DV
ATT_WIDTH = ATT_HEADS * HEAD_DIM
HG_COLS = 3 * HG_HEADS * HG_DK + 2 * HG_HEADS * HG_DV
ATT_COLS = (ATT_HEADS + 2 * ATT_KV_HEADS) * HEAD_DIM
IN_COLS = HG_COLS + ATT_COLS

VMEM_LIMIT_BYTES = 58 * 1024 * 1024
MOD_ROWS = 8

_NT = (((1,), (1,)), ((), ()))
_TN = (((0,), (0,)), ((), ()))


def _dot(a, b):
    return jnp.dot(a, b, preferred_element_type=F32)


def _silu(x):
    h = 0.5 * x
    return h * jnp.tanh(h) + h


def _params(*sem):
    return pltpu.CompilerParams(dimension_semantics=sem, vmem_limit_bytes=VMEM_LIMIT_BYTES)


def _mod_kernel(c_ref, w_ref, b_ref, o_ref):
    a = _silu(c_ref[...]).astype(BF16)
    o_ref[...] = _dot(a, w_ref[...].astype(BF16)) + b_ref[...]


def _modulation(c_rows, w_mod, b_mod, tn=1024):
    d, n = w_mod.shape
    return pl.pallas_call(
        _mod_kernel,
        grid=(n // tn,),
        in_specs=[
            pl.BlockSpec((MOD_ROWS, d), lambda j: (0, 0)),
            pl.BlockSpec((d, tn), lambda j: (0, j)),
            pl.BlockSpec((1, tn), lambda j: (0, j)),
        ],
        out_specs=pl.BlockSpec((MOD_ROWS, tn), lambda j: (0, j)),
        out_shape=jax.ShapeDtypeStruct((MOD_ROWS, n), F32),
        compiler_params=_params("arbitrary"),
        name="adaln_mod",
    )(c_rows, w_mod, b_mod.reshape(1, n))


def _rope_tables(t_len):
    half = HEAD_DIM // 2
    nf = half // 2
    pos = np.arange(t_len)
    inv_freq = ROPE_THETA ** (-np.arange(nf, dtype=np.float64) / nf)
    ang_r = (pos // GRID_W)[:, None] * inv_freq
    ang_c = (pos % GRID_W)[:, None] * inv_freq
    cos = np.concatenate([np.cos(ang_r)] * 2 + [np.cos(ang_c)] * 2, axis=-1)
    sin = np.concatenate([-np.sin(ang_r), np.sin(ang_r), -np.sin(ang_c), np.sin(ang_c)], axis=-1)
    return jnp.asarray(cos, F32), jnp.asarray(sin, F32)


def _quarter_swap():
    quarter = HEAD_DIM // 4
    lane = np.arange(HEAD_DIM)
    src = np.where(lane % (2 * quarter) < quarter, lane + quarter, lane - quarter)
    return (lane[:, None] == src[None, :]).astype(np.float32)


def _norm_modulate_into(x_ref, g_ref, shift, scale, h_scr, rows=128):
    tm = x_ref.shape[0]
    gain = g_ref[...] * (1.0 + scale)

    def body(r, carry):
        sl = pl.ds(pl.multiple_of(r * rows, rows), rows)
        x = x_ref[sl, :]
        ms = jnp.mean(x * x, axis=-1, keepdims=True)
        h_scr[sl, :] = (x * lax.rsqrt(ms + RMS_EPS) * gain + shift).astype(BF16)
        return carry

    lax.fori_loop(0, tm // rows, body, 0)


def _inproj_kernel(tiles_ref, x_ref, mod_ref, g_ref, w_ref, cos_ref, sin_ref, swap_ref, qg_ref, kg_ref, o_ref, h_scr,
                   *, rope):
    j = pl.program_id(1)
    t = tiles_ref[j]
    tn = o_ref.shape[1]
    hd = HEAD_DIM
    q_tile0 = HG_COLS // tn
    kv_tile = (HG_COLS + ATT_WIDTH) // tn

    @pl.when(j == 0)
    def _():
        _norm_modulate_into(x_ref, g_ref, mod_ref[0, 0:1, :], mod_ref[0, 1:2, :], h_scr)

    acc = _dot(h_scr[...], w_ref[...].astype(BF16))

    def qk_prep(cols, g_row, scale):
        x = acc[:, cols]
        ssq = _dot((x * x).astype(BF16), jnp.ones((hd, hd), BF16))
        y = x * lax.rsqrt(ssq * (1.0 / hd) + RMS_EPS) * g_row
        if rope:
            y = y * cos_ref[...] + _dot(y.astype(BF16), swap_ref[...]) * sin_ref[...]
        o_ref[:, cols] = (y * scale).astype(o_ref.dtype)

    @pl.when(t < q_tile0)
    def _():
        o_ref[...] = acc.astype(o_ref.dtype)

    @pl.when((t >= q_tile0) & (t < kv_tile))
    def _():
        for h in range(tn // hd):
            qk_prep(slice(h * hd, (h + 1) * hd), qg_ref[...], hd ** -0.5)

    @pl.when(t == kv_tile)
    def _():
        for h in range(ATT_KV_HEADS):
            qk_prep(slice(h * hd, (h + 1) * hd), kg_ref[...], 1.0)
        o_ref[:, ATT_KV_HEADS * hd:] = acc[:, ATT_KV_HEADS * hd:].astype(o_ref.dtype)


IN_TILE = 512


def _tiles_covering(col_ranges):
    return tuple(t for t in range(IN_COLS // IN_TILE)
                 if any(lo < (t + 1) * IN_TILE and t * IN_TILE < hi for lo, hi in col_ranges))


def _packed_col(col_tiles, col):
    return col_tiles.index(col // IN_TILE) * IN_TILE + col % IN_TILE


def _in_projection(x2d, mod, norm_g, w_in, q_norm_g, k_norm_g, cos, sin, tokens_per_mod, tm, rope, col_tiles):
    m, d = x2d.shape
    tn = IN_TILE
    assert HG_COLS % tn == 0 and ATT_WIDTH % tn == 0 and 2 * ATT_KV_HEADS * HEAD_DIM == tn
    tiles_per_mod = tokens_per_mod // tm
    tiles_per_seq = cos.shape[0] // tm
    vec = pl.BlockSpec((1, HEAD_DIM), lambda i, j, tiles: (0, 0))
    tab = pl.BlockSpec((tm, HEAD_DIM), lambda i, j, tiles: (i % tiles_per_seq, 0))
    swap = pl.BlockSpec((HEAD_DIM, HEAD_DIM), lambda i, j, tiles: (0, 0))
    grid_spec = pltpu.PrefetchScalarGridSpec(
        num_scalar_prefetch=1,
        grid=(m // tm, len(col_tiles)),
        in_specs=[
            pl.BlockSpec((tm, d), lambda i, j, tiles: (i, 0)),
            pl.BlockSpec((1, N_MOD, d), lambda i, j, tiles: (i // tiles_per_mod, 0, 0)),
            pl.BlockSpec((1, d), lambda i, j, tiles: (0, 0)),
            pl.BlockSpec((d, tn), lambda i, j, tiles: (0, tiles[j])),
            tab, tab, swap, vec, vec,
        ],
        out_specs=pl.BlockSpec((tm, tn), lambda i, j, tiles: (i, j)),
        scratch_shapes=[pltpu.VMEM((tm, d), BF16)],
    )
    return pl.pallas_call(
        functools.partial(_inproj_kernel, rope=rope),
        grid_spec=grid_spec,
        out_shape=jax.ShapeDtypeStruct((m, len(col_tiles) * tn), BF16),
        compiler_params=_params("parallel", "arbitrary"),
        name="in_proj",
    )(jnp.asarray(col_tiles, jnp.int32), x2d, mod, norm_g.reshape(1, d), w_in, cos, sin,
      jnp.asarray(_quarter_swap(), BF16), q_norm_g.reshape(1, HEAD_DIM), k_norm_g.reshape(1, HEAD_DIM))


HG_BLOCK = 256


def _chunk_masks():
    r = np.arange(HG_BLOCK)[:, None]
    c = np.arange(HG_BLOCK)[None, :]
    same = (r // HG_CHUNK) == (c // HG_CHUNK)
    return (same & (r >= c)), (same & (r <= c)), same


def _hgrn_kernel(q_ref, ff_ref, fb_ref, i_ref, g_ref, cff_ref, cfb_ref, ci_ref, lb_ref, ng_ref,
                 cumf_ref, cumb_ref, dmask_ref, o_ref, of_scr, ob_scr):
    t_len = q_ref.shape[1]
    l_len = ci_ref.shape[1]
    c = HG_CHUNK
    blk = HG_BLOCK
    nch = blk // c
    n_blk = t_len // blk
    assert l_len == blk

    a = lb_ref[...]
    e = jnp.exp(a - jnp.max(a, axis=0, keepdims=True))
    lb = e[0] / jnp.sum(e, axis=0)

    def gates(f_pre, lb_d):
        f = 0.5 * (1.0 + lb_d) + (0.5 * (1.0 - lb_d)) * jnp.tanh(0.5 * f_pre.astype(F32))
        log_f = jnp.log2(f)
        hi = lax.bitcast_convert_type(lax.bitcast_convert_type(log_f, jnp.uint32) & jnp.uint32(0xFFFF0000), F32)
        return 1.0 - f, jnp.concatenate([hi.astype(BF16), (log_f - hi).astype(BF16)], axis=1)

    def decays(parts, cum_ref, reverse):
        b = _dot(cum_ref[...], parts)
        b = b[:, 0:128] + b[:, 128:256]
        b3 = b.reshape(nch, c, HG_DK)
        tot = b3[:, 0:1, :] if reverse else b3[:, c - 1:c, :]
        b_tot = jnp.broadcast_to(tot, (nch, c, HG_DK)).reshape(blk, HG_DK)
        return b, b_tot, jnp.exp2(tot)

    def increments(key, b, b_tot, v):
        k_dec = (key * jnp.exp2(b_tot - b)).astype(BF16)
        v_t = v.astype(BF16).T
        upd = []
        for h in range(blk // 128):
            kd = k_dec[h * 128:(h + 1) * 128, :]
            kd4 = jnp.concatenate([kd] * 4, axis=1) * dmask_ref[...]
            u4 = _dot(v_t[:, h * 128:(h + 1) * 128], kd4)
            upd += [u4[:, j * HG_DK:(j + 1) * HG_DK] for j in range(4)]
        return upd

    def scores(q_pre, key, b):
        q_dec = (_silu(q_pre.astype(F32)) * jnp.exp2(b)).astype(BF16)
        k_inv = (key * jnp.exp2(-b)).astype(BF16)
        return q_dec, lax.dot_general(q_dec, k_inv, _NT, preferred_element_type=F32)

    def intra(s, v, cum_ref):
        return _dot(s.astype(BF16) * cum_ref[...], v.astype(BF16))

    def recur(st, dec, upd, q_dec, o_intra, reverse):
        outs = [None] * nch
        for n in (range(nch - 1, -1, -1) if reverse else range(nch)):
            if q_dec is not None:
                rows = slice(n * c, (n + 1) * c)
                outs[n] = o_intra[rows, :] + _dot(q_dec[rows, :], st.T.astype(BF16))
            st = st * dec[n] + upd[n]
        return (jnp.concatenate(outs, axis=0) if q_dec is not None else None), st

    lb_d = (lb[0:1, :], lb[1:2, :])
    cum = (cumf_ref, cumb_ref)
    f_lat = (ff_ref, fb_ref)
    o_scr = (of_scr, ob_scr)

    cv = ci_ref[0]
    st = []
    for d, f_ref in enumerate((cff_ref, cfb_ref)):
        key, parts = gates(f_ref[0], lb_d[d])
        b, b_tot, dec = decays(parts, cum[d], d == 1)
        _, st_d = recur(jnp.zeros((HG_DV, HG_DK), F32), dec, increments(key, b, b_tot, cv), None, None, d == 1)
        st.append(st_d)

    blocks_per_step = 2

    def lat_body(r, carry):
        chains = []
        for u in range(blocks_per_step):
            up = r * blocks_per_step + u
            chains.append((0, pl.ds(pl.multiple_of(up * blk, blk), blk)))
            chains.append((1, pl.ds(pl.multiple_of((n_blk - 1 - up) * blk, blk), blk)))
        g = [gates(f_lat[d][0, sl, :], lb_d[d]) for d, sl in chains]
        dk = [decays(g[i][1], cum[d], d == 1) for i, (d, sl) in enumerate(chains)]
        v = [i_ref[0, sl, :] for d, sl in chains]
        upd = [increments(g[i][0], dk[i][0], dk[i][1], v[i]) for i in range(len(chains))]
        qs = [scores(q_ref[0, sl, :], g[i][0], dk[i][0]) for i, (d, sl) in enumerate(chains)]
        o_in = [intra(qs[i][1], v[i], cum[d]) for i, (d, sl) in enumerate(chains)]
        st = list(carry)
        for i, (d, sl) in enumerate(chains):
            o, st[d] = recur(st[d], dk[i][2], upd[i], qs[i][0], o_in[i], d == 1)
            o_scr[d][sl, :] = o
        return tuple(st)

    lax.fori_loop(0, n_blk // blocks_per_step, lat_body, tuple(st))

    ng = ng_ref[...]

    def fin_body(r, carry):
        sl = pl.ds(pl.multiple_of(r * blk, blk), blk)
        o = of_scr[sl, :] + ob_scr[sl, :]
        y = o * lax.rsqrt(jnp.mean(o * o, axis=-1, keepdims=True) + RMS_EPS) * ng
        o_ref[0, sl, :] = (y * _silu(g_ref[0, sl, :].astype(F32))).astype(o_ref.dtype)
        return carry

    lax.fori_loop(0, n_blk, fin_body, 0, unroll=4)


def _hgrn_mixer(p_lat, p_ctx, ctx_tiles, hg_lb, norm_g):
    bsz, t_len, _ = p_lat.shape
    l_len = p_ctx.shape[1]
    w = HG_DK

    def lat(group):
        return pl.BlockSpec((1, t_len, w), lambda b, h: (b, 0, group * HG_HEADS + h))

    def ctx(group):
        blk0 = _packed_col(ctx_tiles, group * HG_HEADS * w) // w
        return pl.BlockSpec((1, l_len, w), lambda b, h: (b, 0, blk0 + h))

    def const(shape):
        return pl.BlockSpec(shape, lambda b, h: (0,) * len(shape))

    lower, upper, same = _chunk_masks()
    diag4 = np.tile(same[:128, ::HG_CHUNK][:, :4, None], (1, 1, HG_DK)).reshape(128, 4 * HG_DK)
    seq_f32 = pltpu.VMEM((t_len, w), F32)
    return pl.pallas_call(
        _hgrn_kernel,
        grid=(bsz, HG_HEADS),
        in_specs=[
            lat(0), lat(1), lat(2), lat(3), lat(4),
            ctx(1), ctx(2), ctx(3),
            pl.BlockSpec((hg_lb.shape[0], 2, w), lambda b, h: (0, 0, h)),
            const((1, HG_DV)),
            const((HG_BLOCK, HG_BLOCK)), const((HG_BLOCK, HG_BLOCK)),
            const((128, 4 * HG_DK)),
        ],
        out_specs=pl.BlockSpec((1, t_len, HG_DV), lambda b, h: (b, 0, h)),
        out_shape=jax.ShapeDtypeStruct((bsz, t_len, HG_WIDTH), BF16),
        scratch_shapes=[seq_f32, seq_f32],
        compiler_params=_params("parallel", "parallel"),
        name="hgrn2_mixer",
    )(p_lat, p_lat, p_lat, p_lat, p_lat, p_ctx, p_ctx, p_ctx, hg_lb, norm_g.reshape(1, HG_DV),
      jnp.asarray(lower, BF16), jnp.asarray(upper, BF16), jnp.asarray(diag4, BF16))


ATT_QROWS = ATT_BLOCK


def _attn_kernel(q_ref, kp_ref, kc_ref, kn_ref, vp_ref, vc_ref, vn_ref, ck_ref, cv_ref, sink_ref, o_ref):
    n = pl.program_id(1)
    n_steps = pl.num_programs(1)
    rows = q_ref.shape[1]
    blk = ATT_BLOCK
    hd = HEAD_DIM
    l_len = ck_ref.shape[1]
    band = rows + 2 * blk
    n_keys = band + l_len

    qi = lax.broadcasted_iota(jnp.int32, (rows, n_keys), 0)
    kj = lax.broadcasted_iota(jnp.int32, (rows, n_keys), 1)
    k_pos = n * rows - blk + kj
    ok = (kj >= band) | ((jnp.abs(kj - blk - qi) <= WINDOW) & (k_pos >= 0) & (k_pos < n_steps * rows))
    bias = jnp.where(ok, 0.0, NEG_INF)
    ones = jnp.ones((n_keys, hd), BF16)

    def keys(prev_ref, cur_ref, next_ref, ctx_ref, cols):
        return jnp.concatenate([prev_ref[0, rows - blk:, cols], cur_ref[0, :, cols], next_ref[0, :blk, cols],
                                ctx_ref[0, :, cols]], axis=0)

    k_all, v_aug = [], []
    for j in range(ATT_KV_HEADS):
        hs = slice(j * hd, (j + 1) * hd)
        k_all.append(keys(kp_ref, kc_ref, kn_ref, ck_ref, hs))
        v_all = keys(vp_ref, vc_ref, vn_ref, cv_ref, hs)
        v_aug.append(jnp.concatenate([v_all, ones], axis=1))
    heads = range(ATT_HEADS)
    s = [lax.dot_general(q_ref[0, :, h * hd:(h + 1) * hd], k_all[h // ATT_GROUP], _NT,
                         preferred_element_type=F32) + bias for h in heads]
    sink = [sink_ref[:, h:h + 1] for h in heads]
    m = [jnp.maximum(jnp.max(s[h], axis=-1, keepdims=True), sink[h]) for h in heads]
    e = [jnp.exp(s[h] - m[h]).astype(BF16) for h in heads]
    oa = [_dot(e[h], v_aug[h // ATT_GROUP]) for h in heads]
    for h in heads:
        o = oa[h][:, :hd] / (oa[h][:, hd:] + jnp.exp(sink[h] - m[h]))
        o_ref[0, :, h * hd:(h + 1) * hd] = o.astype(o_ref.dtype)


def _window_attention(p_lat, p_ctx, ctx_tiles, sink):
    bsz, t_len, _ = p_lat.shape
    l_len = p_ctx.shape[1]
    nb = t_len // ATT_QROWS
    kvw = ATT_KV_HEADS * HEAD_DIM
    q_blk = HG_COLS // ATT_WIDTH
    k_blk = (HG_COLS + ATT_WIDTH) // kvw
    v_blk = k_blk + 1
    ck_blk = _packed_col(ctx_tiles, HG_COLS + ATT_WIDTH) // kvw
    cv_blk = _packed_col(ctx_tiles, HG_COLS + ATT_WIDTH + kvw) // kvw

    def prev(n):
        return jnp.maximum(n - 1, 0)

    def nxt(n):
        return jnp.minimum(n + 1, nb - 1)

    def kv_spec(blk_idx, shift):
        return pl.BlockSpec((1, ATT_QROWS, kvw), lambda b, n: (b, shift(n), blk_idx))

    same = lambda n: n
    return pl.pallas_call(
        _attn_kernel,
        grid=(bsz, nb),
        in_specs=[
            pl.BlockSpec((1, ATT_QROWS, ATT_WIDTH), lambda b, n: (b, n, q_blk)),
            kv_spec(k_blk, prev), kv_spec(k_blk, same), kv_spec(k_blk, nxt),
            kv_spec(v_blk, prev), kv_spec(v_blk, same), kv_spec(v_blk, nxt),
            pl.BlockSpec((1, l_len, kvw), lambda b, n: (b, 0, ck_blk)),
            pl.BlockSpec((1, l_len, kvw), lambda b, n: (b, 0, cv_blk)),
            pl.BlockSpec((1, ATT_HEADS), lambda b, n: (0, 0)),
        ],
        out_specs=pl.BlockSpec((1, ATT_QROWS, ATT_WIDTH), lambda b, n: (b, n, 0)),
        out_shape=jax.ShapeDtypeStruct((bsz, t_len, ATT_WIDTH), BF16),
        compiler_params=_params("parallel", "parallel"),
        name="window_gqa",
    )(p_lat, p_lat, p_lat, p_lat, p_lat, p_lat, p_lat, p_ctx, p_ctx, sink.reshape(1, ATT_HEADS))


def _outproj_kernel(x_ref, mod_ref, hg_ref, at_ref, w_ref, o_ref, w_scr):
    kh = hg_ref.shape[1]

    @pl.when(pl.program_id(0) == 0)
    def _():
        rows = 256

        def body(r, carry):
            sl = pl.ds(pl.multiple_of(r * rows, rows), rows)
            w_scr[sl, :] = w_ref[sl, :].astype(BF16)
            return carry

        lax.fori_loop(0, w_ref.shape[0] // rows, body, 0)

    acc = _dot(hg_ref[...], w_scr[:kh, :]) + _dot(at_ref[...], w_scr[kh:, :])
    o_ref[...] = x_ref[...] + mod_ref[0, 2:3, :] * acc


def _out_projection(x2d, mod, hg2d, at2d, w_out, tokens_per_mod, tm=512):
    m, d = x2d.shape
    kh = hg2d.shape[1]
    tiles_per_mod = tokens_per_mod // tm
    return pl.pallas_call(
        _outproj_kernel,
        grid=(m // tm,),
        in_specs=[
            pl.BlockSpec((tm, d), lambda i: (i, 0)),
            pl.BlockSpec((1, N_MOD, d), lambda i: (i // tiles_per_mod, 0, 0)),
            pl.BlockSpec((tm, kh), lambda i: (i, 0)),
            pl.BlockSpec((tm, kh), lambda i: (i, 0)),
            pl.BlockSpec(w_out.shape, lambda i: (0, 0), pipeline_mode=pl.Buffered(1)),
        ],
        out_specs=pl.BlockSpec((tm, d), lambda i: (i, 0)),
        out_shape=jax.ShapeDtypeStruct((m, d), F32),
        scratch_shapes=[pltpu.VMEM(w_out.shape, BF16)],
        compiler_params=_params("arbitrary"),
        name="out_proj",
    )(x2d, mod, hg2d, at2d, w_out)


def _ffn_kernel(x_ref, mod_ref, g_ref, wg_ref, wu_ref, wd_ref, o_ref, h_scr):
    @pl.when(pl.program_id(1) == 0)
    def _():
        _norm_modulate_into(x_ref, g_ref, mod_ref[0, 3:4, :], mod_ref[0, 4:5, :], h_scr)
        o_ref[...] = x_ref[...]

    h = h_scr[...]
    gate = _dot(h, wg_ref[...].astype(BF16))
    up = _dot(h, wu_ref[...].astype(BF16))
    act = (_silu(gate) * up).astype(BF16)
    o_ref[...] += mod_ref[0, 5:6, :] * _dot(act, wd_ref[...].astype(BF16))


def _ffn(x2d, mod, norm_g, w_gate_up, w_down, tokens_per_mod, tm=1024, tf=256):
    m, d = x2d.shape
    d_ff = w_down.shape[0]
    nf = d_ff // tf
    tiles_per_mod = tokens_per_mod // tm
    return pl.pallas_call(
        _ffn_kernel,
        grid=(m // tm, nf),
        in_specs=[
            pl.BlockSpec((tm, d), lambda i, j: (i, 0)),
            pl.BlockSpec((1, N_MOD, d), lambda i, j: (i // tiles_per_mod, 0, 0)),
            pl.BlockSpec((1, d), lambda i, j: (0, 0)),
            pl.BlockSpec((d, tf), lambda i, j: (0, j)),
            pl.BlockSpec((d, tf), lambda i, j: (0, nf + j)),
            pl.BlockSpec((tf, d), lambda i, j: (j, 0)),
        ],
        out_specs=pl.BlockSpec((tm, d), lambda i, j: (i, 0)),
        out_shape=jax.ShapeDtypeStruct((m, d), F32),
        scratch_shapes=[pltpu.VMEM((tm, d), BF16)],
        compiler_params=_params("parallel", "arbitrary"),
        name="swiglu_ffn",
    )(x2d, mod, norm_g.reshape(1, d), w_gate_up, w_gate_up, w_down)


def kernel(x, c, ctx, c_ctx, w_mod, b_mod, norm_mix_g, norm_ffn_g, w_in, hg_lb, hg_norm_g,
           q_norm_g, k_norm_g, attn_sink, w_out, w_gate_up, w_down):
    bsz, t_len, d = x.shape
    l_len = ctx.shape[1]
    depth = w_mod.shape[0]
    assert depth == 1, "single-layer block: the context stream never needs its own outputs"
    assert bsz + 1 <= MOD_ROWS

    c_rows = jnp.concatenate([c, c_ctx[None, :], jnp.zeros((MOD_ROWS - bsz - 1, d), F32)], axis=0)
    mod = _modulation(c_rows, w_mod[0], b_mod[0]).reshape(MOD_ROWS, N_MOD, d)
    mod_lat = mod[:bsz]
    mod_ctx = mod[bsz:bsz + 1]

    cos, sin = _rope_tables(t_len)
    lat_tiles = _tiles_covering([(0, IN_COLS)])
    hk = HG_HEADS * HG_DK
    ctx_tiles = _tiles_covering([(hk, 3 * hk + HG_WIDTH), (HG_COLS + ATT_WIDTH, IN_COLS)])
    p_lat = _in_projection(x.reshape(bsz * t_len, d), mod_lat, norm_mix_g[0], w_in[0], q_norm_g[0], k_norm_g[0],
                           cos, sin, tokens_per_mod=t_len, tm=1024, rope=True, col_tiles=lat_tiles)
    p_ctx = _in_projection(ctx.reshape(bsz * l_len, d), mod_ctx, norm_mix_g[0], w_in[0], q_norm_g[0], k_norm_g[0],
                           cos, sin, tokens_per_mod=bsz * l_len, tm=bsz * l_len, rope=False, col_tiles=ctx_tiles)
    p_lat = p_lat.reshape(bsz, t_len, -1)
    p_ctx = p_ctx.reshape(bsz, l_len, -1)

    hg = _hgrn_mixer(p_lat, p_ctx, ctx_tiles, hg_lb, hg_norm_g[0])
    at = _window_attention(p_lat, p_ctx, ctx_tiles, attn_sink[0])

    x1 = _out_projection(x.reshape(bsz * t_len, d), mod_lat, hg.reshape(bsz * t_len, HG_WIDTH),
                         at.reshape(bsz * t_len, ATT_WIDTH), w_out[0], tokens_per_mod=t_len)
    x2 = _ffn(x1, mod_lat, norm_ffn_g[0], w_gate_up[0], w_down[0], tokens_per_mod=t_len)
    return x2.reshape(bsz, t_len, d)
```

```python
import functools

import jax
import jax.numpy as jnp
import numpy as np
from jax import lax
from jax.experimental import pallas as pl
from jax.experimental.pallas import tpu as pltpu

F32 = jnp.float32
BF16 = jnp.bfloat16

GRID_W = 64
HG_HEADS = 8
HG_DK = 128
HG_DV = 128
HG_CHUNK = 32
ATT_HEADS = 8
ATT_KV_HEADS = 2
ATT_GROUP = ATT_HEADS // ATT_KV_HEADS
HEAD_DIM = 128
WINDOW = 128
ATT_BLOCK = 128
ROPE_THETA = 10000.0
N_MOD = 6
RMS_EPS = 1e-6
NEG_INF = -1e30
HG_WIDTH = HG_HEADS * HG_DV
ATT_WIDTH = ATT_HEADS * HEAD_DIM
HG_COLS = 3 * HG_HEADS * HG_DK + 2 * HG_HEADS * HG_DV
ATT_COLS = (ATT_HEADS + 2 * ATT_KV_HEADS) * HEAD_DIM
IN_COLS = HG_COLS + ATT_COLS

VMEM_LIMIT_BYTES = 58 * 1024 * 1024
MOD_ROWS = 8

_NT = (((1,), (1,)), ((), ()))
_TN = (((0,), (0,)), ((), ()))


def _dot(a, b):
    return jnp.dot(a, b, preferred_element_type=F32)


def _silu(x):
    h = 0.5 * x
    return h * jnp.tanh(h) + h


def _params(*sem):
    return pltpu.CompilerParams(dimension_semantics=sem, vmem_limit_bytes=VMEM_LIMIT_BYTES)


def _mod_kernel(c_ref, w_ref, b_ref, o_ref):
    a = _silu(c_ref[...]).astype(BF16)
    o_ref[...] = _dot(a, w_ref[...].astype(BF16)) + b_ref[...]


def _modulation(c_rows, w_mod, b_mod, tn=2048):
    d, n = w_mod.shape
    return pl.pallas_call(
        _mod_kernel,
        grid=(n // tn,),
        in_specs=[
            pl.BlockSpec((MOD_ROWS, d), lambda j: (0, 0)),
            pl.BlockSpec((d, tn), lambda j: (0, j)),
            pl.BlockSpec((1, tn), lambda j: (0, j)),
        ],
        out_specs=pl.BlockSpec((MOD_ROWS, tn), lambda j: (0, j)),
        out_shape=jax.ShapeDtypeStruct((MOD_ROWS, n), F32),
        compiler_params=_params("arbitrary"),
        name="adaln_mod",
    )(c_rows, w_mod, b_mod.reshape(1, n))


def _rope_tables(t_len):
    half = HEAD_DIM // 2
    nf = half // 2
    pos = np.arange(t_len)
    inv_freq = ROPE_THETA ** (-np.arange(nf, dtype=np.float64) / nf)
    ang_r = (pos // GRID_W)[:, None] * inv_freq
    ang_c = (pos % GRID_W)[:, None] * inv_freq
    cos = np.concatenate([np.cos(ang_r)] * 2 + [np.cos(ang_c)] * 2, axis=-1)
    sin = np.concatenate([-np.sin(ang_r), np.sin(ang_r), -np.sin(ang_c), np.sin(ang_c)], axis=-1)
    return jnp.asarray(cos, F32), jnp.asarray(sin, F32)


def _quarter_swap():
    quarter = HEAD_DIM // 4
    lane = np.arange(HEAD_DIM)
    src = np.where(lane % (2 * quarter) < quarter, lane + quarter, lane - quarter)
    return (lane[:, None] == src[None, :]).astype(np.float32)


def _norm_modulate_into(x_ref, g_ref, shift, scale, h_scr, rows=128):
    tm = x_ref.shape[0]
    gain = g_ref[...] * (1.0 + scale)

    def body(r, carry):
        sl = pl.ds(pl.multiple_of(r * rows, rows), rows)
        x = x_ref[sl, :]
        ms = jnp.mean(x * x, axis=-1, keepdims=True)
        h_scr[sl, :] = (x * lax.rsqrt(ms + RMS_EPS) * gain + shift).astype(BF16)
        return carry

    lax.fori_loop(0, tm // rows, body, 0)


def _inproj_kernel(tiles_ref, x_ref, mod_ref, g_ref, w_ref, cos_ref, sin_ref, swap_ref, qg_ref, kg_ref, o_ref, h_scr,
                   *, rope):
    j = pl.program_id(1)
    t = tiles_ref[j]
    tn = o_ref.shape[1]
    hd = HEAD_DIM
    q_tile0 = HG_COLS // tn
    kv_tile = (HG_COLS + ATT_WIDTH) // tn

    @pl.when(j == 0)
    def _():
        _norm_modulate_into(x_ref, g_ref, mod_ref[0, 0:1, :], mod_ref[0, 1:2, :], h_scr)

    acc = _dot(h_scr[...], w_ref[...].astype(BF16))

    def qk_prep(cols, g_row, scale):
        x = acc[:, cols]
        ssq = _dot((x * x).astype(BF16), jnp.ones((hd, hd), BF16))
        y = x * lax.rsqrt(ssq * (1.0 / hd) + RMS_EPS) * g_row
        if rope:
            y = y * cos_ref[...] + _dot(y.astype(BF16), swap_ref[...]) * sin_ref[...]
        o_ref[:, cols] = (y * scale).astype(o_ref.dtype)

    @pl.when(t < q_tile0)
    def _():
        o_ref[...] = acc.astype(o_ref.dtype)

    @pl.when((t >= q_tile0) & (t < kv_tile))
    def _():
        for h in range(tn // hd):
            qk_prep(slice(h * hd, (h + 1) * hd), qg_ref[...], hd ** -0.5)

    @pl.when(t == kv_tile)
    def _():
        for h in range(ATT_KV_HEADS):
            qk_prep(slice(h * hd, (h + 1) * hd), kg_ref[...], 1.0)
        o_ref[:, ATT_KV_HEADS * hd:] = acc[:, ATT_KV_HEADS * hd:].astype(o_ref.dtype)


IN_TILE = 512


def _tiles_covering(col_ranges):
    return tuple(t for t in range(IN_COLS // IN_TILE)
                 if any(lo < (t + 1) * IN_TILE and t * IN_TILE < hi for lo, hi in col_ranges))


def _packed_col(col_tiles, col):
    return col_tiles.index(col // IN_TILE) * IN_TILE + col % IN_TILE


def _in_projection(x2d, mod, norm_g, w_in, q_norm_g, k_norm_g, cos, sin, tokens_per_mod, tm, rope, col_tiles):
    m, d = x2d.shape
    tn = IN_TILE
    assert HG_COLS % tn == 0 and ATT_WIDTH % tn == 0 and 2 * ATT_KV_HEADS * HEAD_DIM == tn
    tiles_per_mod = tokens_per_mod // tm
    tiles_per_seq = cos.shape[0] // tm
    vec = pl.BlockSpec((1, HEAD_DIM), lambda i, j, tiles: (0, 0))
    tab = pl.BlockSpec((tm, HEAD_DIM), lambda i, j, tiles: (i % tiles_per_seq, 0))
    swap = pl.BlockSpec((HEAD_DIM, HEAD_DIM), lambda i, j, tiles: (0, 0))
    grid_spec = pltpu.PrefetchScalarGridSpec(
        num_scalar_prefetch=1,
        grid=(m // tm, len(col_tiles)),
        in_specs=[
            pl.BlockSpec((tm, d), lambda i, j, tiles: (i, 0)),
            pl.BlockSpec((1, N_MOD, d), lambda i, j, tiles: (i // tiles_per_mod, 0, 0)),
            pl.BlockSpec((1, d), lambda i, j, tiles: (0, 0)),
            pl.BlockSpec((d, tn), lambda i, j, tiles: (0, tiles[j])),
            tab, tab, swap, vec, vec,
        ],
        out_specs=pl.BlockSpec((tm, tn), lambda i, j, tiles: (i, j)),
        scratch_shapes=[pltpu.VMEM((tm, d), BF16)],
    )
    return pl.pallas_call(
        functools.partial(_inproj_kernel, rope=rope),
        grid_spec=grid_spec,
        out_shape=jax.ShapeDtypeStruct((m, len(col_tiles) * tn), BF16),
        compiler_params=_params("parallel", "arbitrary"),
        name="in_proj",
    )(jnp.asarray(col_tiles, jnp.int32), x2d, mod, norm_g.reshape(1, d), w_in, cos, sin,
      jnp.asarray(_quarter_swap(), BF16), q_norm_g.reshape(1, HEAD_DIM), k_norm_g.reshape(1, HEAD_DIM))


HG_BLOCK = 256


def _chunk_masks():
    r = np.arange(HG_BLOCK)[:, None]
    c = np.arange(HG_BLOCK)[None, :]
    same = (r // HG_CHUNK) == (c // HG_CHUNK)
    return (same & (r >= c)), (same & (r <= c)), same


def _hgrn_kernel(q_ref, ff_ref, fb_ref, i_ref, g_ref, cff_ref, cfb_ref, ci_ref, lb_ref, ng_ref,
                 cumf_ref, cumb_ref, dmask_ref, o_ref, of_scr, ob_scr):
    t_len = q_ref.shape[1]
    l_len = ci_ref.shape[1]
    c = HG_CHUNK
    blk = HG_BLOCK
    nch = blk // c
    n_blk = t_len // blk
    assert l_len == blk

    a = lb_ref[...]
    e = jnp.exp(a - jnp.max(a, axis=0, keepdims=True))
    lb = e[0] / jnp.sum(e, axis=0)

    def gates(f_pre, lb_d):
        f = 0.5 * (1.0 + lb_d) + (0.5 * (1.0 - lb_d)) * jnp.tanh(0.5 * f_pre.astype(F32))
        log_f = jnp.log2(f)
        hi = lax.bitcast_convert_type(lax.bitcast_convert_type(log_f, jnp.uint32) & jnp.uint32(0xFFFF0000), F32)
        return 1.0 - f, jnp.concatenate([hi.astype(BF16), (log_f - hi).astype(BF16)], axis=1)

    def decays(parts, cum_ref, reverse):
        b = _dot(cum_ref[...], parts)
        b = b[:, 0:128] + b[:, 128:256]
        b3 = b.reshape(nch, c, HG_DK)
        tot = b3[:, 0:1, :] if reverse else b3[:, c - 1:c, :]
        b_tot = jnp.broadcast_to(tot, (nch, c, HG_DK)).reshape(blk, HG_DK)
        return b, b_tot, jnp.exp2(tot)

    def increments(key, b, b_tot, v):
        k_dec = (key * jnp.exp2(b_tot - b)).astype(BF16)
        v_t = v.astype(BF16).T
        upd = []
        for h in range(blk // 128):
            kd = k_dec[h * 128:(h + 1) * 128, :]
            kd4 = jnp.concatenate([kd] * 4, axis=1) * dmask_ref[...]
            u4 = _dot(v_t[:, h * 128:(h + 1) * 128], kd4)
            upd += [u4[:, j * HG_DK:(j + 1) * HG_DK] for j in range(4)]
        return upd

    def scores(q_pre, key, b):
        q_dec = (_silu(q_pre.astype(F32)) * jnp.exp2(b)).astype(BF16)
        k_inv = (key * jnp.exp2(-b)).astype(BF16)
        return q_dec, lax.dot_general(q_dec, k_inv, _NT, preferred_element_type=F32)

    def intra(s, v, cum_ref):
        return _dot(s.astype(BF16) * cum_ref[...], v.astype(BF16))

    def recur(st, dec, upd, q_dec, o_intra, reverse):
        outs = [None] * nch
        for n in (range(nch - 1, -1, -1) if reverse else range(nch)):
            if q_dec is not None:
                rows = slice(n * c, (n + 1) * c)
                outs[n] = o_intra[rows, :] + _dot(q_dec[rows, :], st.T.astype(BF16))
            st = st * dec[n] + upd[n]
        return (jnp.concatenate(outs, axis=0) if q_dec is not None else None), st

    lb_d = (lb[0:1, :], lb[1:2, :])
    cum = (cumf_ref, cumb_ref)
    f_lat = (ff_ref, fb_ref)
    o_scr = (of_scr, ob_scr)

    cv = ci_ref[0]
    st = []
    for d, f_ref in enumerate((cff_ref, cfb_ref)):
        key, parts = gates(f_ref[0], lb_d[d])
        b, b_tot, dec = decays(parts, cum[d], d == 1)
        _, st_d = recur(jnp.zeros((HG_DV, HG_DK), F32), dec, increments(key, b, b_tot, cv), None, None, d == 1)
        st.append(st_d)

    blocks_per_step = 2

    def lat_body(r, carry):
        chains = []
        for u in range(blocks_per_step):
            up = r * blocks_per_step + u
            chains.append((0, pl.ds(pl.multiple_of(up * blk, blk), blk)))
            chains.append((1, pl.ds(pl.multiple_of((n_blk - 1 - up) * blk, blk), blk)))
        g = [gates(f_lat[d][0, sl, :], lb_d[d]) for d, sl in chains]
        dk = [decays(g[i][1], cum[d], d == 1) for i, (d, sl) in enumerate(chains)]
        v = [i_ref[0, sl, :] for d, sl in chains]
        upd = [increments(g[i][0], dk[i][0], dk[i][1], v[i]) for i in range(len(chains))]
        qs = [scores(q_ref[0, sl, :], g[i][0], dk[i][0]) for i, (d, sl) in enumerate(chains)]
        o_in = [intra(qs[i][1], v[i], cum[d]) for i, (d, sl) in enumerate(chains)]
        st = list(carry)
        for i, (d, sl) in enumerate(chains):
            o, st[d] = recur(st[d], dk[i][2], upd[i], qs[i][0], o_in[i], d == 1)
            o_scr[d][sl, :] = o
        return tuple(st)

    lax.fori_loop(0, n_blk // blocks_per_step, lat_body, tuple(st))

    ng = ng_ref[...]

    def fin_body(r, carry):
        sl = pl.ds(pl.multiple_of(r * blk, blk), blk)
        o = of_scr[sl, :] + ob_scr[sl, :]
        y = o * lax.rsqrt(jnp.mean(o * o, axis=-1, keepdims=True) + RMS_EPS) * ng
        o_ref[0, sl, :] = (y * _silu(g_ref[0, sl, :].astype(F32))).astype(o_ref.dtype)
        return carry

    lax.fori_loop(0, n_blk, fin_body, 0, unroll=4)


def _hgrn_mixer(p_lat, p_ctx, ctx_tiles, hg_lb, norm_g):
    bsz, t_len, _ = p_lat.shape
    l_len = p_ctx.shape[1]
    w = HG_DK

    def lat(group):
        return pl.BlockSpec((1, t_len, w), lambda b, h: (b, 0, group * HG_HEADS + h))

    def ctx(group):
        blk0 = _packed_col(ctx_tiles, group * HG_HEADS * w) // w
        return pl.BlockSpec((1, l_len, w), lambda b, h: (b, 0, blk0 + h))

    def const(shape):
        return pl.BlockSpec(shape, lambda b, h: (0,) * len(shape))

    lower, upper, same = _chunk_masks()
    diag4 = np.tile(same[:128, ::HG_CHUNK][:, :4, None], (1, 1, HG_DK)).reshape(128, 4 * HG_DK)
    seq_f32 = pltpu.VMEM((t_len, w), F32)
    return pl.pallas_call(
        _hgrn_kernel,
        grid=(bsz, HG_HEADS),
        in_specs=[
            lat(0), lat(1), lat(2), lat(3), lat(4),
            ctx(1), ctx(2), ctx(3),
            pl.BlockSpec((hg_lb.shape[0], 2, w), lambda b, h: (0, 0, h)),
            const((1, HG_DV)),
            const((HG_BLOCK, HG_BLOCK)), const((HG_BLOCK, HG_BLOCK)),
            const((128, 4 * HG_DK)),
        ],
        out_specs=pl.BlockSpec((1, t_len, HG_DV), lambda b, h: (b, 0, h)),
        out_shape=jax.ShapeDtypeStruct((bsz, t_len, HG_WIDTH), BF16),
        scratch_shapes=[seq_f32, seq_f32],
        compiler_params=_params("parallel", "parallel"),
        name="hgrn2_mixer",
    )(p_lat, p_lat, p_lat, p_lat, p_lat, p_ctx, p_ctx, p_ctx, hg_lb, norm_g.reshape(1, HG_DV),
      jnp.asarray(lower, BF16), jnp.asarray(upper, BF16), jnp.asarray(diag4, BF16))


ATT_QROWS = ATT_BLOCK


def _attn_kernel(q_ref, kvp_ref, kvc_ref, kvn_ref, ckv_ref, sink_ref, o_ref):
    n = pl.program_id(1)
    n_steps = pl.num_programs(1)
    rows = q_ref.shape[1]
    blk = ATT_BLOCK
    hd = HEAD_DIM
    l_len = ckv_ref.shape[1]
    band = rows + 2 * blk
    n_keys = band + l_len

    qi = lax.broadcasted_iota(jnp.int32, (rows, n_keys), 0)
    kj = lax.broadcasted_iota(jnp.int32, (rows, n_keys), 1)
    k_pos = n * rows - blk + kj
    ok = (kj >= band) | ((jnp.abs(kj - blk - qi) <= WINDOW) & (k_pos >= 0) & (k_pos < n_steps * rows))
    bias = jnp.where(ok, 0.0, NEG_INF)
    ones = jnp.ones((n_keys, hd), BF16)

    def gather(cols):
        return jnp.concatenate([kvp_ref[0, rows - blk:, cols], kvc_ref[0, :, cols], kvn_ref[0, :blk, cols],
                                ckv_ref[0, :, cols]], axis=0)

    k_all, v_aug = [], []
    for j in range(ATT_KV_HEADS):
        k_all.append(gather(slice(j * hd, (j + 1) * hd)))
        v_all = gather(slice((ATT_KV_HEADS + j) * hd, (ATT_KV_HEADS + j + 1) * hd))
        v_aug.append(jnp.concatenate([v_all, ones], axis=1))
    heads = range(ATT_HEADS)
    s = [lax.dot_general(q_ref[0, :, h * hd:(h + 1) * hd], k_all[h // ATT_GROUP], _NT,
                         preferred_element_type=F32) + bias for h in heads]
    sink = [sink_ref[:, h:h + 1] for h in heads]
    m = [jnp.maximum(jnp.max(s[h], axis=-1, keepdims=True), sink[h]) for h in heads]
    e = [jnp.exp(s[h] - m[h]).astype(BF16) for h in heads]
    oa = [_dot(e[h], v_aug[h // ATT_GROUP]) for h in heads]
    for h in heads:
        o = oa[h][:, :hd] / (oa[h][:, hd:] + jnp.exp(sink[h] - m[h]))
        o_ref[0, :, h * hd:(h + 1) * hd] = o.astype(o_ref.dtype)


def _window_attention(p_lat, p_ctx, ctx_tiles, sink):
    bsz, t_len, _ = p_lat.shape
    l_len = p_ctx.shape[1]
    nb = t_len // ATT_QROWS
    kvw = 2 * ATT_KV_HEADS * HEAD_DIM
    q_blk = HG_COLS // ATT_WIDTH
    kv_blk = (HG_COLS + ATT_WIDTH) // kvw
    ckv_blk = _packed_col(ctx_tiles, HG_COLS + ATT_WIDTH) // kvw

    def prev(n):
        return jnp.maximum(n - 1, 0)

    def nxt(n):
        return jnp.minimum(n + 1, nb - 1)

    def kv_spec(shift):
        return pl.BlockSpec((1, ATT_QROWS, kvw), lambda b, n: (b, shift(n), kv_blk))

    same = lambda n: n
    return pl.pallas_call(
        _attn_kernel,
        grid=(bsz, nb),
        in_specs=[
            pl.BlockSpec((1, ATT_QROWS, ATT_WIDTH), lambda b, n: (b, n, q_blk)),
            kv_spec(prev), kv_spec(same), kv_spec(nxt),
            pl.BlockSpec((1, l_len, kvw), lambda b, n: (b, 0, ckv_blk)),
            pl.BlockSpec((1, ATT_HEADS), lambda b, n: (0, 0)),
        ],
        out_specs=pl.BlockSpec((1, ATT_QROWS, ATT_WIDTH), lambda b, n: (b, n, 0)),
        out_shape=jax.ShapeDtypeStruct((bsz, t_len, ATT_WIDTH), BF16),
        compiler_params=_params("parallel", "parallel"),
        name="window_gqa",
    )(p_lat, p_lat, p_lat, p_lat, p_ctx, sink.reshape(1, ATT_HEADS))


def _outproj_kernel(x_ref, mod_ref, hg_ref, at_ref, w_ref, o_ref, w_scr):
    kh = hg_ref.shape[1]

    @pl.when(pl.program_id(0) == 0)
    def _():
        rows = 256

        def body(r, carry):
            sl = pl.ds(pl.multiple_of(r * rows, rows), rows)
            w_scr[sl, :] = w_ref[sl, :].astype(BF16)
            return carry

        lax.fori_loop(0, w_ref.shape[0] // rows, body, 0)

    acc = _dot(hg_ref[...], w_scr[:kh, :]) + _dot(at_ref[...], w_scr[kh:, :])
    o_ref[...] = x_ref[...] + mod_ref[0, 2:3, :] * acc


def _out_projection(x2d, mod, hg2d, at2d, w_out, tokens_per_mod, tm=512):
    m, d = x2d.shape
    kh = hg2d.shape[1]
    tiles_per_mod = tokens_per_mod // tm
    return pl.pallas_call(
        _outproj_kernel,
        grid=(m // tm,),
        in_specs=[
            pl.BlockSpec((tm, d), lambda i: (i, 0)),
            pl.BlockSpec((1, N_MOD, d), lambda i: (i // tiles_per_mod, 0, 0)),
            pl.BlockSpec((tm, kh), lambda i: (i, 0)),
            pl.BlockSpec((tm, kh), lambda i: (i, 0)),
            pl.BlockSpec(w_out.shape, lambda i: (0, 0), pipeline_mode=pl.Buffered(1)),
        ],
        out_specs=pl.BlockSpec((tm, d), lambda i: (i, 0)),
        out_shape=jax.ShapeDtypeStruct((m, d), F32),
        scratch_shapes=[pltpu.VMEM(w_out.shape, BF16)],
        compiler_params=_params("arbitrary"),
        name="out_proj",
    )(x2d, mod, hg2d, at2d, w_out)


def _ffn_kernel(x_ref, mod_ref, g_ref, wg_ref, wu_ref, wd_ref, o_ref, h_scr):
    j = pl.program_id(1)

    @pl.when(j == 0)
    def _():
        _norm_modulate_into(x_ref, g_ref, mod_ref[0, 3:4, :], mod_ref[0, 4:5, :], h_scr)

    def gated_down():
        h = h_scr[...]
        gate = _dot(h, wg_ref[...].astype(BF16))
        up = _dot(h, wu_ref[...].astype(BF16))
        act = (_silu(gate) * up).astype(BF16)
        return mod_ref[0, 5:6, :] * _dot(act, wd_ref[...].astype(BF16))

    @pl.when(j == 0)
    def _():
        o_ref[...] = x_ref[...] + gated_down()

    @pl.when(j > 0)
    def _():
        o_ref[...] += gated_down()


def _ffn(x2d, mod, norm_g, w_gate_up, w_down, tokens_per_mod, tm=1024, tf=256):
    m, d = x2d.shape
    d_ff = w_down.shape[0]
    nf = d_ff // tf
    tiles_per_mod = tokens_per_mod // tm
    return pl.pallas_call(
        _ffn_kernel,
        grid=(m // tm, nf),
        in_specs=[
            pl.BlockSpec((tm, d), lambda i, j: (i, 0)),
            pl.BlockSpec((1, N_MOD, d), lambda i, j: (i // tiles_per_mod, 0, 0)),
            pl.BlockSpec((1, d), lambda i, j: (0, 0)),
            pl.BlockSpec((d, tf), lambda i, j: (0, j)),
            pl.BlockSpec((d, tf), lambda i, j: (0, nf + j)),
            pl.BlockSpec((tf, d), lambda i, j: (j, 0)),
        ],
        out_specs=pl.BlockSpec((tm, d), lambda i, j: (i, 0)),
        out_shape=jax.ShapeDtypeStruct((m, d), F32),
        scratch_shapes=[pltpu.VMEM((tm, d), BF16)],
        compiler_params=_params("parallel", "arbitrary"),
        name="swiglu_ffn",
    )(x2d, mod, norm_g.reshape(1, d), w_gate_up, w_gate_up, w_down)


def kernel(x, c, ctx, c_ctx, w_mod, b_mod, norm_mix_g, norm_ffn_g, w_in, hg_lb, hg_norm_g,
           q_norm_g, k_norm_g, attn_sink, w_out, w_gate_up, w_down):
    bsz, t_len, d = x.shape
    l_len = ctx.shape[1]
    depth = w_mod.shape[0]
    assert depth == 1, "single-layer block: the context stream never needs its own outputs"
    assert bsz + 1 <= MOD_ROWS

    c_rows = jnp.concatenate([c, c_ctx[None, :], jnp.zeros((MOD_ROWS - bsz - 1, d), F32)], axis=0)
    mod = _modulation(c_rows, w_mod[0], b_mod[0]).reshape(MOD_ROWS, N_MOD, d)
    mod_lat = mod[:bsz]
    mod_ctx = mod[bsz:bsz + 1]

    cos, sin = _rope_tables(t_len)
    lat_tiles = _tiles_covering([(0, IN_COLS)])
    hk = HG_HEADS * HG_DK
    ctx_tiles = _tiles_covering([(hk, 3 * hk + HG_WIDTH), (HG_COLS + ATT_WIDTH, IN_COLS)])
    p_lat = _in_projection(x.reshape(bsz * t_len, d), mod_lat, norm_mix_g[0], w_in[0], q_norm_g[0], k_norm_g[0],
                           cos, sin, tokens_per_mod=t_len, tm=1024, rope=True, col_tiles=lat_tiles)
    p_ctx = _in_projection(ctx.reshape(bsz * l_len, d), mod_ctx, norm_mix_g[0], w_in[0], q_norm_g[0], k_norm_g[0],
                           cos, sin, tokens_per_mod=bsz * l_len, tm=bsz * l_len, rope=False, col_tiles=ctx_tiles)
    p_lat = p_lat.reshape(bsz, t_len, -1)
    p_ctx = p_ctx.reshape(bsz, l_len, -1)

    hg = _hgrn_mixer(p_lat, p_ctx, ctx_tiles, hg_lb, hg_norm_g[0])
    at = _window_attention(p_lat, p_ctx, ctx_tiles, attn_sink[0])

    x1 = _out_projection(x.reshape(bsz * t_len, d), mod_lat, hg.reshape(bsz * t_len, HG_WIDTH),
                         at.reshape(bsz * t_len, ATT_WIDTH), w_out[0], tokens_per_mod=t_len)
    x2 = _ffn(x1, mod_lat, norm_ffn_g[0], w_gate_up[0], w_down[0], tokens_per_mod=t_len)
    return x2.reshape(bsz, t_len, d)
```

```python
import functools

import jax
import jax.numpy as jnp
import numpy as np
from jax import lax
from jax.experimental import pallas as pl
from jax.experimental.pallas import tpu as pltpu

F32 = jnp.float32
BF16 = jnp.bfloat16

GRID_W = 64
HG_HEADS = 8
HG_DK = 128
HG_DV = 128
HG_CHUNK = 32
ATT_HEADS = 8
ATT_KV_HEADS = 2
ATT_GROUP = ATT_HEADS // ATT_KV_HEADS
HEAD_DIM = 128
WINDOW = 128
ATT_BLOCK = 128
ROPE_THETA = 10000.0
N_MOD = 6
RMS_EPS = 1e-6
NEG_INF = -1e30
HG_WIDTH = HG_HEADS * HG_DV
ATT_WIDTH = ATT_HEADS * HEAD_DIM
HG_COLS = 3 * HG_HEADS * HG_DK + 2 * HG_HEADS * HG_DV
ATT_COLS = (ATT_HEADS + 2 * ATT_KV_HEADS) * HEAD_DIM
IN_COLS = HG_COLS + ATT_COLS

VMEM_LIMIT_BYTES = 58 * 1024 * 1024
MOD_ROWS = 8

_NT = (((1,), (1,)), ((), ()))
_TN = (((0,), (0,)), ((), ()))


def _dot(a, b):
    return jnp.dot(a, b, preferred_element_type=F32)


def _silu(x):
    h = 0.5 * x
    return h * jnp.tanh(h) + h


def _params(*sem):
    return pltpu.CompilerParams(dimension_semantics=sem, vmem_limit_bytes=VMEM_LIMIT_BYTES)


def _mod_kernel(c_ref, w_ref, b_ref, o_ref):
    a = _silu(c_ref[...]).astype(BF16)
    o_ref[...] = _dot(a, w_ref[...].astype(BF16)) + b_ref[...]


def _modulation(c_rows, w_mod, b_mod, n_cols, tn=1024):
    d, n = w_mod.shape
    return pl.pallas_call(
        _mod_kernel,
        grid=(n_cols // tn,),
        in_specs=[
            pl.BlockSpec((MOD_ROWS, d), lambda j: (0, 0)),
            pl.BlockSpec((d, tn), lambda j: (0, j)),
            pl.BlockSpec((1, tn), lambda j: (0, j)),
        ],
        out_specs=pl.BlockSpec((MOD_ROWS, tn), lambda j: (0, j)),
        out_shape=jax.ShapeDtypeStruct((MOD_ROWS, n_cols), F32),
        compiler_params=_params("arbitrary"),
        name="adaln_mod",
    )(c_rows, w_mod, b_mod.reshape(1, n))


def _rope_tables(t_len):
    half = HEAD_DIM // 2
    nf = half // 2
    pos = np.arange(t_len)
    inv_freq = ROPE_THETA ** (-np.arange(nf, dtype=np.float64) / nf)
    ang_r = (pos // GRID_W)[:, None] * inv_freq
    ang_c = (pos % GRID_W)[:, None] * inv_freq
    cos = np.concatenate([np.cos(ang_r)] * 2 + [np.cos(ang_c)] * 2, axis=-1)
    sin = np.concatenate([-np.sin(ang_r), np.sin(ang_r), -np.sin(ang_c), np.sin(ang_c)], axis=-1)
    pair = lambda a: jnp.asarray(np.tile(a, (1, HEAD_PAIR // HEAD_DIM)), F32)
    return pair(cos), pair(sin)


HEAD_PAIR = 2 * HEAD_DIM


def _pair_matrices():
    quarter = HEAD_DIM // 4
    lane = np.arange(HEAD_PAIR)
    src = np.where(lane % (2 * quarter) < quarter, lane + quarter, lane - quarter)
    swap = lane[:, None] == src[None, :]
    head_mean = ((lane[:, None] // HEAD_DIM) == (lane[None, :] // HEAD_DIM)) / HEAD_DIM
    return jnp.asarray(swap, BF16), jnp.asarray(head_mean, BF16)


def _norm_modulate_into(x_ref, g_ref, shift, scale, h_scr, rows=128):
    tm = x_ref.shape[0]
    gain = g_ref[...] * (1.0 + scale)

    def body(r, carry):
        sl = pl.ds(pl.multiple_of(r * rows, rows), rows)
        x = x_ref[sl, :]
        ms = jnp.mean(x * x, axis=-1, keepdims=True)
        h_scr[sl, :] = (x * lax.rsqrt(ms + RMS_EPS) * gain + shift).astype(BF16)
        return carry

    lax.fori_loop(0, tm // rows, body, 0)


def _inproj_kernel(tiles_ref, x_ref, mod_ref, g_ref, w_ref, cos_ref, sin_ref, swap_ref, hmean_ref, qg_ref, kg_ref,
                   o_ref, h_scr, *, rope):
    j = pl.program_id(1)
    t = tiles_ref[j]
    tn = o_ref.shape[1]
    hd = HEAD_DIM
    q_tile0 = HG_COLS // tn
    kv_tile = (HG_COLS + ATT_WIDTH) // tn

    @pl.when(j == 0)
    def _():
        _norm_modulate_into(x_ref, g_ref, mod_ref[0, 0:1, :], mod_ref[0, 1:2, :], h_scr)

    acc = _dot(h_scr[...], w_ref[...].astype(BF16))

    def qk_prep(n_heads, g_row, scale):
        cols = [slice(p * HEAD_PAIR, (p + 1) * HEAD_PAIR) for p in range(n_heads * hd // HEAD_PAIR)]
        x = [acc[:, c] for c in cols]
        gain = g_row * scale
        ms = [_dot((xh * xh).astype(BF16), hmean_ref[...]) for xh in x]
        y = [xh * lax.rsqrt(m + RMS_EPS) * gain for xh, m in zip(x, ms)]
        if rope:
            swapped = [_dot(yh.astype(BF16), swap_ref[...]) for yh in y]
            y = [yh * cos_ref[...] + sh * sin_ref[...] for yh, sh in zip(y, swapped)]
        for c, yh in zip(cols, y):
            o_ref[:, c] = yh.astype(o_ref.dtype)

    @pl.when(t < q_tile0)
    def _():
        o_ref[...] = acc.astype(o_ref.dtype)

    @pl.when((t >= q_tile0) & (t < kv_tile))
    def _():
        qk_prep(tn // hd, qg_ref[...], hd ** -0.5)

    @pl.when(t == kv_tile)
    def _():
        qk_prep(ATT_KV_HEADS, kg_ref[...], 1.0)
        o_ref[:, ATT_KV_HEADS * hd:] = acc[:, ATT_KV_HEADS * hd:].astype(o_ref.dtype)


IN_TILE = 512


def _tiles_covering(col_ranges):
    return tuple(t for t in range(IN_COLS // IN_TILE)
                 if any(lo < (t + 1) * IN_TILE and t * IN_TILE < hi for lo, hi in col_ranges))


def _packed_col(col_tiles, col):
    return col_tiles.index(col // IN_TILE) * IN_TILE + col % IN_TILE


def _in_projection(x2d, mod, norm_g, w_in, q_norm_g, k_norm_g, cos, sin, tokens_per_mod, tm, rope, col_tiles):
    m, d = x2d.shape
    tn = IN_TILE
    assert HG_COLS % tn == 0 and ATT_WIDTH % tn == 0 and 2 * ATT_KV_HEADS * HEAD_DIM == tn
    tiles_per_mod = tokens_per_mod // tm
    tiles_per_seq = cos.shape[0] // tm
    vec = pl.BlockSpec((1, HEAD_PAIR), lambda i, j, tiles: (0, 0))
    tab = pl.BlockSpec((tm, HEAD_PAIR), lambda i, j, tiles: (i % tiles_per_seq, 0))
    mat = pl.BlockSpec((HEAD_PAIR, HEAD_PAIR), lambda i, j, tiles: (0, 0))
    per_pair = lambda g: jnp.tile(g.reshape(1, HEAD_DIM), (1, HEAD_PAIR // HEAD_DIM))
    grid_spec = pltpu.PrefetchScalarGridSpec(
        num_scalar_prefetch=1,
        grid=(m // tm, len(col_tiles)),
        in_specs=[
            pl.BlockSpec((tm, d), lambda i, j, tiles: (i, 0)),
            pl.BlockSpec((1, mod.shape[1], d), lambda i, j, tiles: (i // tiles_per_mod, 0, 0)),
            pl.BlockSpec((1, d), lambda i, j, tiles: (0, 0)),
            pl.BlockSpec((d, tn), lambda i, j, tiles: (0, tiles[j])),
            tab, tab, mat, mat, vec, vec,
        ],
        out_specs=pl.BlockSpec((tm, tn), lambda i, j, tiles: (i, j)),
        scratch_shapes=[pltpu.VMEM((tm, d), BF16)],
    )
    return pl.pallas_call(
        functools.partial(_inproj_kernel, rope=rope),
        grid_spec=grid_spec,
        out_shape=jax.ShapeDtypeStruct((m, len(col_tiles) * tn), BF16),
        compiler_params=_params("parallel", "arbitrary"),
        name="in_proj",
    )(jnp.asarray(col_tiles, jnp.int32), x2d, mod, norm_g.reshape(1, d), w_in, cos, sin,
      *_pair_matrices(), per_pair(q_norm_g), per_pair(k_norm_g))


HG_BLOCK = 256


def _chunk_masks():
    r = np.arange(HG_BLOCK)[:, None]
    c = np.arange(HG_BLOCK)[None, :]
    same = (r // HG_CHUNK) == (c // HG_CHUNK)
    return (same & (r >= c)), (same & (r <= c)), same


def _hgrn_kernel(q_ref, ff_ref, fb_ref, i_ref, g_ref, cff_ref, cfb_ref, ci_ref, lb_ref, ng_ref,
                 cumf_ref, cumb_ref, dmask_ref, c_ref, wm_ref, bm_ref, o_ref, mod_ref, of_scr, ob_scr):
    mod_ref[...] = _dot(_silu(c_ref[...]).astype(BF16), wm_ref[...].astype(BF16)) + bm_ref[...]

    t_len = q_ref.shape[1]
    l_len = ci_ref.shape[1]
    c = HG_CHUNK
    blk = HG_BLOCK
    nch = blk // c
    n_blk = t_len // blk
    assert l_len == blk

    a = lb_ref[...]
    e = jnp.exp(a - jnp.max(a, axis=0, keepdims=True))
    lb = e[0] / jnp.sum(e, axis=0)

    def gates(f_pre, lb_d):
        f = 0.5 * (1.0 + lb_d) + (0.5 * (1.0 - lb_d)) * jnp.tanh(0.5 * f_pre.astype(F32))
        log_f = jnp.log2(f)
        hi = lax.bitcast_convert_type(lax.bitcast_convert_type(log_f, jnp.uint32) & jnp.uint32(0xFFFF0000), F32)
        return 1.0 - f, jnp.concatenate([hi.astype(BF16), (log_f - hi).astype(BF16)], axis=1)

    def decays(parts, cum_ref, reverse):
        b = _dot(cum_ref[...], parts)
        b = b[:, 0:128] + b[:, 128:256]
        b3 = b.reshape(nch, c, HG_DK)
        tot = b3[:, 0:1, :] if reverse else b3[:, c - 1:c, :]
        b_tot = jnp.broadcast_to(tot, (nch, c, HG_DK)).reshape(blk, HG_DK)
        return b, b_tot, jnp.exp2(tot)

    def increments(key, b, b_tot, v):
        k_dec = (key * jnp.exp2(b_tot - b)).astype(BF16)
        v_t = v.astype(BF16).T
        upd = []
        for h in range(blk // 128):
            kd = k_dec[h * 128:(h + 1) * 128, :]
            kd4 = jnp.concatenate([kd] * 4, axis=1) * dmask_ref[...]
            u4 = _dot(v_t[:, h * 128:(h + 1) * 128], kd4)
            upd += [u4[:, j * HG_DK:(j + 1) * HG_DK] for j in range(4)]
        return upd

    def scores(q_pre, key, b):
        q_dec = (_silu(q_pre.astype(F32)) * jnp.exp2(b)).astype(BF16)
        k_inv = (key * jnp.exp2(-b)).astype(BF16)
        return q_dec, lax.dot_general(q_dec, k_inv, _NT, preferred_element_type=F32)

    def intra(s, v, cum_ref):
        return _dot(s.astype(BF16) * cum_ref[...], v.astype(BF16))

    def recur_pair(st, chains):
        outs = [[None] * nch for _ in chains]
        st = list(st)
        for step in range(nch):
            for d, (dec, upd, q_dec, o_intra) in enumerate(chains):
                n = nch - 1 - step if d == 1 else step
                if q_dec is not None:
                    rows = slice(n * c, (n + 1) * c)
                    outs[d][n] = o_intra[rows, :] + _dot(q_dec[rows, :], st[d].T.astype(BF16))
                st[d] = st[d] * dec[n] + upd[n]
        return [jnp.concatenate(o, axis=0) if o[0] is not None else None for o in outs], st

    lb_d = (lb[0:1, :], lb[1:2, :])
    cum = (cumf_ref, cumb_ref)
    f_lat = (ff_ref, fb_ref)
    o_scr = (of_scr, ob_scr)

    cv = ci_ref[0]
    ctx_chains = []
    for d, f_ref in enumerate((cff_ref, cfb_ref)):
        key, parts = gates(f_ref[0], lb_d[d])
        b, b_tot, dec = decays(parts, cum[d], d == 1)
        ctx_chains.append((dec, increments(key, b, b_tot, cv), None, None))
    _, st = recur_pair([jnp.zeros((HG_DV, HG_DK), F32)] * 2, ctx_chains)

    blocks_per_step = 2

    def lat_body(r, carry):
        chains = []
        for u in range(blocks_per_step):
            up = r * blocks_per_step + u
            chains.append((0, pl.ds(pl.multiple_of(up * blk, blk), blk)))
            chains.append((1, pl.ds(pl.multiple_of((n_blk - 1 - up) * blk, blk), blk)))
        g = [gates(f_lat[d][0, sl, :], lb_d[d]) for d, sl in chains]
        dk = [decays(g[i][1], cum[d], d == 1) for i, (d, sl) in enumerate(chains)]
        v = [i_ref[0, sl, :] for d, sl in chains]
        upd = [increments(g[i][0], dk[i][0], dk[i][1], v[i]) for i in range(len(chains))]
        qs = [scores(q_ref[0, sl, :], g[i][0], dk[i][0]) for i, (d, sl) in enumerate(chains)]
        o_in = [intra(qs[i][1], v[i], cum[d]) for i, (d, sl) in enumerate(chains)]
        st = list(carry)
        for u in range(blocks_per_step):
            pair = range(2 * u, 2 * u + 2)
            outs, st = recur_pair(st, [(dk[i][2], upd[i], qs[i][0], o_in[i]) for i in pair])
            for i in pair:
                d, sl = chains[i]
                o_scr[d][sl, :] = outs[d]
        return tuple(st)

    lax.fori_loop(0, n_blk // blocks_per_step, lat_body, tuple(st))

    ng = ng_ref[...]

    def fin_body(r, carry):
        sl = pl.ds(pl.multiple_of(r * blk, blk), blk)
        o = of_scr[sl, :] + ob_scr[sl, :]
        y = o * lax.rsqrt(jnp.mean(o * o, axis=-1, keepdims=True) + RMS_EPS) * ng
        o_ref[0, sl, :] = (y * _silu(g_ref[0, sl, :].astype(F32))).astype(o_ref.dtype)
        return carry

    lax.fori_loop(0, n_blk, fin_body, 0, unroll=4)


def _hgrn_mixer(p_lat, p_ctx, ctx_tiles, hg_lb, norm_g, c_rows, w_mod, b_mod, mod_col0):
    bsz, t_len, _ = p_lat.shape
    l_len = p_ctx.shape[1]
    w = HG_DK
    d, n_mod = w_mod.shape
    mod_tn = (n_mod - mod_col0) // (bsz * HG_HEADS)
    assert mod_tn % 128 == 0 and mod_col0 % mod_tn == 0 and mod_tn * bsz * HG_HEADS == n_mod - mod_col0

    def lat(group):
        return pl.BlockSpec((1, t_len, w), lambda b, h: (b, 0, group * HG_HEADS + h))

    def ctx(group):
        blk0 = _packed_col(ctx_tiles, group * HG_HEADS * w) // w
        return pl.BlockSpec((1, l_len, w), lambda b, h: (b, 0, blk0 + h))

    def const(shape):
        return pl.BlockSpec(shape, lambda b, h: (0,) * len(shape))

    lower, upper, same = _chunk_masks()
    diag4 = np.tile(same[:128, ::HG_CHUNK][:, :4, None], (1, 1, HG_DK)).reshape(128, 4 * HG_DK)
    seq_f32 = pltpu.VMEM((t_len, w), F32)
    return pl.pallas_call(
        _hgrn_kernel,
        grid=(bsz, HG_HEADS),
        in_specs=[
            lat(0), lat(1), lat(2), lat(3), lat(4),
            ctx(1), ctx(2), ctx(3),
            pl.BlockSpec((hg_lb.shape[0], 2, w), lambda b, h: (0, 0, h)),
            const((1, HG_DV)),
            const((HG_BLOCK, HG_BLOCK)), const((HG_BLOCK, HG_BLOCK)),
            const((128, 4 * HG_DK)),
            const((MOD_ROWS, d)),
            pl.BlockSpec((d, mod_tn), lambda b, h: (0, mod_col0 // mod_tn + b * HG_HEADS + h)),
            pl.BlockSpec((1, mod_tn), lambda b, h: (0, mod_col0 // mod_tn + b * HG_HEADS + h)),
        ],
        out_specs=[pl.BlockSpec((1, t_len, HG_DV), lambda b, h: (b, 0, h)),
                   pl.BlockSpec((MOD_ROWS, mod_tn), lambda b, h: (0, b * HG_HEADS + h))],
        out_shape=[jax.ShapeDtypeStruct((bsz, t_len, HG_WIDTH), BF16),
                   jax.ShapeDtypeStruct((MOD_ROWS, n_mod - mod_col0), F32)],
        scratch_shapes=[seq_f32, seq_f32],
        compiler_params=_params("parallel", "parallel"),
        name="hgrn2_mixer",
    )(p_lat, p_lat, p_lat, p_lat, p_lat, p_ctx, p_ctx, p_ctx, hg_lb, norm_g.reshape(1, HG_DV),
      jnp.asarray(lower, BF16), jnp.asarray(upper, BF16), jnp.asarray(diag4, BF16),
      c_rows, w_mod, b_mod.reshape(1, n_mod))


ATT_QROWS = ATT_BLOCK


def _attn_kernel(q_ref, kvp_ref, kvc_ref, kvn_ref, ckv_ref, sink_ref, o_ref):
    n = pl.program_id(1)
    n_steps = pl.num_programs(1)
    rows = q_ref.shape[1]
    blk = ATT_BLOCK
    hd = HEAD_DIM
    l_len = ckv_ref.shape[1]
    band = rows + 2 * blk
    n_keys = band + l_len

    qi = lax.broadcasted_iota(jnp.int32, (rows, n_keys), 0)
    kj = lax.broadcasted_iota(jnp.int32, (rows, n_keys), 1)
    k_pos = n * rows - blk + kj
    ok = (kj >= band) | ((jnp.abs(kj - blk - qi) <= WINDOW) & (k_pos >= 0) & (k_pos < n_steps * rows))
    bias = jnp.where(ok, 0.0, NEG_INF)
    ones = jnp.ones((n_keys, hd), BF16)

    def gather(cols):
        return jnp.concatenate([kvp_ref[0, rows - blk:, cols], kvc_ref[0, :, cols], kvn_ref[0, :blk, cols],
                                ckv_ref[0, :, cols]], axis=0)

    k_all, v_aug = [], []
    for j in range(ATT_KV_HEADS):
        k_all.append(gather(slice(j * hd, (j + 1) * hd)))
        v_all = gather(slice((ATT_KV_HEADS + j) * hd, (ATT_KV_HEADS + j + 1) * hd))
        v_aug.append(jnp.concatenate([v_all, ones], axis=1))
    heads = range(ATT_HEADS)
    s = [lax.dot_general(q_ref[0, :, h * hd:(h + 1) * hd], k_all[h // ATT_GROUP], _NT,
                         preferred_element_type=F32) + bias for h in heads]
    sink = [sink_ref[:, h:h + 1] for h in heads]
    m = [jnp.maximum(jnp.max(s[h], axis=-1, keepdims=True), sink[h]) for h in heads]
    e = [jnp.exp(s[h] - m[h]).astype(BF16) for h in heads]
    oa = [_dot(e[h], v_aug[h // ATT_GROUP]) for h in heads]
    for h in heads:
        o = oa[h][:, :hd] / (oa[h][:, hd:] + jnp.exp(sink[h] - m[h]))
        o_ref[0, :, h * hd:(h + 1) * hd] = o.astype(o_ref.dtype)


def _window_attention(p_lat, p_ctx, ctx_tiles, sink):
    bsz, t_len, _ = p_lat.shape
    l_len = p_ctx.shape[1]
    nb = t_len // ATT_QROWS
    kvw = 2 * ATT_KV_HEADS * HEAD_DIM
    q_blk = HG_COLS // ATT_WIDTH
    kv_blk = (HG_COLS + ATT_WIDTH) // kvw
    ckv_blk = _packed_col(ctx_tiles, HG_COLS + ATT_WIDTH) // kvw

    def prev(n):
        return jnp.maximum(n - 1, 0)

    def nxt(n):
        return jnp.minimum(n + 1, nb - 1)

    def kv_spec(shift):
        return pl.BlockSpec((1, ATT_QROWS, kvw), lambda b, n: (b, shift(n), kv_blk))

    same = lambda n: n
    return pl.pallas_call(
        _attn_kernel,
        grid=(bsz, nb),
        in_specs=[
            pl.BlockSpec((1, ATT_QROWS, ATT_WIDTH), lambda b, n: (b, n, q_blk)),
            kv_spec(prev), kv_spec(same), kv_spec(nxt),
            pl.BlockSpec((1, l_len, kvw), lambda b, n: (b, 0, ckv_blk)),
            pl.BlockSpec((1, ATT_HEADS), lambda b, n: (0, 0)),
        ],
        out_specs=pl.BlockSpec((1, ATT_QROWS, ATT_WIDTH), lambda b, n: (b, n, 0)),
        out_shape=jax.ShapeDtypeStruct((bsz, t_len, ATT_WIDTH), BF16),
        compiler_params=_params("parallel", "parallel"),
        name="window_gqa",
    )(p_lat, p_lat, p_lat, p_lat, p_ctx, sink.reshape(1, ATT_HEADS))


def _outproj_kernel(x_ref, mod_ref, hg_ref, at_ref, w_ref, o_ref, w_scr):
    kh = hg_ref.shape[1]

    @pl.when(pl.program_id(0) == 0)
    def _():
        rows = 256

        def body(r, carry):
            sl = pl.ds(pl.multiple_of(r * rows, rows), rows)
            w_scr[sl, :] = w_ref[sl, :].astype(BF16)
            return carry

        lax.fori_loop(0, w_ref.shape[0] // rows, body, 0)

    acc = _dot(hg_ref[...], w_scr[:kh, :]) + _dot(at_ref[...], w_scr[kh:, :])
    o_ref[...] = x_ref[...] + mod_ref[0, 2:3, :] * acc


def _out_projection(x2d, mod, hg2d, at2d, w_out, tokens_per_mod, tm=512):
    m, d = x2d.shape
    kh = hg2d.shape[1]
    tiles_per_mod = tokens_per_mod // tm
    return pl.pallas_call(
        _outproj_kernel,
        grid=(m // tm,),
        in_specs=[
            pl.BlockSpec((tm, d), lambda i: (i, 0)),
            pl.BlockSpec((1, N_MOD, d), lambda i: (i // tiles_per_mod, 0, 0)),
            pl.BlockSpec((tm, kh), lambda i: (i, 0)),
            pl.BlockSpec((tm, kh), lambda i: (i, 0)),
            pl.BlockSpec(w_out.shape, lambda i: (0, 0), pipeline_mode=pl.Buffered(1)),
        ],
        out_specs=pl.BlockSpec((tm, d), lambda i: (i, 0)),
        out_shape=jax.ShapeDtypeStruct((m, d), F32),
        scratch_shapes=[pltpu.VMEM(w_out.shape, BF16)],
        compiler_params=_params("arbitrary"),
        name="out_proj",
    )(x2d, mod, hg2d, at2d, w_out)


def _ffn_kernel(x_ref, mod_ref, g_ref, wg_ref, wu_ref, wd_ref, o_ref, h_scr):
    j = pl.program_id(1)

    @pl.when(j == 0)
    def _():
        _norm_modulate_into(x_ref, g_ref, mod_ref[0, 3:4, :], mod_ref[0, 4:5, :], h_scr)

    def gated_down():
        h = h_scr[...]
        gate = _dot(h, wg_ref[...].astype(BF16))
        up = _dot(h, wu_ref[...].astype(BF16))
        act = (_silu(gate) * up).astype(BF16)
        return mod_ref[0, 5:6, :] * _dot(act, wd_ref[...].astype(BF16))

    @pl.when(j == 0)
    def _():
        o_ref[...] = x_ref[...] + gated_down()

    @pl.when(j > 0)
    def _():
        o_ref[...] += gated_down()


def _ffn(x2d, mod, norm_g, w_gate_up, w_down, tokens_per_mod, tm=1024, tf=256):
    m, d = x2d.shape
    d_ff = w_down.shape[0]
    nf = d_ff // tf
    tiles_per_mod = tokens_per_mod // tm
    return pl.pallas_call(
        _ffn_kernel,
        grid=(m // tm, nf),
        in_specs=[
            pl.BlockSpec((tm, d), lambda i, j: (i, 0)),
            pl.BlockSpec((1, N_MOD, d), lambda i, j: (i // tiles_per_mod, 0, 0)),
            pl.BlockSpec((1, d), lambda i, j: (0, 0)),
            pl.BlockSpec((d, tf), lambda i, j: (0, j)),
            pl.BlockSpec((d, tf), lambda i, j: (0, nf + j)),
            pl.BlockSpec((tf, d), lambda i, j: (j, 0)),
        ],
        out_specs=pl.BlockSpec((tm, d), lambda i, j: (i, 0)),
        out_shape=jax.ShapeDtypeStruct((m, d), F32),
        scratch_shapes=[pltpu.VMEM((tm, d), BF16)],
        compiler_params=_params("parallel", "arbitrary"),
        name="swiglu_ffn",
    )(x2d, mod, norm_g.reshape(1, d), w_gate_up, w_gate_up, w_down)


def kernel(x, c, ctx, c_ctx, w_mod, b_mod, norm_mix_g, norm_ffn_g, w_in, hg_lb, hg_norm_g,
           q_norm_g, k_norm_g, attn_sink, w_out, w_gate_up, w_down):
    bsz, t_len, d = x.shape
    l_len = ctx.shape[1]
    depth = w_mod.shape[0]
    assert depth == 1, "single-layer block: the context stream never needs its own outputs"
    assert bsz + 1 <= MOD_ROWS

    c_rows = jnp.concatenate([c, c_ctx[None, :], jnp.zeros((MOD_ROWS - bsz - 1, d), F32)], axis=0)
    n_early = 2 * d
    mod_early = _modulation(c_rows, w_mod[0], b_mod[0], n_early)
    mod_mix = mod_early.reshape(MOD_ROWS, 2, d)

    cos, sin = _rope_tables(t_len)
    lat_tiles = _tiles_covering([(0, IN_COLS)])
    hk = HG_HEADS * HG_DK
    ctx_tiles = _tiles_covering([(hk, 3 * hk + HG_WIDTH), (HG_COLS + ATT_WIDTH, IN_COLS)])
    p_lat = _in_projection(x.reshape(bsz * t_len, d), mod_mix[:bsz], norm_mix_g[0], w_in[0], q_norm_g[0],
                           k_norm_g[0], cos, sin, tokens_per_mod=t_len, tm=1024, rope=True, col_tiles=lat_tiles)
    p_ctx = _in_projection(ctx.reshape(bsz * l_len, d), mod_mix[bsz:bsz + 1], norm_mix_g[0], w_in[0], q_norm_g[0],
                           k_norm_g[0], cos, sin, tokens_per_mod=bsz * l_len, tm=bsz * l_len, rope=False,
                           col_tiles=ctx_tiles)
    p_lat = p_lat.reshape(bsz, t_len, -1)
    p_ctx = p_ctx.reshape(bsz, l_len, -1)

    hg, mod_late = _hgrn_mixer(p_lat, p_ctx, ctx_tiles, hg_lb, hg_norm_g[0], c_rows, w_mod[0], b_mod[0], n_early)
    mod_lat = jnp.concatenate([mod_early, mod_late], axis=1).reshape(MOD_ROWS, N_MOD, d)[:bsz]
    at = _window_attention(p_lat, p_ctx, ctx_tiles, attn_sink[0])

    x1 = _out_projection(x.reshape(bsz * t_len, d), mod_lat, hg.reshape(bsz * t_len, HG_WIDTH),
                         at.reshape(bsz * t_len, ATT_WIDTH), w_out[0], tokens_per_mod=t_len)
    x2 = _ffn(x1, mod_lat, norm_ffn_g[0], w_gate_up[0], w_down[0], tokens_per_mod=t_len)
    return x2.reshape(bsz, t_len, d)
```

```python
import functools

import jax
import jax.numpy as jnp
import numpy as np
from jax import lax
from jax.experimental import pallas as pl
from jax.experimental.pallas import tpu as pltpu

F32 = jnp.float32
BF16 = jnp.bfloat16

GRID_W = 64
HG_HEADS = 8
HG_DK = 128
HG_DV = 128
HG_CHUNK = 32
ATT_HEADS = 8
ATT_KV_HEADS = 2
ATT_GROUP = ATT_HEADS // ATT_KV_HEADS
HEAD_DIM = 128
WINDOW = 128
ATT_BLOCK = 128
ROPE_THETA = 10000.0
N_MOD = 6
RMS_EPS = 1e-6
NEG_INF = -1e30
HG_WIDTH = HG_HEADS * HG_DV
ATT_WIDTH = ATT_HEADS * HEAD_DIM
HG_COLS = 3 * HG_HEADS * HG_DK + 2 * HG_HEADS * HG_DV
ATT_COLS = (ATT_HEADS + 2 * ATT_KV_HEADS) * HEAD_DIM
IN_COLS = HG_COLS + ATT_COLS

VMEM_LIMIT_BYTES = 58 * 1024 * 1024
MOD_ROWS = 8

_NT = (((1,), (1,)), ((), ()))
_TN = (((0,), (0,)), ((), ()))


def _dot(a, b):
    return jnp.dot(a, b, preferred_element_type=F32)


def _silu(x):
    h = 0.5 * x
    return h * jnp.tanh(h) + h


def _params(*sem):
    return pltpu.CompilerParams(dimension_semantics=sem, vmem_limit_bytes=VMEM_LIMIT_BYTES)


def _mod_kernel(c_ref, w_ref, b_ref, o_ref):
    a = _silu(c_ref[...]).astype(BF16)
    o_ref[...] = _dot(a, w_ref[...].astype(BF16)) + b_ref[...]


def _modulation(c_rows, w_mod, b_mod, n_cols, tn=1024):
    d, n = w_mod.shape
    return pl.pallas_call(
        _mod_kernel,
        grid=(n_cols // tn,),
        in_specs=[
            pl.BlockSpec((MOD_ROWS, d), lambda j: (0, 0)),
            pl.BlockSpec((d, tn), lambda j: (0, j)),
            pl.BlockSpec((1, tn), lambda j: (0, j)),
        ],
        out_specs=pl.BlockSpec((MOD_ROWS, tn), lambda j: (0, j)),
        out_shape=jax.ShapeDtypeStruct((MOD_ROWS, n_cols), F32),
        compiler_params=_params("arbitrary"),
        name="adaln_mod",
    )(c_rows, w_mod, b_mod.reshape(1, n))


def _rope_tables(t_len):
    half = HEAD_DIM // 2
    nf = half // 2
    pos = np.arange(t_len)
    inv_freq = ROPE_THETA ** (-np.arange(nf, dtype=np.float64) / nf)
    ang_r = (pos // GRID_W)[:, None] * inv_freq
    ang_c = (pos % GRID_W)[:, None] * inv_freq
    cos = np.concatenate([np.cos(ang_r)] * 2 + [np.cos(ang_c)] * 2, axis=-1)
    sin = np.concatenate([-np.sin(ang_r), np.sin(ang_r), -np.sin(ang_c), np.sin(ang_c)], axis=-1)
    pair = lambda a: jnp.asarray(np.tile(a, (1, HEAD_PAIR // HEAD_DIM)), F32)
    return pair(cos), pair(sin)


HEAD_PAIR = 2 * HEAD_DIM


def _pair_matrices():
    quarter = HEAD_DIM // 4
    lane = np.arange(HEAD_PAIR)
    src = np.where(lane % (2 * quarter) < quarter, lane + quarter, lane - quarter)
    swap = lane[:, None] == src[None, :]
    head_mean = ((lane[:, None] // HEAD_DIM) == (lane[None, :] // HEAD_DIM)) / HEAD_DIM
    return jnp.asarray(swap, BF16), jnp.asarray(head_mean, BF16)


def _norm_modulate_into(x_ref, g_ref, shift, scale, h_scr, rows=128):
    tm = x_ref.shape[0]
    gain = g_ref[...] * (1.0 + scale)

    def body(r, carry):
        sl = pl.ds(pl.multiple_of(r * rows, rows), rows)
        x = x_ref[sl, :]
        ms = jnp.mean(x * x, axis=-1, keepdims=True)
        h_scr[sl, :] = (x * lax.rsqrt(ms + RMS_EPS) * gain + shift).astype(BF16)
        return carry

    lax.fori_loop(0, tm // rows, body, 0)


def _inproj_kernel(tiles_ref, x_ref, mod_ref, g_ref, w_ref, cos_ref, sin_ref, swap_ref, hmean_ref, qg_ref, kg_ref,
                   o_ref, h_scr, *, rope):
    j = pl.program_id(1)
    t = tiles_ref[j]
    tn = o_ref.shape[1]
    hd = HEAD_DIM
    q_tile0 = HG_COLS // tn
    kv_tile = (HG_COLS + ATT_WIDTH) // tn

    @pl.when(j == 0)
    def _():
        _norm_modulate_into(x_ref, g_ref, mod_ref[0, 0:1, :], mod_ref[0, 1:2, :], h_scr)

    acc = _dot(h_scr[...], w_ref[...].astype(BF16))

    def qk_prep(n_heads, g_row, scale):
        cols = [slice(p * HEAD_PAIR, (p + 1) * HEAD_PAIR) for p in range(n_heads * hd // HEAD_PAIR)]
        x = [acc[:, c] for c in cols]
        gain = g_row * scale
        ms = [_dot((xh * xh).astype(BF16), hmean_ref[...]) for xh in x]
        y = [xh * lax.rsqrt(m + RMS_EPS) * gain for xh, m in zip(x, ms)]
        if rope:
            swapped = [_dot(yh.astype(BF16), swap_ref[...]) for yh in y]
            y = [yh * cos_ref[...] + sh * sin_ref[...] for yh, sh in zip(y, swapped)]
        for c, yh in zip(cols, y):
            o_ref[:, c] = yh.astype(o_ref.dtype)

    @pl.when(t < q_tile0)
    def _():
        o_ref[...] = acc.astype(o_ref.dtype)

    @pl.when((t >= q_tile0) & (t < kv_tile))
    def _():
        qk_prep(tn // hd, qg_ref[...], hd ** -0.5)

    @pl.when(t == kv_tile)
    def _():
        qk_prep(ATT_KV_HEADS, kg_ref[...], 1.0)
        o_ref[:, ATT_KV_HEADS * hd:] = acc[:, ATT_KV_HEADS * hd:].astype(o_ref.dtype)


IN_TILE = 512


def _tiles_covering(col_ranges):
    return tuple(t for t in range(IN_COLS // IN_TILE)
                 if any(lo < (t + 1) * IN_TILE and t * IN_TILE < hi for lo, hi in col_ranges))


def _packed_col(col_tiles, col):
    return col_tiles.index(col // IN_TILE) * IN_TILE + col % IN_TILE


def _in_projection(x2d, mod, norm_g, w_in, q_norm_g, k_norm_g, cos, sin, tokens_per_mod, tm, rope, col_tiles):
    m, d = x2d.shape
    tn = IN_TILE
    assert HG_COLS % tn == 0 and ATT_WIDTH % tn == 0 and 2 * ATT_KV_HEADS * HEAD_DIM == tn
    tiles_per_mod = tokens_per_mod // tm
    tiles_per_seq = cos.shape[0] // tm
    vec = pl.BlockSpec((1, HEAD_PAIR), lambda i, j, tiles: (0, 0))
    tab = pl.BlockSpec((tm, HEAD_PAIR), lambda i, j, tiles: (i % tiles_per_seq, 0))
    mat = pl.BlockSpec((HEAD_PAIR, HEAD_PAIR), lambda i, j, tiles: (0, 0))
    per_pair = lambda g: jnp.tile(g.reshape(1, HEAD_DIM), (1, HEAD_PAIR // HEAD_DIM))
    grid_spec = pltpu.PrefetchScalarGridSpec(
        num_scalar_prefetch=1,
        grid=(m // tm, len(col_tiles)),
        in_specs=[
            pl.BlockSpec((tm, d), lambda i, j, tiles: (i, 0)),
            pl.BlockSpec((1, mod.shape[1], d), lambda i, j, tiles: (i // tiles_per_mod, 0, 0)),
            pl.BlockSpec((1, d), lambda i, j, tiles: (0, 0)),
            pl.BlockSpec((d, tn), lambda i, j, tiles: (0, tiles[j])),
            tab, tab, mat, mat, vec, vec,
        ],
        out_specs=pl.BlockSpec((tm, tn), lambda i, j, tiles: (i, j)),
        scratch_shapes=[pltpu.VMEM((tm, d), BF16)],
    )
    return pl.pallas_call(
        functools.partial(_inproj_kernel, rope=rope),
        grid_spec=grid_spec,
        out_shape=jax.ShapeDtypeStruct((m, len(col_tiles) * tn), BF16),
        compiler_params=_params("parallel", "arbitrary"),
        name="in_proj",
    )(jnp.asarray(col_tiles, jnp.int32), x2d, mod, norm_g.reshape(1, d), w_in, cos, sin,
      *_pair_matrices(), per_pair(q_norm_g), per_pair(k_norm_g))


HG_BLOCK = 256
CAST_ROWS = 16


def _cast_rows(src_ref, dst_ref):
    def body(r, carry):
        sl = pl.ds(pl.multiple_of(r * CAST_ROWS, CAST_ROWS), CAST_ROWS)
        dst_ref[sl, :] = src_ref[sl, :].astype(dst_ref.dtype)
        return carry

    lax.fori_loop(0, src_ref.shape[0] // CAST_ROWS, body, 0)


def _chunk_masks():
    r = np.arange(HG_BLOCK)[:, None]
    c = np.arange(HG_BLOCK)[None, :]
    same = (r // HG_CHUNK) == (c // HG_CHUNK)
    return (same & (r >= c)), (same & (r <= c)), same


def _hgrn_kernel(q_ref, ff_ref, fb_ref, i_ref, g_ref, cff_ref, cfb_ref, ci_ref, lb_ref, ng_ref,
                 cumf_ref, cumb_ref, dmask_ref, c_ref, wm_ref, bm_ref, *rest, n_cast):
    cast_src = rest[:n_cast]
    o_ref, mod_ref = rest[n_cast:n_cast + 2]
    cast_dst = rest[n_cast + 2:2 * n_cast + 2]
    of_scr, ob_scr = rest[2 * n_cast + 2:]

    mod_ref[...] = _dot(_silu(c_ref[...]).astype(BF16), wm_ref[...].astype(BF16)) + bm_ref[...]
    for src, dst in zip(cast_src, cast_dst):
        _cast_rows(src, dst)

    t_len = q_ref.shape[1]
    l_len = ci_ref.shape[1]
    c = HG_CHUNK
    blk = HG_BLOCK
    nch = blk // c
    n_blk = t_len // blk
    assert l_len == blk

    a = lb_ref[...]
    e = jnp.exp(a - jnp.max(a, axis=0, keepdims=True))
    lb = e[0] / jnp.sum(e, axis=0)

    def gates(f_pre, lb_d):
        f = 0.5 * (1.0 + lb_d) + (0.5 * (1.0 - lb_d)) * jnp.tanh(0.5 * f_pre.astype(F32))
        log_f = jnp.log2(f)
        hi = lax.bitcast_convert_type(lax.bitcast_convert_type(log_f, jnp.uint32) & jnp.uint32(0xFFFF0000), F32)
        return 1.0 - f, jnp.concatenate([hi.astype(BF16), (log_f - hi).astype(BF16)], axis=1)

    def decays(parts, cum_ref, reverse):
        b = _dot(cum_ref[...], parts)
        b = b[:, 0:128] + b[:, 128:256]
        b3 = b.reshape(nch, c, HG_DK)
        tot = b3[:, 0:1, :] if reverse else b3[:, c - 1:c, :]
        b_tot = jnp.broadcast_to(tot, (nch, c, HG_DK)).reshape(blk, HG_DK)
        return b, b_tot, jnp.exp2(tot)

    def increments(key, b, b_tot, v):
        k_dec = (key * jnp.exp2(b_tot - b)).astype(BF16)
        v_t = v.astype(BF16).T
        upd = []
        for h in range(blk // 128):
            kd = k_dec[h * 128:(h + 1) * 128, :]
            kd4 = jnp.concatenate([kd] * 4, axis=1) * dmask_ref[...]
            u4 = _dot(v_t[:, h * 128:(h + 1) * 128], kd4)
            upd += [u4[:, j * HG_DK:(j + 1) * HG_DK] for j in range(4)]
        return upd

    def scores(q_pre, key, b):
        q_dec = (_silu(q_pre.astype(F32)) * jnp.exp2(b)).astype(BF16)
        k_inv = (key * jnp.exp2(-b)).astype(BF16)
        return q_dec, lax.dot_general(q_dec, k_inv, _NT, preferred_element_type=F32)

    def intra(s, v, cum_ref):
        return _dot(s.astype(BF16) * cum_ref[...], v.astype(BF16))

    def recur_pair(st, chains):
        outs = [[None] * nch for _ in chains]
        st = list(st)
        for step in range(nch):
            for d, (dec, upd, q_dec, o_intra) in enumerate(chains):
                n = nch - 1 - step if d == 1 else step
                if q_dec is not None:
                    rows = slice(n * c, (n + 1) * c)
                    outs[d][n] = o_intra[rows, :] + _dot(q_dec[rows, :], st[d].T.astype(BF16))
                st[d] = st[d] * dec[n] + upd[n]
        return [jnp.concatenate(o, axis=0) if o[0] is not None else None for o in outs], st

    lb_d = (lb[0:1, :], lb[1:2, :])
    cum = (cumf_ref, cumb_ref)
    f_lat = (ff_ref, fb_ref)
    o_scr = (of_scr, ob_scr)

    cv = ci_ref[0]
    ctx_chains = []
    for d, f_ref in enumerate((cff_ref, cfb_ref)):
        key, parts = gates(f_ref[0], lb_d[d])
        b, b_tot, dec = decays(parts, cum[d], d == 1)
        ctx_chains.append((dec, increments(key, b, b_tot, cv), None, None))
    _, st = recur_pair([jnp.zeros((HG_DV, HG_DK), F32)] * 2, ctx_chains)

    blocks_per_step = 2

    def lat_body(r, carry):
        chains = []
        for u in range(blocks_per_step):
            up = r * blocks_per_step + u
            chains.append((0, pl.ds(pl.multiple_of(up * blk, blk), blk)))
            chains.append((1, pl.ds(pl.multiple_of((n_blk - 1 - up) * blk, blk), blk)))
        g = [gates(f_lat[d][0, sl, :], lb_d[d]) for d, sl in chains]
        dk = [decays(g[i][1], cum[d], d == 1) for i, (d, sl) in enumerate(chains)]
        v = [i_ref[0, sl, :] for d, sl in chains]
        upd = [increments(g[i][0], dk[i][0], dk[i][1], v[i]) for i in range(len(chains))]
        qs = [scores(q_ref[0, sl, :], g[i][0], dk[i][0]) for i, (d, sl) in enumerate(chains)]
        o_in = [intra(qs[i][1], v[i], cum[d]) for i, (d, sl) in enumerate(chains)]
        st = list(carry)
        for u in range(blocks_per_step):
            pair = range(2 * u, 2 * u + 2)
            outs, st = recur_pair(st, [(dk[i][2], upd[i], qs[i][0], o_in[i]) for i in pair])
            for i in pair:
                d, sl = chains[i]
                o_scr[d][sl, :] = outs[d]
        return tuple(st)

    lax.fori_loop(0, n_blk // blocks_per_step, lat_body, tuple(st))

    ng = ng_ref[...]

    def fin_body(r, carry):
        sl = pl.ds(pl.multiple_of(r * blk, blk), blk)
        o = of_scr[sl, :] + ob_scr[sl, :]
        y = o * lax.rsqrt(jnp.mean(o * o, axis=-1, keepdims=True) + RMS_EPS) * ng
        o_ref[0, sl, :] = (y * _silu(g_ref[0, sl, :].astype(F32))).astype(o_ref.dtype)
        return carry

    lax.fori_loop(0, n_blk, fin_body, 0, unroll=4)


def _hgrn_mixer(p_lat, p_ctx, ctx_tiles, hg_lb, norm_g, c_rows, w_mod, b_mod, mod_col0, cast_weights):
    bsz, t_len, _ = p_lat.shape
    l_len = p_ctx.shape[1]
    w = HG_DK
    d, n_mod = w_mod.shape
    n_steps = bsz * HG_HEADS
    mod_tn = (n_mod - mod_col0) // n_steps
    assert mod_tn % 128 == 0 and mod_col0 % mod_tn == 0 and mod_tn * n_steps == n_mod - mod_col0
    assert all(m.shape[0] % (n_steps * CAST_ROWS) == 0 for m in cast_weights)

    def slab(m):
        return pl.BlockSpec((m.shape[0] // n_steps, m.shape[1]), lambda b, h: (b * HG_HEADS + h, 0))

    def lat(group):
        return pl.BlockSpec((1, t_len, w), lambda b, h: (b, 0, group * HG_HEADS + h))

    def ctx(group):
        blk0 = _packed_col(ctx_tiles, group * HG_HEADS * w) // w
        return pl.BlockSpec((1, l_len, w), lambda b, h: (b, 0, blk0 + h))

    def const(shape):
        return pl.BlockSpec(shape, lambda b, h: (0,) * len(shape))

    lower, upper, same = _chunk_masks()
    diag4 = np.tile(same[:128, ::HG_CHUNK][:, :4, None], (1, 1, HG_DK)).reshape(128, 4 * HG_DK)
    seq_f32 = pltpu.VMEM((t_len, w), F32)
    return pl.pallas_call(
        functools.partial(_hgrn_kernel, n_cast=len(cast_weights)),
        grid=(bsz, HG_HEADS),
        in_specs=[
            lat(0), lat(1), lat(2), lat(3), lat(4),
            ctx(1), ctx(2), ctx(3),
            pl.BlockSpec((hg_lb.shape[0], 2, w), lambda b, h: (0, 0, h)),
            const((1, HG_DV)),
            const((HG_BLOCK, HG_BLOCK)), const((HG_BLOCK, HG_BLOCK)),
            const((128, 4 * HG_DK)),
            const((MOD_ROWS, d)),
            pl.BlockSpec((d, mod_tn), lambda b, h: (0, mod_col0 // mod_tn + b * HG_HEADS + h)),
            pl.BlockSpec((1, mod_tn), lambda b, h: (0, mod_col0 // mod_tn + b * HG_HEADS + h)),
        ] + [slab(m) for m in cast_weights],
        out_specs=[pl.BlockSpec((1, t_len, HG_DV), lambda b, h: (b, 0, h)),
                   pl.BlockSpec((MOD_ROWS, mod_tn), lambda b, h: (0, b * HG_HEADS + h))]
        + [slab(m) for m in cast_weights],
        out_shape=[jax.ShapeDtypeStruct((bsz, t_len, HG_WIDTH), BF16),
                   jax.ShapeDtypeStruct((MOD_ROWS, n_mod - mod_col0), F32)]
        + [jax.ShapeDtypeStruct(m.shape, BF16) for m in cast_weights],
        scratch_shapes=[seq_f32, seq_f32],
        compiler_params=_params("parallel", "parallel"),
        name="hgrn2_mixer",
    )(p_lat, p_lat, p_lat, p_lat, p_lat, p_ctx, p_ctx, p_ctx, hg_lb, norm_g.reshape(1, HG_DV),
      jnp.asarray(lower, BF16), jnp.asarray(upper, BF16), jnp.asarray(diag4, BF16),
      c_rows, w_mod, b_mod.reshape(1, n_mod), *cast_weights)


ATT_QROWS = ATT_BLOCK


def _attn_kernel(q_ref, kvp_ref, kvc_ref, kvn_ref, ckv_ref, sink_ref, o_ref):
    n = pl.program_id(1)
    n_steps = pl.num_programs(1)
    rows = q_ref.shape[1]
    blk = ATT_BLOCK
    hd = HEAD_DIM
    l_len = ckv_ref.shape[1]
    band = rows + 2 * blk
    n_keys = band + l_len

    qi = lax.broadcasted_iota(jnp.int32, (rows, n_keys), 0)
    kj = lax.broadcasted_iota(jnp.int32, (rows, n_keys), 1)
    k_pos = n * rows - blk + kj
    ok = (kj >= band) | ((jnp.abs(kj - blk - qi) <= WINDOW) & (k_pos >= 0) & (k_pos < n_steps * rows))
    bias = jnp.where(ok, 0.0, NEG_INF)
    ones = jnp.ones((n_keys, hd), BF16)

    def gather(cols):
        return jnp.concatenate([kvp_ref[0, rows - blk:, cols], kvc_ref[0, :, cols], kvn_ref[0, :blk, cols],
                                ckv_ref[0, :, cols]], axis=0)

    k_all, v_aug = [], []
    for j in range(ATT_KV_HEADS):
        k_all.append(gather(slice(j * hd, (j + 1) * hd)))
        v_all = gather(slice((ATT_KV_HEADS + j) * hd, (ATT_KV_HEADS + j + 1) * hd))
        v_aug.append(jnp.concatenate([v_all, ones], axis=1))
    heads = range(ATT_HEADS)
    s = [lax.dot_general(q_ref[0, :, h * hd:(h + 1) * hd], k_all[h // ATT_GROUP], _NT,
                         preferred_element_type=F32) + bias for h in heads]
    sink = [sink_ref[:, h:h + 1] for h in heads]
    m = [jnp.maximum(jnp.max(s[h], axis=-1, keepdims=True), sink[h]) for h in heads]
    e = [jnp.exp(s[h] - m[h]).astype(BF16) for h in heads]
    oa = [_dot(e[h], v_aug[h // ATT_GROUP]) for h in heads]
    for h in heads:
        o = oa[h][:, :hd] / (oa[h][:, hd:] + jnp.exp(sink[h] - m[h]))
        o_ref[0, :, h * hd:(h + 1) * hd] = o.astype(o_ref.dtype)


def _window_attention(p_lat, p_ctx, ctx_tiles, sink):
    bsz, t_len, _ = p_lat.shape
    l_len = p_ctx.shape[1]
    nb = t_len // ATT_QROWS
    kvw = 2 * ATT_KV_HEADS * HEAD_DIM
    q_blk = HG_COLS // ATT_WIDTH
    kv_blk = (HG_COLS + ATT_WIDTH) // kvw
    ckv_blk = _packed_col(ctx_tiles, HG_COLS + ATT_WIDTH) // kvw

    def prev(n):
        return jnp.maximum(n - 1, 0)

    def nxt(n):
        return jnp.minimum(n + 1, nb - 1)

    def kv_spec(shift):
        return pl.BlockSpec((1, ATT_QROWS, kvw), lambda b, n: (b, shift(n), kv_blk))

    same = lambda n: n
    return pl.pallas_call(
        _attn_kernel,
        grid=(bsz, nb),
        in_specs=[
            pl.BlockSpec((1, ATT_QROWS, ATT_WIDTH), lambda b, n: (b, n, q_blk)),
            kv_spec(prev), kv_spec(same), kv_spec(nxt),
            pl.BlockSpec((1, l_len, kvw), lambda b, n: (b, 0, ckv_blk)),
            pl.BlockSpec((1, ATT_HEADS), lambda b, n: (0, 0)),
        ],
        out_specs=pl.BlockSpec((1, ATT_QROWS, ATT_WIDTH), lambda b, n: (b, n, 0)),
        out_shape=jax.ShapeDtypeStruct((bsz, t_len, ATT_WIDTH), BF16),
        compiler_params=_params("parallel", "parallel"),
        name="window_gqa",
    )(p_lat, p_lat, p_lat, p_lat, p_ctx, sink.reshape(1, ATT_HEADS))


def _outproj_kernel(x_ref, mod_ref, hg_ref, at_ref, w_ref, o_ref):
    kh = hg_ref.shape[1]
    acc = _dot(hg_ref[...], w_ref[:kh, :]) + _dot(at_ref[...], w_ref[kh:, :])
    o_ref[...] = x_ref[...] + mod_ref[0, 2:3, :] * acc


def _out_projection(x2d, mod, hg2d, at2d, w_out, tokens_per_mod, tm=512):
    m, d = x2d.shape
    kh = hg2d.shape[1]
    tiles_per_mod = tokens_per_mod // tm
    return pl.pallas_call(
        _outproj_kernel,
        grid=(m // tm,),
        in_specs=[
            pl.BlockSpec((tm, d), lambda i: (i, 0)),
            pl.BlockSpec((1, N_MOD, d), lambda i: (i // tiles_per_mod, 0, 0)),
            pl.BlockSpec((tm, kh), lambda i: (i, 0)),
            pl.BlockSpec((tm, kh), lambda i: (i, 0)),
            pl.BlockSpec(w_out.shape, lambda i: (0, 0), pipeline_mode=pl.Buffered(1)),
        ],
        out_specs=pl.BlockSpec((tm, d), lambda i: (i, 0)),
        out_shape=jax.ShapeDtypeStruct((m, d), F32),
        compiler_params=_params("parallel"),
        name="out_proj",
    )(x2d, mod, hg2d, at2d, w_out)


def _ffn_kernel(x_ref, mod_ref, g_ref, wg_ref, wu_ref, wd_ref, o_ref, h_scr):
    j = pl.program_id(1)

    @pl.when(j == 0)
    def _():
        _norm_modulate_into(x_ref, g_ref, mod_ref[0, 3:4, :], mod_ref[0, 4:5, :], h_scr)

    def gated_down():
        h = h_scr[...]
        gate = _dot(h, wg_ref[...])
        up = _dot(h, wu_ref[...])
        act = (_silu(gate) * up).astype(BF16)
        return mod_ref[0, 5:6, :] * _dot(act, wd_ref[...])

    @pl.when(j == 0)
    def _():
        o_ref[...] = x_ref[...] + gated_down()

    @pl.when(j > 0)
    def _():
        o_ref[...] += gated_down()


def _ffn(x2d, mod, norm_g, w_gate_up, w_down, tokens_per_mod, tm=1024, tf=512):
    m, d = x2d.shape
    d_ff = w_down.shape[0]
    nf = d_ff // tf
    tiles_per_mod = tokens_per_mod // tm
    return pl.pallas_call(
        _ffn_kernel,
        grid=(m // tm, nf),
        in_specs=[
            pl.BlockSpec((tm, d), lambda i, j: (i, 0)),
            pl.BlockSpec((1, N_MOD, d), lambda i, j: (i // tiles_per_mod, 0, 0)),
            pl.BlockSpec((1, d), lambda i, j: (0, 0)),
            pl.BlockSpec((d, tf), lambda i, j: (0, j)),
            pl.BlockSpec((d, tf), lambda i, j: (0, nf + j)),
            pl.BlockSpec((tf, d), lambda i, j: (j, 0)),
        ],
        out_specs=pl.BlockSpec((tm, d), lambda i, j: (i, 0)),
        out_shape=jax.ShapeDtypeStruct((m, d), F32),
        scratch_shapes=[pltpu.VMEM((tm, d), BF16)],
        compiler_params=_params("parallel", "arbitrary"),
        name="swiglu_ffn",
    )(x2d, mod, norm_g.reshape(1, d), w_gate_up, w_gate_up, w_down)


def kernel(x, c, ctx, c_ctx, w_mod, b_mod, norm_mix_g, norm_ffn_g, w_in, hg_lb, hg_norm_g,
           q_norm_g, k_norm_g, attn_sink, w_out, w_gate_up, w_down):
    bsz, t_len, d = x.shape
    l_len = ctx.shape[1]
    depth = w_mod.shape[0]
    assert depth == 1, "single-layer block: the context stream never needs its own outputs"
    assert bsz + 1 <= MOD_ROWS

    c_rows = jnp.concatenate([c, c_ctx[None, :], jnp.zeros((MOD_ROWS - bsz - 1, d), F32)], axis=0)
    n_early = 2 * d
    mod_early = _modulation(c_rows, w_mod[0], b_mod[0], n_early)
    mod_mix = mod_early.reshape(MOD_ROWS, 2, d)

    cos, sin = _rope_tables(t_len)
    lat_tiles = _tiles_covering([(0, IN_COLS)])
    hk = HG_HEADS * HG_DK
    ctx_tiles = _tiles_covering([(hk, 3 * hk + HG_WIDTH), (HG_COLS + ATT_WIDTH, IN_COLS)])
    p_lat = _in_projection(x.reshape(bsz * t_len, d), mod_mix[:bsz], norm_mix_g[0], w_in[0], q_norm_g[0],
                           k_norm_g[0], cos, sin, tokens_per_mod=t_len, tm=1024, rope=True, col_tiles=lat_tiles)
    p_ctx = _in_projection(ctx.reshape(bsz * l_len, d), mod_mix[bsz:bsz + 1], norm_mix_g[0], w_in[0], q_norm_g[0],
                           k_norm_g[0], cos, sin, tokens_per_mod=bsz * l_len, tm=bsz * l_len, rope=False,
                           col_tiles=ctx_tiles)
    p_lat = p_lat.reshape(bsz, t_len, -1)
    p_ctx = p_ctx.reshape(bsz, l_len, -1)

    hg, mod_late, w_out16, w_gate_up16, w_down16 = _hgrn_mixer(
        p_lat, p_ctx, ctx_tiles, hg_lb, hg_norm_g[0], c_rows, w_mod[0], b_mod[0], n_early,
        cast_weights=(w_out[0], w_gate_up[0], w_down[0]))
    mod_lat = jnp.concatenate([mod_early, mod_late], axis=1).reshape(MOD_ROWS, N_MOD, d)[:bsz]
    at = _window_attention(p_lat, p_ctx, ctx_tiles, attn_sink[0])

    x1 = _out_projection(x.reshape(bsz * t_len, d), mod_lat, hg.reshape(bsz * t_len, HG_WIDTH),
                         at.reshape(bsz * t_len, ATT_WIDTH), w_out16, tokens_per_mod=t_len)
    x2 = _ffn(x1, mod_lat, norm_ffn_g[0], w_gate_up16, w_down16, tokens_per_mod=t_len)
    return x2.reshape(bsz, t_len, d)
```

```python
import functools

import jax
import jax.numpy as jnp
import numpy as np
from jax import lax
from jax.experimental import pallas as pl
from jax.experimental.pallas import tpu as pltpu

F32 = jnp.float32
BF16 = jnp.bfloat16

GRID_W = 64
HG_HEADS = 8
HG_DK = 128
HG_DV = 128
HG_CHUNK = 32
ATT_HEADS = 8
ATT_KV_HEADS = 2
ATT_GROUP = ATT_HEADS // ATT_KV_HEADS
HEAD_DIM = 128
WINDOW = 128
ATT_BLOCK = 128
ROPE_THETA = 10000.0
N_MOD = 6
RMS_EPS = 1e-6
NEG_INF = -1e30
HG_WIDTH = HG_HEADS * HG_DV
ATT_WIDTH = ATT_HEADS * HEAD_DIM
HG_COLS = 3 * HG_HEADS * HG_DK + 2 * HG_HEADS * HG_DV
ATT_COLS = (ATT_HEADS + 2 * ATT_KV_HEADS) * HEAD_DIM
IN_COLS = HG_COLS + ATT_COLS

VMEM_LIMIT_BYTES = 58 * 1024 * 1024
MOD_ROWS = 8

_NT = (((1,), (1,)), ((), ()))
_TN = (((0,), (0,)), ((), ()))


def _dot(a, b):
    return jnp.dot(a, b, preferred_element_type=F32)


def _silu(x):
    h = 0.5 * x
    return h * jnp.tanh(h) + h


def _params(*sem):
    return pltpu.CompilerParams(dimension_semantics=sem, vmem_limit_bytes=VMEM_LIMIT_BYTES)


def _mod_kernel(c_ref, w_ref, b_ref, o_ref):
    a = _silu(c_ref[...]).astype(BF16)
    o_ref[...] = _dot(a, w_ref[...].astype(BF16)) + b_ref[...]


def _modulation(c_rows, w_mod, b_mod, n_cols, tn=1024):
    d, n = w_mod.shape
    return pl.pallas_call(
        _mod_kernel,
        grid=(n_cols // tn,),
        in_specs=[
            pl.BlockSpec((MOD_ROWS, d), lambda j: (0, 0)),
            pl.BlockSpec((d, tn), lambda j: (0, j)),
            pl.BlockSpec((1, tn), lambda j: (0, j)),
        ],
        out_specs=pl.BlockSpec((MOD_ROWS, tn), lambda j: (0, j)),
        out_shape=jax.ShapeDtypeStruct((MOD_ROWS, n_cols), F32),
        compiler_params=_params("arbitrary"),
        name="adaln_mod",
    )(c_rows, w_mod, b_mod.reshape(1, n))


def _rope_tables(t_len):
    half = HEAD_DIM // 2
    nf = half // 2
    pos = np.arange(t_len)
    inv_freq = ROPE_THETA ** (-np.arange(nf, dtype=np.float64) / nf)
    ang_r = (pos // GRID_W)[:, None] * inv_freq
    ang_c = (pos % GRID_W)[:, None] * inv_freq
    cos = np.concatenate([np.cos(ang_r)] * 2 + [np.cos(ang_c)] * 2, axis=-1)
    sin = np.concatenate([-np.sin(ang_r), np.sin(ang_r), -np.sin(ang_c), np.sin(ang_c)], axis=-1)
    pair = lambda a: jnp.asarray(np.tile(a, (1, HEAD_PAIR // HEAD_DIM)), F32)
    return pair(cos), pair(sin)


HEAD_PAIR = 2 * HEAD_DIM


def _pair_matrices():
    quarter = HEAD_DIM // 4
    lane = np.arange(HEAD_PAIR)
    src = np.where(lane % (2 * quarter) < quarter, lane + quarter, lane - quarter)
    swap = lane[:, None] == src[None, :]
    head_mean = ((lane[:, None] // HEAD_DIM) == (lane[None, :] // HEAD_DIM)) / HEAD_DIM
    return jnp.asarray(swap, BF16), jnp.asarray(head_mean, BF16)


def _norm_modulate_into(x_ref, g_ref, shift, scale, h_scr, rows=128):
    tm = x_ref.shape[0]
    gain = g_ref[...] * (1.0 + scale)

    def body(r, carry):
        sl = pl.ds(pl.multiple_of(r * rows, rows), rows)
        x = x_ref[sl, :]
        ms = jnp.mean(x * x, axis=-1, keepdims=True)
        h_scr[sl, :] = (x * lax.rsqrt(ms + RMS_EPS) * gain + shift).astype(BF16)
        return carry

    lax.fori_loop(0, tm // rows, body, 0)


def _inproj_kernel(tiles_ref, x_ref, mod_ref, g_ref, w_ref, cos_ref, sin_ref, swap_ref, hmean_ref, qg_ref, kg_ref,
                   o_ref, h_scr, *, rope):
    j = pl.program_id(1)
    t = tiles_ref[j]
    tn = o_ref.shape[1]
    hd = HEAD_DIM
    q_tile0 = HG_COLS // tn
    kv_tile = (HG_COLS + ATT_WIDTH) // tn

    @pl.when(j == 0)
    def _():
        _norm_modulate_into(x_ref, g_ref, mod_ref[0, 0:1, :], mod_ref[0, 1:2, :], h_scr)

    def project():
        return _dot(h_scr[...], w_ref[...].astype(BF16))

    def qk_prep(acc, n_heads, g_row, scale):
        cols = [slice(p * HEAD_PAIR, (p + 1) * HEAD_PAIR) for p in range(n_heads * hd // HEAD_PAIR)]
        x = [acc[:, c] for c in cols]
        gain = g_row * scale
        ms = [_dot((xh * xh).astype(BF16), hmean_ref[...]) for xh in x]
        y = [xh * lax.rsqrt(m + RMS_EPS) * gain for xh, m in zip(x, ms)]
        if rope:
            swapped = [_dot(yh.astype(BF16), swap_ref[...]) for yh in y]
            y = [yh * cos_ref[...] + sh * sin_ref[...] for yh, sh in zip(y, swapped)]
        for c, yh in zip(cols, y):
            o_ref[:, c] = yh.astype(o_ref.dtype)

    @pl.when(t < q_tile0)
    def _():
        o_ref[...] = project().astype(o_ref.dtype)

    @pl.when((t >= q_tile0) & (t < kv_tile))
    def _():
        qk_prep(project(), tn // hd, qg_ref[...], hd ** -0.5)

    @pl.when(t == kv_tile)
    def _():
        acc = project()
        qk_prep(acc, ATT_KV_HEADS, kg_ref[...], 1.0)
        o_ref[:, ATT_KV_HEADS * hd:] = acc[:, ATT_KV_HEADS * hd:].astype(o_ref.dtype)


IN_TILE = 512


def _tiles_covering(col_ranges):
    return tuple(t for t in range(IN_COLS // IN_TILE)
                 if any(lo < (t + 1) * IN_TILE and t * IN_TILE < hi for lo, hi in col_ranges))


def _packed_col(col_tiles, col):
    return col_tiles.index(col // IN_TILE) * IN_TILE + col % IN_TILE


def _in_projection(x2d, mod, norm_g, w_in, q_norm_g, k_norm_g, cos, sin, tokens_per_mod, tm, rope, col_tiles):
    m, d = x2d.shape
    tn = IN_TILE
    assert HG_COLS % tn == 0 and ATT_WIDTH % tn == 0 and 2 * ATT_KV_HEADS * HEAD_DIM == tn
    tiles_per_mod = tokens_per_mod // tm
    tiles_per_seq = cos.shape[0] // tm
    vec = pl.BlockSpec((1, HEAD_PAIR), lambda i, j, tiles: (0, 0))
    tab = pl.BlockSpec((tm, HEAD_PAIR), lambda i, j, tiles: (i % tiles_per_seq, 0))
    mat = pl.BlockSpec((HEAD_PAIR, HEAD_PAIR), lambda i, j, tiles: (0, 0))
    per_pair = lambda g: jnp.tile(g.reshape(1, HEAD_DIM), (1, HEAD_PAIR // HEAD_DIM))
    grid_spec = pltpu.PrefetchScalarGridSpec(
        num_scalar_prefetch=1,
        grid=(m // tm, len(col_tiles)),
        in_specs=[
            pl.BlockSpec((tm, d), lambda i, j, tiles: (i, 0)),
            pl.BlockSpec((1, mod.shape[1], d), lambda i, j, tiles: (i // tiles_per_mod, 0, 0)),
            pl.BlockSpec((1, d), lambda i, j, tiles: (0, 0)),
            pl.BlockSpec((d, tn), lambda i, j, tiles: (0, tiles[j])),
            tab, tab, mat, mat, vec, vec,
        ],
        out_specs=pl.BlockSpec((tm, tn), lambda i, j, tiles: (i, j)),
        scratch_shapes=[pltpu.VMEM((tm, d), BF16)],
    )
    return pl.pallas_call(
        functools.partial(_inproj_kernel, rope=rope),
        grid_spec=grid_spec,
        out_shape=jax.ShapeDtypeStruct((m, len(col_tiles) * tn), BF16),
        compiler_params=_params("parallel", "arbitrary"),
        name="in_proj",
    )(jnp.asarray(col_tiles, jnp.int32), x2d, mod, norm_g.reshape(1, d), w_in, cos, sin,
      *_pair_matrices(), per_pair(q_norm_g), per_pair(k_norm_g))


HG_BLOCK = 256
CAST_ROWS = 16


def _cast_rows(src_ref, dst_ref):
    def body(r, carry):
        sl = pl.ds(pl.multiple_of(r * CAST_ROWS, CAST_ROWS), CAST_ROWS)
        dst_ref[sl, :] = src_ref[sl, :].astype(dst_ref.dtype)
        return carry

    lax.fori_loop(0, src_ref.shape[0] // CAST_ROWS, body, 0)


def _chunk_masks():
    r = np.arange(HG_BLOCK)[:, None]
    c = np.arange(HG_BLOCK)[None, :]
    same = (r // HG_CHUNK) == (c // HG_CHUNK)
    return (same & (r >= c)), (same & (r <= c)), same


def _hgrn_kernel(q_ref, ff_ref, fb_ref, i_ref, g_ref, cff_ref, cfb_ref, ci_ref, lb_ref, ng_ref,
                 cumf_ref, cumb_ref, dmask_ref, c_ref, wm_ref, bm_ref, *rest, n_cast):
    cast_src = rest[:n_cast]
    o_ref, mod_ref = rest[n_cast:n_cast + 2]
    cast_dst = rest[n_cast + 2:2 * n_cast + 2]
    of_scr, ob_scr = rest[2 * n_cast + 2:]

    mod_ref[...] = _dot(_silu(c_ref[...]).astype(BF16), wm_ref[...].astype(BF16)) + bm_ref[...]
    for src, dst in zip(cast_src, cast_dst):
        _cast_rows(src, dst)

    t_len = q_ref.shape[1]
    l_len = ci_ref.shape[1]
    c = HG_CHUNK
    blk = HG_BLOCK
    nch = blk // c
    n_blk = t_len // blk
    assert l_len == blk

    a = lb_ref[...]
    e = jnp.exp(a - jnp.max(a, axis=0, keepdims=True))
    lb = e[0] / jnp.sum(e, axis=0)

    def gates(f_pre, lb_d):
        f = 0.5 * (1.0 + lb_d) + (0.5 * (1.0 - lb_d)) * jnp.tanh(0.5 * f_pre.astype(F32))
        log_f = jnp.log2(f)
        hi = lax.bitcast_convert_type(lax.bitcast_convert_type(log_f, jnp.uint32) & jnp.uint32(0xFFFF0000), F32)
        return 1.0 - f, jnp.concatenate([hi.astype(BF16), (log_f - hi).astype(BF16)], axis=1)

    def decays(parts, cum_ref, reverse):
        b = _dot(cum_ref[...], parts)
        b = b[:, 0:128] + b[:, 128:256]
        b3 = b.reshape(nch, c, HG_DK)
        tot = b3[:, 0:1, :] if reverse else b3[:, c - 1:c, :]
        b_tot = jnp.broadcast_to(tot, (nch, c, HG_DK)).reshape(blk, HG_DK)
        return b, b_tot, jnp.exp2(tot)

    def increments(key, b, b_tot, v):
        k_dec = (key * jnp.exp2(b_tot - b)).astype(BF16)
        v_t = v.astype(BF16).T
        upd = []
        for h in range(blk // 128):
            kd = k_dec[h * 128:(h + 1) * 128, :]
            kd4 = jnp.concatenate([kd] * 4, axis=1) * dmask_ref[...]
            u4 = _dot(v_t[:, h * 128:(h + 1) * 128], kd4)
            upd += [u4[:, j * HG_DK:(j + 1) * HG_DK] for j in range(4)]
        return upd

    def scores(q_pre, key, b):
        q_dec = (_silu(q_pre.astype(F32)) * jnp.exp2(b)).astype(BF16)
        k_inv = (key * jnp.exp2(-b)).astype(BF16)
        return q_dec, lax.dot_general(q_dec, k_inv, _NT, preferred_element_type=F32)

    def intra(s, v, cum_ref):
        return _dot(s.astype(BF16) * cum_ref[...], v.astype(BF16))

    def recur_pair(st, chains):
        outs = [[None] * nch for _ in chains]
        st = list(st)
        for step in range(nch):
            for d, (dec, upd, q_dec, o_intra) in enumerate(chains):
                n = nch - 1 - step if d == 1 else step
                if q_dec is not None:
                    rows = slice(n * c, (n + 1) * c)
                    outs[d][n] = o_intra[rows, :] + _dot(q_dec[rows, :], st[d].T.astype(BF16))
                st[d] = st[d] * dec[n] + upd[n]
        return [jnp.concatenate(o, axis=0) if o[0] is not None else None for o in outs], st

    lb_d = (lb[0:1, :], lb[1:2, :])
    cum = (cumf_ref, cumb_ref)
    f_lat = (ff_ref, fb_ref)
    o_scr = (of_scr, ob_scr)

    cv = ci_ref[0]
    ctx_chains = []
    for d, f_ref in enumerate((cff_ref, cfb_ref)):
        key, parts = gates(f_ref[0], lb_d[d])
        b, b_tot, dec = decays(parts, cum[d], d == 1)
        ctx_chains.append((dec, increments(key, b, b_tot, cv), None, None))
    _, st = recur_pair([jnp.zeros((HG_DV, HG_DK), F32)] * 2, ctx_chains)

    blocks_per_step = 2

    def lat_body(r, carry):
        chains = []
        for u in range(blocks_per_step):
            up = r * blocks_per_step + u
            chains.append((0, pl.ds(pl.multiple_of(up * blk, blk), blk)))
            chains.append((1, pl.ds(pl.multiple_of((n_blk - 1 - up) * blk, blk), blk)))
        g = [gates(f_lat[d][0, sl, :], lb_d[d]) for d, sl in chains]
        dk = [decays(g[i][1], cum[d], d == 1) for i, (d, sl) in enumerate(chains)]
        v = [i_ref[0, sl, :] for d, sl in chains]
        upd = [increments(g[i][0], dk[i][0], dk[i][1], v[i]) for i in range(len(chains))]
        qs = [scores(q_ref[0, sl, :], g[i][0], dk[i][0]) for i, (d, sl) in enumerate(chains)]
        o_in = [intra(qs[i][1], v[i], cum[d]) for i, (d, sl) in enumerate(chains)]
        st = list(carry)
        for u in range(blocks_per_step):
            pair = range(2 * u, 2 * u + 2)
            outs, st = recur_pair(st, [(dk[i][2], upd[i], qs[i][0], o_in[i]) for i in pair])
            for i in pair:
                d, sl = chains[i]
                o_scr[d][sl, :] = outs[d]
        return tuple(st)

    lax.fori_loop(0, n_blk // blocks_per_step, lat_body, tuple(st))

    ng = ng_ref[...]

    def fin_body(r, carry):
        sl = pl.ds(pl.multiple_of(r * blk, blk), blk)
        o = of_scr[sl, :] + ob_scr[sl, :]
        y = o * lax.rsqrt(jnp.mean(o * o, axis=-1, keepdims=True) + RMS_EPS) * ng
        o_ref[0, sl, :] = (y * _silu(g_ref[0, sl, :].astype(F32))).astype(o_ref.dtype)
        return carry

    lax.fori_loop(0, n_blk, fin_body, 0, unroll=4)


def _hgrn_mixer(p_lat, p_ctx, ctx_tiles, hg_lb, norm_g, c_rows, w_mod, b_mod, mod_col0, cast_weights):
    bsz, t_len, _ = p_lat.shape
    l_len = p_ctx.shape[1]
    w = HG_DK
    d, n_mod = w_mod.shape
    n_steps = bsz * HG_HEADS
    mod_tn = (n_mod - mod_col0) // n_steps
    assert mod_tn % 128 == 0 and mod_col0 % mod_tn == 0 and mod_tn * n_steps == n_mod - mod_col0
    assert all(m.shape[0] % (n_steps * CAST_ROWS) == 0 for m in cast_weights)

    def slab(m):
        return pl.BlockSpec((m.shape[0] // n_steps, m.shape[1]), lambda b, h: (b * HG_HEADS + h, 0))

    def lat(group):
        return pl.BlockSpec((1, t_len, w), lambda b, h: (b, 0, group * HG_HEADS + h))

    def ctx(group):
        blk0 = _packed_col(ctx_tiles, group * HG_HEADS * w) // w
        return pl.BlockSpec((1, l_len, w), lambda b, h: (b, 0, blk0 + h))

    def const(shape):
        return pl.BlockSpec(shape, lambda b, h: (0,) * len(shape))

    lower, upper, same = _chunk_masks()
    diag4 = np.tile(same[:128, ::HG_CHUNK][:, :4, None], (1, 1, HG_DK)).reshape(128, 4 * HG_DK)
    seq_f32 = pltpu.VMEM((t_len, w), F32)
    return pl.pallas_call(
        functools.partial(_hgrn_kernel, n_cast=len(cast_weights)),
        grid=(bsz, HG_HEADS),
        in_specs=[
            lat(0), lat(1), lat(2), lat(3), lat(4),
            ctx(1), ctx(2), ctx(3),
            pl.BlockSpec((hg_lb.shape[0], 2, w), lambda b, h: (0, 0, h)),
            const((1, HG_DV)),
            const((HG_BLOCK, HG_BLOCK)), const((HG_BLOCK, HG_BLOCK)),
            const((128, 4 * HG_DK)),
            const((MOD_ROWS, d)),
            pl.BlockSpec((d, mod_tn), lambda b, h: (0, mod_col0 // mod_tn + b * HG_HEADS + h)),
            pl.BlockSpec((1, mod_tn), lambda b, h: (0, mod_col0 // mod_tn + b * HG_HEADS + h)),
        ] + [slab(m) for m in cast_weights],
        out_specs=[pl.BlockSpec((1, t_len, HG_DV), lambda b, h: (b, 0, h)),
                   pl.BlockSpec((MOD_ROWS, mod_tn), lambda b, h: (0, b * HG_HEADS + h))]
        + [slab(m) for m in cast_weights],
        out_shape=[jax.ShapeDtypeStruct((bsz, t_len, HG_WIDTH), BF16),
                   jax.ShapeDtypeStruct((MOD_ROWS, n_mod - mod_col0), F32)]
        + [jax.ShapeDtypeStruct(m.shape, BF16) for m in cast_weights],
        scratch_shapes=[seq_f32, seq_f32],
        compiler_params=_params("parallel", "parallel"),
        name="hgrn2_mixer",
    )(p_lat, p_lat, p_lat, p_lat, p_lat, p_ctx, p_ctx, p_ctx, hg_lb, norm_g.reshape(1, HG_DV),
      jnp.asarray(lower, BF16), jnp.asarray(upper, BF16), jnp.asarray(diag4, BF16),
      c_rows, w_mod, b_mod.reshape(1, n_mod), *cast_weights)


ATT_QROWS = 2 * ATT_BLOCK


def _attn_kernel(q_ref, kvp_ref, kvc_ref, kvn_ref, ckv_ref, sink_ref, o_ref):
    n = pl.program_id(1)
    n_steps = pl.num_programs(1)
    rows = q_ref.shape[1]
    blk = ATT_BLOCK
    hd = HEAD_DIM
    l_len = ckv_ref.shape[1]
    band = 3 * blk
    n_keys = band + l_len
    ones = jnp.ones((n_keys, hd), BF16)

    def band_rows(cols):
        return jnp.concatenate([kvp_ref[0, rows - blk:, cols], kvc_ref[0, :, cols], kvn_ref[0, :blk, cols]], axis=0)

    kv_slabs = [band_rows(slice(j * hd, (j + 1) * hd)) for j in range(2 * ATT_KV_HEADS)]
    ctx_slabs = [ckv_ref[0, :, j * hd:(j + 1) * hd] for j in range(2 * ATT_KV_HEADS)]
    qi = lax.broadcasted_iota(jnp.int32, (blk, n_keys), 0)
    kj = lax.broadcasted_iota(jnp.int32, (blk, n_keys), 1)
    in_window = jnp.abs(kj - blk - qi) <= WINDOW

    for sub in range(rows // blk):
        lo = sub * blk
        k_pos = n * rows + lo - blk + kj
        ok = (kj >= band) | (in_window & (k_pos >= 0) & (k_pos < n_steps * rows))
        bias = jnp.where(ok, 0.0, NEG_INF)
        k_all = [jnp.concatenate([kv_slabs[j][lo:lo + band], ctx_slabs[j]], axis=0) for j in range(ATT_KV_HEADS)]
        v_aug = [jnp.concatenate([jnp.concatenate([kv_slabs[ATT_KV_HEADS + j][lo:lo + band],
                                                   ctx_slabs[ATT_KV_HEADS + j]], axis=0), ones], axis=1)
                 for j in range(ATT_KV_HEADS)]
        heads = range(ATT_HEADS)
        s = [lax.dot_general(q_ref[0, lo:lo + blk, h * hd:(h + 1) * hd], k_all[h // ATT_GROUP], _NT,
                             preferred_element_type=F32) + bias for h in heads]
        sink = [sink_ref[:, h:h + 1] for h in heads]
        m = [jnp.maximum(jnp.max(s[h], axis=-1, keepdims=True), sink[h]) for h in heads]
        e = [jnp.exp(s[h] - m[h]).astype(BF16) for h in heads]
        oa = [_dot(e[h], v_aug[h // ATT_GROUP]) for h in heads]
        for h in heads:
            o = oa[h][:, :hd] / (oa[h][:, hd:] + jnp.exp(sink[h] - m[h]))
            o_ref[0, lo:lo + blk, h * hd:(h + 1) * hd] = o.astype(o_ref.dtype)


def _window_attention(p_lat, p_ctx, ctx_tiles, sink):
    bsz, t_len, _ = p_lat.shape
    l_len = p_ctx.shape[1]
    nb = t_len // ATT_QROWS
    kvw = 2 * ATT_KV_HEADS * HEAD_DIM
    q_blk = HG_COLS // ATT_WIDTH
    kv_blk = (HG_COLS + ATT_WIDTH) // kvw
    ckv_blk = _packed_col(ctx_tiles, HG_COLS + ATT_WIDTH) // kvw

    def prev(n):
        return jnp.maximum(n - 1, 0)

    def nxt(n):
        return jnp.minimum(n + 1, nb - 1)

    def kv_spec(shift):
        return pl.BlockSpec((1, ATT_QROWS, kvw), lambda b, n: (b, shift(n), kv_blk))

    same = lambda n: n
    return pl.pallas_call(
        _attn_kernel,
        grid=(bsz, nb),
        in_specs=[
            pl.BlockSpec((1, ATT_QROWS, ATT_WIDTH), lambda b, n: (b, n, q_blk)),
            kv_spec(prev), kv_spec(same), kv_spec(nxt),
            pl.BlockSpec((1, l_len, kvw), lambda b, n: (b, 0, ckv_blk)),
            pl.BlockSpec((1, ATT_HEADS), lambda b, n: (0, 0)),
        ],
        out_specs=pl.BlockSpec((1, ATT_QROWS, ATT_WIDTH), lambda b, n: (b, n, 0)),
        out_shape=jax.ShapeDtypeStruct((bsz, t_len, ATT_WIDTH), BF16),
        compiler_params=_params("parallel", "parallel"),
        name="window_gqa",
    )(p_lat, p_lat, p_lat, p_lat, p_ctx, sink.reshape(1, ATT_HEADS))


def _outproj_kernel(x_ref, mod_ref, hg_ref, at_ref, w_ref, o_ref):
    kh = hg_ref.shape[1]
    acc = _dot(hg_ref[...], w_ref[:kh, :]) + _dot(at_ref[...], w_ref[kh:, :])
    o_ref[...] = x_ref[...] + mod_ref[0, 2:3, :] * acc


def _out_projection(x2d, mod, hg2d, at2d, w_out, tokens_per_mod, tm=512):
    m, d = x2d.shape
    kh = hg2d.shape[1]
    tiles_per_mod = tokens_per_mod // tm
    return pl.pallas_call(
        _outproj_kernel,
        grid=(m // tm,),
        in_specs=[
            pl.BlockSpec((tm, d), lambda i: (i, 0)),
            pl.BlockSpec((1, N_MOD, d), lambda i: (i // tiles_per_mod, 0, 0)),
            pl.BlockSpec((tm, kh), lambda i: (i, 0)),
            pl.BlockSpec((tm, kh), lambda i: (i, 0)),
            pl.BlockSpec(w_out.shape, lambda i: (0, 0), pipeline_mode=pl.Buffered(1)),
        ],
        out_specs=pl.BlockSpec((tm, d), lambda i: (i, 0)),
        out_shape=jax.ShapeDtypeStruct((m, d), F32),
        compiler_params=_params("parallel"),
        name="out_proj",
    )(x2d, mod, hg2d, at2d, w_out)


def _ffn_kernel(x_ref, mod_ref, g_ref, wg_ref, wu_ref, wd_ref, o_ref, h_scr):
    j = pl.program_id(1)

    @pl.when(j == 0)
    def _():
        _norm_modulate_into(x_ref, g_ref, mod_ref[0, 3:4, :], mod_ref[0, 4:5, :], h_scr)

    def gated_down():
        h = h_scr[...]
        gate = _dot(h, wg_ref[...])
        up = _dot(h, wu_ref[...])
        act = (_silu(gate) * up).astype(BF16)
        return mod_ref[0, 5:6, :] * _dot(act, wd_ref[...])

    @pl.when(j == 0)
    def _():
        o_ref[...] = x_ref[...] + gated_down()

    @pl.when(j > 0)
    def _():
        o_ref[...] += gated_down()


def _ffn(x2d, mod, norm_g, w_gate_up, w_down, tokens_per_mod, tm=1024, tf=512):
    m, d = x2d.shape
    d_ff = w_down.shape[0]
    nf = d_ff // tf
    tiles_per_mod = tokens_per_mod // tm
    return pl.pallas_call(
        _ffn_kernel,
        grid=(m // tm, nf),
        in_specs=[
            pl.BlockSpec((tm, d), lambda i, j: (i, 0)),
            pl.BlockSpec((1, N_MOD, d), lambda i, j: (i // tiles_per_mod, 0, 0)),
            pl.BlockSpec((1, d), lambda i, j: (0, 0)),
            pl.BlockSpec((d, tf), lambda i, j: (0, j)),
            pl.BlockSpec((d, tf), lambda i, j: (0, nf + j)),
            pl.BlockSpec((tf, d), lambda i, j: (j, 0)),
        ],
        out_specs=pl.BlockSpec((tm, d), lambda i, j: (i, 0)),
        out_shape=jax.ShapeDtypeStruct((m, d), F32),
        scratch_shapes=[pltpu.VMEM((tm, d), BF16)],
        compiler_params=_params("parallel", "arbitrary"),
        name="swiglu_ffn",
    )(x2d, mod, norm_g.reshape(1, d), w_gate_up, w_gate_up, w_down)


def kernel(x, c, ctx, c_ctx, w_mod, b_mod, norm_mix_g, norm_ffn_g, w_in, hg_lb, hg_norm_g,
           q_norm_g, k_norm_g, attn_sink, w_out, w_gate_up, w_down):
    bsz, t_len, d = x.shape
    l_len = ctx.shape[1]
    depth = w_mod.shape[0]
    assert depth == 1, "single-layer block: the context stream never needs its own outputs"
    assert bsz + 1 <= MOD_ROWS

    c_rows = jnp.concatenate([c, c_ctx[None, :], jnp.zeros((MOD_ROWS - bsz - 1, d), F32)], axis=0)
    n_early = 2 * d
    mod_early = _modulation(c_rows, w_mod[0], b_mod[0], n_early)
    mod_mix = mod_early.reshape(MOD_ROWS, 2, d)

    cos, sin = _rope_tables(t_len)
    lat_tiles = _tiles_covering([(0, IN_COLS)])
    hk = HG_HEADS * HG_DK
    ctx_tiles = _tiles_covering([(hk, 3 * hk + HG_WIDTH), (HG_COLS + ATT_WIDTH, IN_COLS)])
    p_lat = _in_projection(x.reshape(bsz * t_len, d), mod_mix[:bsz], norm_mix_g[0], w_in[0], q_norm_g[0],
                           k_norm_g[0], cos, sin, tokens_per_mod=t_len, tm=1024, rope=True, col_tiles=lat_tiles)
    p_ctx = _in_projection(ctx.reshape(bsz * l_len, d), mod_mix[bsz:bsz + 1], norm_mix_g[0], w_in[0], q_norm_g[0],
                           k_norm_g[0], cos, sin, tokens_per_mod=bsz * l_len, tm=bsz * l_len, rope=False,
                           col_tiles=ctx_tiles)
    p_lat = p_lat.reshape(bsz, t_len, -1)
    p_ctx = p_ctx.reshape(bsz, l_len, -1)

    hg, mod_late, w_out16, w_gate_up16, w_down16 = _hgrn_mixer(
        p_lat, p_ctx, ctx_tiles, hg_lb, hg_norm_g[0], c_rows, w_mod[0], b_mod[0], n_early,
        cast_weights=(w_out[0], w_gate_up[0], w_down[0]))
    mod_lat = jnp.concatenate([mod_early, mod_late], axis=1).reshape(MOD_ROWS, N_MOD, d)[:bsz]
    at = _window_attention(p_lat, p_ctx, ctx_tiles, attn_sink[0])

    x1 = _out_projection(x.reshape(bsz * t_len, d), mod_lat, hg.reshape(bsz * t_len, HG_WIDTH),
                         at.reshape(bsz * t_len, ATT_WIDTH), w_out16, tokens_per_mod=t_len)
    x2 = _ffn(x1, mod_lat, norm_ffn_g[0], w_gate_up16, w_down16, tokens_per_mod=t_len)
    return x2.reshape(bsz, t_len, d)
```

```python
import functools

import jax
import jax.numpy as jnp
import numpy as np
from jax import lax
from jax.experimental import pallas as pl
from jax.experimental.pallas import tpu as pltpu

F32 = jnp.float32
BF16 = jnp.bfloat16

GRID_W = 64
HG_HEADS = 8
HG_DK = 128
HG_DV = 128
HG_CHUNK = 32
ATT_HEADS = 8
ATT_KV_HEADS = 2
ATT_GROUP = ATT_HEADS // ATT_KV_HEADS
HEAD_DIM = 128
WINDOW = 128
ATT_BLOCK = 128
ROPE_THETA = 10000.0
N_MOD = 6
RMS_EPS = 1e-6
NEG_INF = -1e30
HG_WIDTH = HG_HEADS * HG_DV
ATT_WIDTH = ATT_HEADS * HEAD_DIM
HG_COLS = 3 * HG_HEADS * HG_DK + 2 * HG_HEADS * HG_DV
ATT_COLS = (ATT_HEADS + 2 * ATT_KV_HEADS) * HEAD_DIM
IN_COLS = HG_COLS + ATT_COLS

VMEM_LIMIT_BYTES = 58 * 1024 * 1024
MOD_ROWS = 8

_NT = (((1,), (1,)), ((), ()))
_TN = (((0,), (0,)), ((), ()))


def _dot(a, b):
    return jnp.dot(a, b, preferred_element_type=F32)


def _silu(x):
    h = 0.5 * x
    return h * jnp.tanh(h) + h


def _params(*sem):
    return pltpu.CompilerParams(dimension_semantics=sem, vmem_limit_bytes=VMEM_LIMIT_BYTES)


def _mod_kernel(c_ref, w_ref, b_ref, o_ref):
    a = _silu(c_ref[...]).astype(BF16)
    o_ref[...] = _dot(a, w_ref[...].astype(BF16)) + b_ref[...]


def _modulation(c_rows, w_mod, b_mod, n_cols, tn=1024):
    d, n = w_mod.shape
    return pl.pallas_call(
        _mod_kernel,
        grid=(n_cols // tn,),
        in_specs=[
            pl.BlockSpec((MOD_ROWS, d), lambda j: (0, 0)),
            pl.BlockSpec((d, tn), lambda j: (0, j)),
            pl.BlockSpec((1, tn), lambda j: (0, j)),
        ],
        out_specs=pl.BlockSpec((MOD_ROWS, tn), lambda j: (0, j)),
        out_shape=jax.ShapeDtypeStruct((MOD_ROWS, n_cols), F32),
        compiler_params=_params("arbitrary"),
        name="adaln_mod",
    )(c_rows, w_mod, b_mod.reshape(1, n))


def _rope_tables(t_len):
    half = HEAD_DIM // 2
    nf = half // 2
    pos = np.arange(t_len)
    inv_freq = ROPE_THETA ** (-np.arange(nf, dtype=np.float64) / nf)
    ang_r = (pos // GRID_W)[:, None] * inv_freq
    ang_c = (pos % GRID_W)[:, None] * inv_freq
    cos = np.concatenate([np.cos(ang_r)] * 2 + [np.cos(ang_c)] * 2, axis=-1)
    sin = np.concatenate([-np.sin(ang_r), np.sin(ang_r), -np.sin(ang_c), np.sin(ang_c)], axis=-1)
    pair = lambda a: jnp.asarray(np.tile(a, (1, HEAD_PAIR // HEAD_DIM)), F32)
    return pair(cos), pair(sin)


HEAD_PAIR = 2 * HEAD_DIM


def _pair_matrices():
    quarter = HEAD_DIM // 4
    lane = np.arange(HEAD_PAIR)
    src = np.where(lane % (2 * quarter) < quarter, lane + quarter, lane - quarter)
    swap = lane[:, None] == src[None, :]
    head_mean = ((lane[:, None] // HEAD_DIM) == (lane[None, :] // HEAD_DIM)) / HEAD_DIM
    return jnp.asarray(swap, BF16), jnp.asarray(head_mean, BF16)


def _norm_modulate_into(x_ref, g_ref, shift, scale, h_scr, rows=128):
    tm = x_ref.shape[0]
    gain = g_ref[...] * (1.0 + scale)

    def body(r, carry):
        sl = pl.ds(pl.multiple_of(r * rows, rows), rows)
        x = x_ref[sl, :]
        ms = jnp.mean(x * x, axis=-1, keepdims=True)
        h_scr[sl, :] = (x * lax.rsqrt(ms + RMS_EPS) * gain + shift).astype(BF16)
        return carry

    lax.fori_loop(0, tm // rows, body, 0)


def _inproj_kernel(tiles_ref, x_ref, mod_ref, g_ref, wa_ref, wb_ref, cos_ref, sin_ref, swap_ref, hmean_ref, qg_ref,
                   kg_ref, o_ref, h_scr, *, rope):
    j = pl.program_id(1)
    tn = wa_ref.shape[1]
    hd = HEAD_DIM
    q_tile0 = HG_COLS // tn
    kv_tile = (HG_COLS + ATT_WIDTH) // tn

    @pl.when(j == 0)
    def _():
        _norm_modulate_into(x_ref, g_ref, mod_ref[0, 0:1, :], mod_ref[0, 1:2, :], h_scr)

    def qk_prep(acc, col0, n_heads, g_row, scale):
        cols = [slice(p * HEAD_PAIR, (p + 1) * HEAD_PAIR) for p in range(n_heads * hd // HEAD_PAIR)]
        x = [acc[:, c] for c in cols]
        gain = g_row * scale
        ms = [_dot((xh * xh).astype(BF16), hmean_ref[...]) for xh in x]
        y = [xh * lax.rsqrt(m + RMS_EPS) * gain for xh, m in zip(x, ms)]
        if rope:
            swapped = [_dot(yh.astype(BF16), swap_ref[...]) for yh in y]
            y = [yh * cos_ref[...] + sh * sin_ref[...] for yh, sh in zip(y, swapped)]
        for c, yh in zip(cols, y):
            o_ref[:, col0 + c.start:col0 + c.stop] = yh.astype(o_ref.dtype)

    for slot, w_ref in enumerate((wa_ref, wb_ref)):
        t = tiles_ref[2 * j + slot]
        col0 = slot * tn

        def project(w_ref=w_ref):
            return _dot(h_scr[...], w_ref[...].astype(BF16))

        @pl.when((t >= 0) & (t < q_tile0))
        def _(col0=col0, project=project):
            o_ref[:, col0:col0 + tn] = project().astype(o_ref.dtype)

        @pl.when((t >= q_tile0) & (t < kv_tile))
        def _(col0=col0, project=project):
            qk_prep(project(), col0, tn // hd, qg_ref[...], hd ** -0.5)

        @pl.when(t == kv_tile)
        def _(col0=col0, project=project):
            acc = project()
            qk_prep(acc, col0, ATT_KV_HEADS, kg_ref[...], 1.0)
            v0 = ATT_KV_HEADS * hd
            o_ref[:, col0 + v0:col0 + tn] = acc[:, v0:].astype(o_ref.dtype)


IN_TILE = 512


def _tiles_covering(col_ranges):
    return tuple(t for t in range(IN_COLS // IN_TILE)
                 if any(lo < (t + 1) * IN_TILE and t * IN_TILE < hi for lo, hi in col_ranges))


def _packed_col(col_tiles, col):
    return col_tiles.index(col // IN_TILE) * IN_TILE + col % IN_TILE


def _in_projection(x2d, mod, norm_g, w_in, q_norm_g, k_norm_g, cos, sin, tokens_per_mod, tm, rope, col_tiles):
    m, d = x2d.shape
    tn = IN_TILE
    assert HG_COLS % tn == 0 and ATT_WIDTH % tn == 0 and 2 * ATT_KV_HEADS * HEAD_DIM == tn
    tiles_per_mod = tokens_per_mod // tm
    tiles_per_seq = cos.shape[0] // tm
    vec = pl.BlockSpec((1, HEAD_PAIR), lambda i, j, tiles: (0, 0))
    tab = pl.BlockSpec((tm, HEAD_PAIR), lambda i, j, tiles: (i % tiles_per_seq, 0))
    mat = pl.BlockSpec((HEAD_PAIR, HEAD_PAIR), lambda i, j, tiles: (0, 0))
    per_pair = lambda g: jnp.tile(g.reshape(1, HEAD_DIM), (1, HEAD_PAIR // HEAD_DIM))
    n_steps = pl.cdiv(len(col_tiles), 2)
    slots = np.full((2 * n_steps,), -1, np.int32)
    slots[:len(col_tiles)] = col_tiles
    grid_spec = pltpu.PrefetchScalarGridSpec(
        num_scalar_prefetch=1,
        grid=(m // tm, n_steps),
        in_specs=[
            pl.BlockSpec((tm, d), lambda i, j, tiles: (i, 0)),
            pl.BlockSpec((1, mod.shape[1], d), lambda i, j, tiles: (i // tiles_per_mod, 0, 0)),
            pl.BlockSpec((1, d), lambda i, j, tiles: (0, 0)),
            pl.BlockSpec((d, tn), lambda i, j, tiles: (0, tiles[2 * j])),
            pl.BlockSpec((d, tn), lambda i, j, tiles: (0, jnp.maximum(tiles[2 * j + 1], 0))),
            tab, tab, mat, mat, vec, vec,
        ],
        out_specs=pl.BlockSpec((tm, 2 * tn), lambda i, j, tiles: (i, j)),
        scratch_shapes=[pltpu.VMEM((tm, d), BF16)],
    )
    return pl.pallas_call(
        functools.partial(_inproj_kernel, rope=rope),
        grid_spec=grid_spec,
        out_shape=jax.ShapeDtypeStruct((m, len(col_tiles) * tn), BF16),
        compiler_params=_params("parallel", "arbitrary"),
        name="in_proj",
    )(jnp.asarray(slots), x2d, mod, norm_g.reshape(1, d), w_in, w_in, cos, sin,
      *_pair_matrices(), per_pair(q_norm_g), per_pair(k_norm_g))


HG_BLOCK = 256
CAST_ROWS = 16


def _cast_rows(src_ref, dst_ref):
    def body(r, carry):
        sl = pl.ds(pl.multiple_of(r * CAST_ROWS, CAST_ROWS), CAST_ROWS)
        dst_ref[sl, :] = src_ref[sl, :].astype(dst_ref.dtype)
        return carry

    lax.fori_loop(0, src_ref.shape[0] // CAST_ROWS, body, 0)


def _chunk_masks():
    r = np.arange(HG_BLOCK)[:, None]
    c = np.arange(HG_BLOCK)[None, :]
    same = (r // HG_CHUNK) == (c // HG_CHUNK)
    return (same & (r >= c)), (same & (r <= c)), same


def _hgrn_kernel(q_ref, ff_ref, fb_ref, i_ref, g_ref, cff_ref, cfb_ref, ci_ref, lb_ref, ng_ref,
                 cumf_ref, cumb_ref, dmask_ref, c_ref, wm_ref, bm_ref, *rest, n_cast):
    cast_src = rest[:n_cast]
    o_ref, mod_ref = rest[n_cast:n_cast + 2]
    cast_dst = rest[n_cast + 2:2 * n_cast + 2]
    of_scr, ob_scr = rest[2 * n_cast + 2:]

    mod_ref[...] = _dot(_silu(c_ref[...]).astype(BF16), wm_ref[...].astype(BF16)) + bm_ref[...]
    for src, dst in zip(cast_src, cast_dst):
        _cast_rows(src, dst)

    t_len = q_ref.shape[1]
    l_len = ci_ref.shape[1]
    c = HG_CHUNK
    blk = HG_BLOCK
    nch = blk // c
    n_blk = t_len // blk
    assert l_len == blk

    a = lb_ref[...]
    e = jnp.exp(a - jnp.max(a, axis=0, keepdims=True))
    lb = e[0] / jnp.sum(e, axis=0)

    def gates(f_pre, lb_d):
        f = 0.5 * (1.0 + lb_d) + (0.5 * (1.0 - lb_d)) * jnp.tanh(0.5 * f_pre.astype(F32))
        log_f = jnp.log2(f)
        hi = lax.bitcast_convert_type(lax.bitcast_convert_type(log_f, jnp.uint32) & jnp.uint32(0xFFFF0000), F32)
        return 1.0 - f, jnp.concatenate([hi.astype(BF16), (log_f - hi).astype(BF16)], axis=1)

    def decays(parts, cum_ref, reverse):
        b = _dot(cum_ref[...], parts)
        b = b[:, 0:128] + b[:, 128:256]
        b3 = b.reshape(nch, c, HG_DK)
        tot = b3[:, 0:1, :] if reverse else b3[:, c - 1:c, :]
        b_tot = jnp.broadcast_to(tot, (nch, c, HG_DK)).reshape(blk, HG_DK)
        return b, b_tot, jnp.exp2(tot)

    def increments(key, b, b_tot, v):
        k_dec = (key * jnp.exp2(b_tot - b)).astype(BF16)
        v_t = v.astype(BF16).T
        upd = []
        for h in range(blk // 128):
            kd = k_dec[h * 128:(h + 1) * 128, :]
            kd4 = jnp.concatenate([kd] * 4, axis=1) * dmask_ref[...]
            u4 = _dot(v_t[:, h * 128:(h + 1) * 128], kd4)
            upd += [u4[:, j * HG_DK:(j + 1) * HG_DK] for j in range(4)]
        return upd

    def scores(q_pre, key, b):
        q_dec = (_silu(q_pre.astype(F32)) * jnp.exp2(b)).astype(BF16)
        k_inv = (key * jnp.exp2(-b)).astype(BF16)
        return q_dec, lax.dot_general(q_dec, k_inv, _NT, preferred_element_type=F32)

    def intra(s, v, cum_ref):
        return _dot(s.astype(BF16) * cum_ref[...], v.astype(BF16))

    def recur_pair(st, chains):
        outs = [[None] * nch for _ in chains]
        st = list(st)
        for step in range(nch):
            for d, (dec, upd, q_dec, o_intra) in enumerate(chains):
                n = nch - 1 - step if d == 1 else step
                if q_dec is not None:
                    rows = slice(n * c, (n + 1) * c)
                    outs[d][n] = o_intra[rows, :] + _dot(q_dec[rows, :], st[d].T.astype(BF16))
                st[d] = st[d] * dec[n] + upd[n]
        return [jnp.concatenate(o, axis=0) if o[0] is not None else None for o in outs], st

    lb_d = (lb[0:1, :], lb[1:2, :])
    cum = (cumf_ref, cumb_ref)
    f_lat = (ff_ref, fb_ref)
    o_scr = (of_scr, ob_scr)

    cv = ci_ref[0]
    ctx_chains = []
    for d, f_ref in enumerate((cff_ref, cfb_ref)):
        key, parts = gates(f_ref[0], lb_d[d])
        b, b_tot, dec = decays(parts, cum[d], d == 1)
        ctx_chains.append((dec, increments(key, b, b_tot, cv), None, None))
    _, st = recur_pair([jnp.zeros((HG_DV, HG_DK), F32)] * 2, ctx_chains)

    blocks_per_step = 2

    def lat_body(r, carry):
        chains = []
        for u in range(blocks_per_step):
            up = r * blocks_per_step + u
            chains.append((0, pl.ds(pl.multiple_of(up * blk, blk), blk)))
            chains.append((1, pl.ds(pl.multiple_of((n_blk - 1 - up) * blk, blk), blk)))
        g = [gates(f_lat[d][0, sl, :], lb_d[d]) for d, sl in chains]
        dk = [decays(g[i][1], cum[d], d == 1) for i, (d, sl) in enumerate(chains)]
        v = [i_ref[0, sl, :] for d, sl in chains]
        upd = [increments(g[i][0], dk[i][0], dk[i][1], v[i]) for i in range(len(chains))]
        qs = [scores(q_ref[0, sl, :], g[i][0], dk[i][0]) for i, (d, sl) in enumerate(chains)]
        o_in = [intra(qs[i][1], v[i], cum[d]) for i, (d, sl) in enumerate(chains)]
        st = list(carry)
        for u in range(blocks_per_step):
            pair = range(2 * u, 2 * u + 2)
            outs, st = recur_pair(st, [(dk[i][2], upd[i], qs[i][0], o_in[i]) for i in pair])
            for i in pair:
                d, sl = chains[i]
                o_scr[d][sl, :] = outs[d]
        return tuple(st)

    lax.fori_loop(0, n_blk // blocks_per_step, lat_body, tuple(st))

    ng = ng_ref[...]

    def fin_body(r, carry):
        sl = pl.ds(pl.multiple_of(r * blk, blk), blk)
        o = of_scr[sl, :] + ob_scr[sl, :]
        y = o * lax.rsqrt(jnp.mean(o * o, axis=-1, keepdims=True) + RMS_EPS) * ng
        o_ref[0, sl, :] = (y * _silu(g_ref[0, sl, :].astype(F32))).astype(o_ref.dtype)
        return carry

    lax.fori_loop(0, n_blk, fin_body, 0, unroll=4)


def _hgrn_mixer(p_lat, p_ctx, ctx_tiles, hg_lb, norm_g, c_rows, w_mod, b_mod, mod_col0, cast_weights):
    bsz, t_len, _ = p_lat.shape
    l_len = p_ctx.shape[1]
    w = HG_DK
    d, n_mod = w_mod.shape
    n_steps = bsz * HG_HEADS
    mod_tn = (n_mod - mod_col0) // n_steps
    assert mod_tn % 128 == 0 and mod_col0 % mod_tn == 0 and mod_tn * n_steps == n_mod - mod_col0
    assert all(m.shape[0] % (n_steps * CAST_ROWS) == 0 for m in cast_weights)

    def slab(m):
        return pl.BlockSpec((m.shape[0] // n_steps, m.shape[1]), lambda b, h: (b * HG_HEADS + h, 0))

    def lat(group):
        return pl.BlockSpec((1, t_len, w), lambda b, h: (b, 0, group * HG_HEADS + h))

    def ctx(group):
        blk0 = _packed_col(ctx_tiles, group * HG_HEADS * w) // w
        return pl.BlockSpec((1, l_len, w), lambda b, h: (b, 0, blk0 + h))

    def const(shape):
        return pl.BlockSpec(shape, lambda b, h: (0,) * len(shape))

    lower, upper, same = _chunk_masks()
    diag4 = np.tile(same[:128, ::HG_CHUNK][:, :4, None], (1, 1, HG_DK)).reshape(128, 4 * HG_DK)
    seq_f32 = pltpu.VMEM((t_len, w), F32)
    return pl.pallas_call(
        functools.partial(_hgrn_kernel, n_cast=len(cast_weights)),
        grid=(bsz, HG_HEADS),
        in_specs=[
            lat(0), lat(1), lat(2), lat(3), lat(4),
            ctx(1), ctx(2), ctx(3),
            pl.BlockSpec((hg_lb.shape[0], 2, w), lambda b, h: (0, 0, h)),
            const((1, HG_DV)),
            const((HG_BLOCK, HG_BLOCK)), const((HG_BLOCK, HG_BLOCK)),
            const((128, 4 * HG_DK)),
            const((MOD_ROWS, d)),
            pl.BlockSpec((d, mod_tn), lambda b, h: (0, mod_col0 // mod_tn + b * HG_HEADS + h)),
            pl.BlockSpec((1, mod_tn), lambda b, h: (0, mod_col0 // mod_tn + b * HG_HEADS + h)),
        ] + [slab(m) for m in cast_weights],
        out_specs=[pl.BlockSpec((1, t_len, HG_DV), lambda b, h: (b, 0, h)),
                   pl.BlockSpec((MOD_ROWS, mod_tn), lambda b, h: (0, b * HG_HEADS + h))]
        + [slab(m) for m in cast_weights],
        out_shape=[jax.ShapeDtypeStruct((bsz, t_len, HG_WIDTH), BF16),
                   jax.ShapeDtypeStruct((MOD_ROWS, n_mod - mod_col0), F32)]
        + [jax.ShapeDtypeStruct(m.shape, BF16) for m in cast_weights],
        scratch_shapes=[seq_f32, seq_f32],
        compiler_params=_params("parallel", "parallel"),
        name="hgrn2_mixer",
    )(p_lat, p_lat, p_lat, p_lat, p_lat, p_ctx, p_ctx, p_ctx, hg_lb, norm_g.reshape(1, HG_DV),
      jnp.asarray(lower, BF16), jnp.asarray(upper, BF16), jnp.asarray(diag4, BF16),
      c_rows, w_mod, b_mod.reshape(1, n_mod), *cast_weights)


ATT_QROWS = 8 * ATT_BLOCK


def _attn_kernel(q_ref, kvp_ref, kvc_ref, kvn_ref, ckv_ref, sink_ref, o_ref):
    n = pl.program_id(1)
    n_steps = pl.num_programs(1)
    rows = q_ref.shape[1]
    blk = ATT_BLOCK
    hd = HEAD_DIM
    l_len = ckv_ref.shape[1]
    band = 3 * blk
    n_keys = band + l_len
    ones = jnp.ones((n_keys, hd), BF16)

    def band_rows(cols):
        return jnp.concatenate([kvp_ref[0, rows - blk:, cols], kvc_ref[0, :, cols], kvn_ref[0, :blk, cols]], axis=0)

    kv_slabs = [band_rows(slice(j * hd, (j + 1) * hd)) for j in range(2 * ATT_KV_HEADS)]
    ctx_slabs = [ckv_ref[0, :, j * hd:(j + 1) * hd] for j in range(2 * ATT_KV_HEADS)]
    qi = lax.broadcasted_iota(jnp.int32, (blk, n_keys), 0)
    kj = lax.broadcasted_iota(jnp.int32, (blk, n_keys), 1)
    in_window = jnp.abs(kj - blk - qi) <= WINDOW

    for sub in range(rows // blk):
        lo = sub * blk
        k_pos = n * rows + lo - blk + kj
        ok = (kj >= band) | (in_window & (k_pos >= 0) & (k_pos < n_steps * rows))
        bias = jnp.where(ok, 0.0, NEG_INF)
        k_all = [jnp.concatenate([kv_slabs[j][lo:lo + band], ctx_slabs[j]], axis=0) for j in range(ATT_KV_HEADS)]
        v_aug = [jnp.concatenate([jnp.concatenate([kv_slabs[ATT_KV_HEADS + j][lo:lo + band],
                                                   ctx_slabs[ATT_KV_HEADS + j]], axis=0), ones], axis=1)
                 for j in range(ATT_KV_HEADS)]
        heads = range(ATT_HEADS)
        s = [lax.dot_general(q_ref[0, lo:lo + blk, h * hd:(h + 1) * hd], k_all[h // ATT_GROUP], _NT,
                             preferred_element_type=F32) + bias for h in heads]
        sink = [sink_ref[:, h:h + 1] for h in heads]
        m = [jnp.maximum(jnp.max(s[h], axis=-1, keepdims=True), sink[h]) for h in heads]
        e = [jnp.exp(s[h] - m[h]).astype(BF16) for h in heads]
        oa = [_dot(e[h], v_aug[h // ATT_GROUP]) for h in heads]
        for h in heads:
            o = oa[h][:, :hd] / (oa[h][:, hd:] + jnp.exp(sink[h] - m[h]))
            o_ref[0, lo:lo + blk, h * hd:(h + 1) * hd] = o.astype(o_ref.dtype)


def _window_attention(p_lat, p_ctx, ctx_tiles, sink):
    bsz, t_len, _ = p_lat.shape
    l_len = p_ctx.shape[1]
    nb = t_len // ATT_QROWS
    kvw = 2 * ATT_KV_HEADS * HEAD_DIM
    q_blk = HG_COLS // ATT_WIDTH
    kv_blk = (HG_COLS + ATT_WIDTH) // kvw
    ckv_blk = _packed_col(ctx_tiles, HG_COLS + ATT_WIDTH) // kvw

    def prev(n):
        return jnp.maximum(n - 1, 0)

    def nxt(n):
        return jnp.minimum(n + 1, nb - 1)

    def kv_spec(shift):
        return pl.BlockSpec((1, ATT_QROWS, kvw), lambda b, n: (b, shift(n), kv_blk))

    same = lambda n: n
    return pl.pallas_call(
        _attn_kernel,
        grid=(bsz, nb),
        in_specs=[
            pl.BlockSpec((1, ATT_QROWS, ATT_WIDTH), lambda b, n: (b, n, q_blk)),
            kv_spec(prev), kv_spec(same), kv_spec(nxt),
            pl.BlockSpec((1, l_len, kvw), lambda b, n: (b, 0, ckv_blk)),
            pl.BlockSpec((1, ATT_HEADS), lambda b, n: (0, 0)),
        ],
        out_specs=pl.BlockSpec((1, ATT_QROWS, ATT_WIDTH), lambda b, n: (b, n, 0)),
        out_shape=jax.ShapeDtypeStruct((bsz, t_len, ATT_WIDTH), BF16),
        compiler_params=_params("parallel", "parallel"),
        name="window_gqa",
    )(p_lat, p_lat, p_lat, p_lat, p_ctx, sink.reshape(1, ATT_HEADS))


def _outproj_kernel(x_ref, mod_ref, hg_ref, at_ref, w_ref, o_ref):
    kh = hg_ref.shape[1]
    acc = _dot(hg_ref[...], w_ref[:kh, :]) + _dot(at_ref[...], w_ref[kh:, :])
    o_ref[...] = x_ref[...] + mod_ref[0, 2:3, :] * acc


def _out_projection(x2d, mod, hg2d, at2d, w_out, tokens_per_mod, tm=512):
    m, d = x2d.shape
    kh = hg2d.shape[1]
    tiles_per_mod = tokens_per_mod // tm
    return pl.pallas_call(
        _outproj_kernel,
        grid=(m // tm,),
        in_specs=[
            pl.BlockSpec((tm, d), lambda i: (i, 0)),
            pl.BlockSpec((1, N_MOD, d), lambda i: (i // tiles_per_mod, 0, 0)),
            pl.BlockSpec((tm, kh), lambda i: (i, 0)),
            pl.BlockSpec((tm, kh), lambda i: (i, 0)),
            pl.BlockSpec(w_out.shape, lambda i: (0, 0), pipeline_mode=pl.Buffered(1)),
        ],
        out_specs=pl.BlockSpec((tm, d), lambda i: (i, 0)),
        out_shape=jax.ShapeDtypeStruct((m, d), F32),
        compiler_params=_params("parallel"),
        name="out_proj",
    )(x2d, mod, hg2d, at2d, w_out)


def _ffn_kernel(x_ref, mod_ref, g_ref, wg_ref, wu_ref, wd_ref, o_ref, h_scr):
    j = pl.program_id(1)

    @pl.when(j == 0)
    def _():
        _norm_modulate_into(x_ref, g_ref, mod_ref[0, 3:4, :], mod_ref[0, 4:5, :], h_scr)

    def gated_down():
        h = h_scr[...]
        gate = _dot(h, wg_ref[...])
        up = _dot(h, wu_ref[...])
        act = (_silu(gate) * up).astype(BF16)
        return mod_ref[0, 5:6, :] * _dot(act, wd_ref[...])

    @pl.when(j == 0)
    def _():
        o_ref[...] = x_ref[...] + gated_down()

    @pl.when(j > 0)
    def _():
        o_ref[...] += gated_down()


def _ffn(x2d, mod, norm_g, w_gate_up, w_down, tokens_per_mod, tm=1024, tf=512):
    m, d = x2d.shape
    d_ff = w_down.shape[0]
    nf = d_ff // tf
    tiles_per_mod = tokens_per_mod // tm
    return pl.pallas_call(
        _ffn_kernel,
        grid=(m // tm, nf),
        in_specs=[
            pl.BlockSpec((tm, d), lambda i, j: (i, 0)),
            pl.BlockSpec((1, N_MOD, d), lambda i, j: (i // tiles_per_mod, 0, 0)),
            pl.BlockSpec((1, d), lambda i, j: (0, 0)),
            pl.BlockSpec((d, tf), lambda i, j: (0, j)),
            pl.BlockSpec((d, tf), lambda i, j: (0, nf + j)),
            pl.BlockSpec((tf, d), lambda i, j: (j, 0)),
        ],
        out_specs=pl.BlockSpec((tm, d), lambda i, j: (i, 0)),
        out_shape=jax.ShapeDtypeStruct((m, d), F32),
        scratch_shapes=[pltpu.VMEM((tm, d), BF16)],
        compiler_params=_params("parallel", "arbitrary"),
        name="swiglu_ffn",
    )(x2d, mod, norm_g.reshape(1, d), w_gate_up, w_gate_up, w_down)


def kernel(x, c, ctx, c_ctx, w_mod, b_mod, norm_mix_g, norm_ffn_g, w_in, hg_lb, hg_norm_g,
           q_norm_g, k_norm_g, attn_sink, w_out, w_gate_up, w_down):
    bsz, t_len, d = x.shape
    l_len = ctx.shape[1]
    depth = w_mod.shape[0]
    assert depth == 1, "single-layer block: the context stream never needs its own outputs"
    assert bsz + 1 <= MOD_ROWS

    c_rows = jnp.concatenate([c, c_ctx[None, :], jnp.zeros((MOD_ROWS - bsz - 1, d), F32)], axis=0)
    n_early = 2 * d
    mod_early = _modulation(c_rows, w_mod[0], b_mod[0], n_early)
    mod_mix = mod_early.reshape(MOD_ROWS, 2, d)

    cos, sin = _rope_tables(t_len)
    lat_tiles = _tiles_covering([(0, IN_COLS)])
    hk = HG_HEADS * HG_DK
    ctx_tiles = _tiles_covering([(hk, 3 * hk + HG_WIDTH), (HG_COLS + ATT_WIDTH, IN_COLS)])
    p_lat = _in_projection(x.reshape(bsz * t_len, d), mod_mix[:bsz], norm_mix_g[0], w_in[0], q_norm_g[0],
                           k_norm_g[0], cos, sin, tokens_per_mod=t_len, tm=1024, rope=True, col_tiles=lat_tiles)
    p_ctx = _in_projection(ctx.reshape(bsz * l_len, d), mod_mix[bsz:bsz + 1], norm_mix_g[0], w_in[0], q_norm_g[0],
                           k_norm_g[0], cos, sin, tokens_per_mod=bsz * l_len, tm=bsz * l_len, rope=False,
                           col_tiles=ctx_tiles)
    p_lat = p_lat.reshape(bsz, t_len, -1)
    p_ctx = p_ctx.reshape(bsz, l_len, -1)

    hg, mod_late, w_out16, w_gate_up16, w_down16 = _hgrn_mixer(
        p_lat, p_ctx, ctx_tiles, hg_lb, hg_norm_g[0], c_rows, w_mod[0], b_mod[0], n_early,
        cast_weights=(w_out[0], w_gate_up[0], w_down[0]))
    mod_lat = jnp.concatenate([mod_early, mod_late], axis=1).reshape(MOD_ROWS, N_MOD, d)[:bsz]
    at = _window_attention(p_lat, p_ctx, ctx_tiles, attn_sink[0])

    x1 = _out_projection(x.reshape(bsz * t_len, d), mod_lat, hg.reshape(bsz * t_len, HG_WIDTH),
                         at.reshape(bsz * t_len, ATT_WIDTH), w_out16, tokens_per_mod=t_len)
    x2 = _ffn(x1, mod_lat, norm_ffn_g[0], w_gate_up16, w_down16, tokens_per_mod=t_len)
    return x2.reshape(bsz, t_len, d)
```

```python
import functools

import jax
import jax.numpy as jnp
import numpy as np
from jax import lax
from jax.experimental import pallas as pl
from jax.experimental.pallas import tpu as pltpu

F32 = jnp.float32
BF16 = jnp.bfloat16

GRID_W = 64
HG_HEADS = 8
HG_DK = 128
HG_DV = 128
HG_CHUNK = 32
ATT_HEADS = 8
ATT_KV_HEADS = 2
ATT_GROUP = ATT_HEADS // ATT_KV_HEADS
HEAD_DIM = 128
WINDOW = 128
ATT_BLOCK = 128
ROPE_THETA = 10000.0
N_MOD = 6
RMS_EPS = 1e-6
NEG_INF = -1e30
HG_WIDTH = HG_HEADS * HG_DV
ATT_WIDTH = ATT_HEADS * HEAD_DIM
HG_COLS = 3 * HG_HEADS * HG_DK + 2 * HG_HEADS * HG_DV
ATT_COLS = (ATT_HEADS + 2 * ATT_KV_HEADS) * HEAD_DIM
IN_COLS = HG_COLS + ATT_COLS

VMEM_LIMIT_BYTES = 58 * 1024 * 1024
MOD_ROWS = 8

_NT = (((1,), (1,)), ((), ()))
_TN = (((0,), (0,)), ((), ()))


def _dot(a, b):
    return jnp.dot(a, b, preferred_element_type=F32)


def _silu(x):
    h = 0.5 * x
    return h * jnp.tanh(h) + h


def _params(*sem):
    return pltpu.CompilerParams(dimension_semantics=sem, vmem_limit_bytes=VMEM_LIMIT_BYTES)


def _mod_kernel(c_ref, w_ref, b_ref, o_ref):
    a = _silu(c_ref[...]).astype(BF16)
    o_ref[...] = _dot(a, w_ref[...].astype(BF16)) + b_ref[...]


def _modulation(c_rows, w_mod, b_mod, n_cols, tn=1024):
    d, n = w_mod.shape
    return pl.pallas_call(
        _mod_kernel,
        grid=(n_cols // tn,),
        in_specs=[
            pl.BlockSpec((MOD_ROWS, d), lambda j: (0, 0)),
            pl.BlockSpec((d, tn), lambda j: (0, j)),
            pl.BlockSpec((1, tn), lambda j: (0, j)),
        ],
        out_specs=pl.BlockSpec((MOD_ROWS, tn), lambda j: (0, j)),
        out_shape=jax.ShapeDtypeStruct((MOD_ROWS, n_cols), F32),
        compiler_params=_params("arbitrary"),
        name="adaln_mod",
    )(c_rows, w_mod, b_mod.reshape(1, n))


def _rope_tables(t_len):
    half = HEAD_DIM // 2
    nf = half // 2
    pos = np.arange(t_len)
    inv_freq = ROPE_THETA ** (-np.arange(nf, dtype=np.float64) / nf)
    ang_r = (pos // GRID_W)[:, None] * inv_freq
    ang_c = (pos % GRID_W)[:, None] * inv_freq
    cos = np.concatenate([np.cos(ang_r)] * 2 + [np.cos(ang_c)] * 2, axis=-1)
    sin = np.concatenate([-np.sin(ang_r), np.sin(ang_r), -np.sin(ang_c), np.sin(ang_c)], axis=-1)
    pair = lambda a: jnp.asarray(np.tile(a, (1, HEAD_PAIR // HEAD_DIM)), F32)
    return pair(cos), pair(sin)


HEAD_PAIR = 2 * HEAD_DIM


def _pair_matrices():
    quarter = HEAD_DIM // 4
    lane = np.arange(HEAD_PAIR)
    src = np.where(lane % (2 * quarter) < quarter, lane + quarter, lane - quarter)
    swap = lane[:, None] == src[None, :]
    head_mean = ((lane[:, None] // HEAD_DIM) == (lane[None, :] // HEAD_DIM)) / HEAD_DIM
    return jnp.asarray(swap, BF16), jnp.asarray(head_mean, BF16)


def _norm_modulate_into(x_ref, g_ref, shift, scale, h_scr, rows=128):
    tm = x_ref.shape[0]
    gain = g_ref[...] * (1.0 + scale)

    def body(r, carry):
        sl = pl.ds(pl.multiple_of(r * rows, rows), rows)
        x = x_ref[sl, :]
        ms = jnp.mean(x * x, axis=-1, keepdims=True)
        h_scr[sl, :] = (x * lax.rsqrt(ms + RMS_EPS) * gain + shift).astype(BF16)
        return carry

    lax.fori_loop(0, tm // rows, body, 0)


def _inproj_kernel(tiles_ref, x_ref, mod_ref, g_ref, w_ref, cos_ref, sin_ref, swap_ref, hmean_ref, qg_ref, kg_ref,
                   o_ref, h_scr, *, rope):
    j = pl.program_id(1)
    t = tiles_ref[j]
    tn = o_ref.shape[1]
    hd = HEAD_DIM
    q_tile0 = HG_COLS // tn
    kv_tile = (HG_COLS + ATT_WIDTH) // tn

    @pl.when(j == 0)
    def _():
        _norm_modulate_into(x_ref, g_ref, mod_ref[0, 0:1, :], mod_ref[0, 1:2, :], h_scr)

    def project():
        return _dot(h_scr[...], w_ref[...].astype(BF16))

    def qk_prep(acc, n_heads, g_row, scale):
        cols = [slice(p * HEAD_PAIR, (p + 1) * HEAD_PAIR) for p in range(n_heads * hd // HEAD_PAIR)]
        x = [acc[:, c] for c in cols]
        gain = g_row * scale
        ms = [_dot((xh * xh).astype(BF16), hmean_ref[...]) for xh in x]
        y = [xh * lax.rsqrt(m + RMS_EPS) * gain for xh, m in zip(x, ms)]
        if rope:
            swapped = [_dot(yh.astype(BF16), swap_ref[...]) for yh in y]
            y = [yh * cos_ref[...] + sh * sin_ref[...] for yh, sh in zip(y, swapped)]
        for c, yh in zip(cols, y):
            o_ref[:, c] = yh.astype(o_ref.dtype)

    @pl.when(t < q_tile0)
    def _():
        o_ref[...] = project().astype(o_ref.dtype)

    @pl.when((t >= q_tile0) & (t < kv_tile))
    def _():
        qk_prep(project(), tn // hd, qg_ref[...], hd ** -0.5)

    @pl.when(t == kv_tile)
    def _():
        acc = project()
        qk_prep(acc, ATT_KV_HEADS, kg_ref[...], 1.0)
        o_ref[:, ATT_KV_HEADS * hd:] = acc[:, ATT_KV_HEADS * hd:].astype(o_ref.dtype)


IN_TILE = 512


def _tiles_covering(col_ranges):
    return tuple(t for t in range(IN_COLS // IN_TILE)
                 if any(lo < (t + 1) * IN_TILE and t * IN_TILE < hi for lo, hi in col_ranges))


def _packed_col(col_tiles, col):
    return col_tiles.index(col // IN_TILE) * IN_TILE + col % IN_TILE


def _in_projection(x2d, mod, norm_g, w_in, q_norm_g, k_norm_g, cos, sin, tokens_per_mod, tm, rope, col_tiles):
    m, d = x2d.shape
    tn = IN_TILE
    assert HG_COLS % tn == 0 and ATT_WIDTH % tn == 0 and 2 * ATT_KV_HEADS * HEAD_DIM == tn
    tiles_per_mod = tokens_per_mod // tm
    tiles_per_seq = cos.shape[0] // tm
    vec = pl.BlockSpec((1, HEAD_PAIR), lambda i, j, tiles: (0, 0))
    tab = pl.BlockSpec((tm, HEAD_PAIR), lambda i, j, tiles: (i % tiles_per_seq, 0))
    mat = pl.BlockSpec((HEAD_PAIR, HEAD_PAIR), lambda i, j, tiles: (0, 0))
    per_pair = lambda g: jnp.tile(g.reshape(1, HEAD_DIM), (1, HEAD_PAIR // HEAD_DIM))
    grid_spec = pltpu.PrefetchScalarGridSpec(
        num_scalar_prefetch=1,
        grid=(m // tm, len(col_tiles)),
        in_specs=[
            pl.BlockSpec((tm, d), lambda i, j, tiles: (i, 0)),
            pl.BlockSpec((1, mod.shape[1], d), lambda i, j, tiles: (i // tiles_per_mod, 0, 0)),
            pl.BlockSpec((1, d), lambda i, j, tiles: (0, 0)),
            pl.BlockSpec((d, tn), lambda i, j, tiles: (0, tiles[j])),
            tab, tab, mat, mat, vec, vec,
        ],
        out_specs=pl.BlockSpec((tm, tn), lambda i, j, tiles: (i, j)),
        scratch_shapes=[pltpu.VMEM((tm, d), BF16)],
    )
    return pl.pallas_call(
        functools.partial(_inproj_kernel, rope=rope),
        grid_spec=grid_spec,
        out_shape=jax.ShapeDtypeStruct((m, len(col_tiles) * tn), BF16),
        compiler_params=_params("parallel", "arbitrary"),
        name="in_proj",
    )(jnp.asarray(col_tiles, jnp.int32), x2d, mod, norm_g.reshape(1, d), w_in, cos, sin,
      *_pair_matrices(), per_pair(q_norm_g), per_pair(k_norm_g))


HG_BLOCK = 256
CAST_ROWS = 16


def _cast_rows(src_ref, dst_ref):
    def body(r, carry):
        sl = pl.ds(pl.multiple_of(r * CAST_ROWS, CAST_ROWS), CAST_ROWS)
        dst_ref[sl, :] = src_ref[sl, :].astype(dst_ref.dtype)
        return carry

    lax.fori_loop(0, src_ref.shape[0] // CAST_ROWS, body, 0)


def _chunk_masks():
    r = np.arange(HG_BLOCK)[:, None]
    c = np.arange(HG_BLOCK)[None, :]
    same = (r // HG_CHUNK) == (c // HG_CHUNK)
    return (same & (r >= c)), (same & (r <= c)), same


def _hgrn_kernel(q_ref, ff_ref, fb_ref, i_ref, g_ref, cff_ref, cfb_ref, ci_ref, lb_ref, ng_ref,
                 cumf_ref, cumb_ref, dmask_ref, c_ref, wm_ref, bm_ref, *rest, n_cast):
    cast_src = rest[:n_cast]
    o_ref, mod_ref = rest[n_cast:n_cast + 2]
    cast_dst = rest[n_cast + 2:2 * n_cast + 2]
    of_scr, ob_scr = rest[2 * n_cast + 2:]

    mod_ref[...] = _dot(_silu(c_ref[...]).astype(BF16), wm_ref[...].astype(BF16)) + bm_ref[...]
    for src, dst in zip(cast_src, cast_dst):
        _cast_rows(src, dst)

    t_len = q_ref.shape[1]
    l_len = ci_ref.shape[1]
    c = HG_CHUNK
    blk = HG_BLOCK
    nch = blk // c
    n_blk = t_len // blk
    assert l_len == blk

    a = lb_ref[...]
    e = jnp.exp(a - jnp.max(a, axis=0, keepdims=True))
    lb = e[0] / jnp.sum(e, axis=0)

    def gates(f_pre, lb_d):
        f = 0.5 * (1.0 + lb_d) + (0.5 * (1.0 - lb_d)) * jnp.tanh(0.5 * f_pre.astype(F32))
        log_f = jnp.log2(f)
        hi = lax.bitcast_convert_type(lax.bitcast_convert_type(log_f, jnp.uint32) & jnp.uint32(0xFFFF0000), F32)
        return 1.0 - f, jnp.concatenate([hi.astype(BF16), (log_f - hi).astype(BF16)], axis=1)

    def decays(parts, cum_ref, reverse):
        b = _dot(cum_ref[...], parts)
        b = b[:, 0:128] + b[:, 128:256]
        b3 = b.reshape(nch, c, HG_DK)
        tot = b3[:, 0:1, :] if reverse else b3[:, c - 1:c, :]
        b_tot = jnp.broadcast_to(tot, (nch, c, HG_DK)).reshape(blk, HG_DK)
        return b, b_tot, jnp.exp2(tot)

    def increments(key, b, b_tot, v):
        k_dec = (key * jnp.exp2(b_tot - b)).astype(BF16)
        v_t = v.astype(BF16).T
        upd = []
        for h in range(blk // 128):
            kd = k_dec[h * 128:(h + 1) * 128, :]
            kd4 = jnp.concatenate([kd] * 4, axis=1) * dmask_ref[...]
            u4 = _dot(v_t[:, h * 128:(h + 1) * 128], kd4)
            upd += [u4[:, j * HG_DK:(j + 1) * HG_DK] for j in range(4)]
        return upd

    def scores(q_pre, key, b):
        q_dec = (_silu(q_pre.astype(F32)) * jnp.exp2(b)).astype(BF16)
        k_inv = (key * jnp.exp2(-b)).astype(BF16)
        return q_dec, lax.dot_general(q_dec, k_inv, _NT, preferred_element_type=F32)

    def intra(s, v, cum_ref):
        return _dot(s.astype(BF16) * cum_ref[...], v.astype(BF16))

    def recur_pair(st, chains):
        outs = [[None] * nch for _ in chains]
        st = list(st)
        for step in range(nch):
            for d, (dec, upd, q_dec, o_intra) in enumerate(chains):
                n = nch - 1 - step if d == 1 else step
                if q_dec is not None:
                    rows = slice(n * c, (n + 1) * c)
                    outs[d][n] = o_intra[rows, :] + _dot(q_dec[rows, :], st[d].T.astype(BF16))
                st[d] = st[d] * dec[n] + upd[n]
        return [jnp.concatenate(o, axis=0) if o[0] is not None else None for o in outs], st

    lb_d = (lb[0:1, :], lb[1:2, :])
    cum = (cumf_ref, cumb_ref)
    f_lat = (ff_ref, fb_ref)
    o_scr = (of_scr, ob_scr)

    cv = ci_ref[0]
    ctx_chains = []
    for d, f_ref in enumerate((cff_ref, cfb_ref)):
        key, parts = gates(f_ref[0], lb_d[d])
        b, b_tot, dec = decays(parts, cum[d], d == 1)
        ctx_chains.append((dec, increments(key, b, b_tot, cv), None, None))
    _, st = recur_pair([jnp.zeros((HG_DV, HG_DK), F32)] * 2, ctx_chains)

    blocks_per_step = 2

    def lat_body(r, carry):
        chains = []
        for u in range(blocks_per_step):
            up = r * blocks_per_step + u
            chains.append((0, pl.ds(pl.multiple_of(up * blk, blk), blk)))
            chains.append((1, pl.ds(pl.multiple_of((n_blk - 1 - up) * blk, blk), blk)))
        g = [gates(f_lat[d][0, sl, :], lb_d[d]) for d, sl in chains]
        dk = [decays(g[i][1], cum[d], d == 1) for i, (d, sl) in enumerate(chains)]
        v = [i_ref[0, sl, :] for d, sl in chains]
        upd = [increments(g[i][0], dk[i][0], dk[i][1], v[i]) for i in range(len(chains))]
        qs = [scores(q_ref[0, sl, :], g[i][0], dk[i][0]) for i, (d, sl) in enumerate(chains)]
        o_in = [intra(qs[i][1], v[i], cum[d]) for i, (d, sl) in enumerate(chains)]
        st = list(carry)
        for u in range(blocks_per_step):
            pair = range(2 * u, 2 * u + 2)
            outs, st = recur_pair(st, [(dk[i][2], upd[i], qs[i][0], o_in[i]) for i in pair])
            for i in pair:
                d, sl = chains[i]
                o_scr[d][sl, :] = outs[d]
        return tuple(st)

    lax.fori_loop(0, n_blk // blocks_per_step, lat_body, tuple(st))

    ng = ng_ref[...]

    def fin_body(r, carry):
        sl = pl.ds(pl.multiple_of(r * blk, blk), blk)
        o = of_scr[sl, :] + ob_scr[sl, :]
        y = o * lax.rsqrt(jnp.mean(o * o, axis=-1, keepdims=True) + RMS_EPS) * ng
        o_ref[0, sl, :] = (y * _silu(g_ref[0, sl, :].astype(F32))).astype(o_ref.dtype)
        return carry

    lax.fori_loop(0, n_blk, fin_body, 0, unroll=4)


def _hgrn_mixer(p_lat, p_ctx, ctx_tiles, hg_lb, norm_g, c_rows, w_mod, b_mod, mod_col0, cast_weights):
    bsz, t_len, _ = p_lat.shape
    l_len = p_ctx.shape[1]
    w = HG_DK
    d, n_mod = w_mod.shape
    n_steps = bsz * HG_HEADS
    mod_tn = (n_mod - mod_col0) // n_steps
    assert mod_tn % 128 == 0 and mod_col0 % mod_tn == 0 and mod_tn * n_steps == n_mod - mod_col0
    assert all(m.shape[0] % (n_steps * CAST_ROWS) == 0 for m in cast_weights)

    def slab(m):
        return pl.BlockSpec((m.shape[0] // n_steps, m.shape[1]), lambda b, h: (b * HG_HEADS + h, 0))

    def lat(group):
        return pl.BlockSpec((1, t_len, w), lambda b, h: (b, 0, group * HG_HEADS + h))

    def ctx(group):
        blk0 = _packed_col(ctx_tiles, group * HG_HEADS * w) // w
        return pl.BlockSpec((1, l_len, w), lambda b, h: (b, 0, blk0 + h))

    def const(shape):
        return pl.BlockSpec(shape, lambda b, h: (0,) * len(shape))

    lower, upper, same = _chunk_masks()
    diag4 = np.tile(same[:128, ::HG_CHUNK][:, :4, None], (1, 1, HG_DK)).reshape(128, 4 * HG_DK)
    seq_f32 = pltpu.VMEM((t_len, w), F32)
    return pl.pallas_call(
        functools.partial(_hgrn_kernel, n_cast=len(cast_weights)),
        grid=(bsz, HG_HEADS),
        in_specs=[
            lat(0), lat(1), lat(2), lat(3), lat(4),
            ctx(1), ctx(2), ctx(3),
            pl.BlockSpec((hg_lb.shape[0], 2, w), lambda b, h: (0, 0, h)),
            const((1, HG_DV)),
            const((HG_BLOCK, HG_BLOCK)), const((HG_BLOCK, HG_BLOCK)),
            const((128, 4 * HG_DK)),
            const((MOD_ROWS, d)),
            pl.BlockSpec((d, mod_tn), lambda b, h: (0, mod_col0 // mod_tn + b * HG_HEADS + h)),
            pl.BlockSpec((1, mod_tn), lambda b, h: (0, mod_col0 // mod_tn + b * HG_HEADS + h)),
        ] + [slab(m) for m in cast_weights],
        out_specs=[pl.BlockSpec((1, t_len, HG_DV), lambda b, h: (b, 0, h)),
                   pl.BlockSpec((MOD_ROWS, mod_tn), lambda b, h: (0, b * HG_HEADS + h))]
        + [slab(m) for m in cast_weights],
        out_shape=[jax.ShapeDtypeStruct((bsz, t_len, HG_WIDTH), BF16),
                   jax.ShapeDtypeStruct((MOD_ROWS, n_mod - mod_col0), F32)]
        + [jax.ShapeDtypeStruct(m.shape, BF16) for m in cast_weights],
        scratch_shapes=[seq_f32, seq_f32],
        compiler_params=_params("parallel", "parallel"),
        name="hgrn2_mixer",
    )(p_lat, p_lat, p_lat, p_lat, p_lat, p_ctx, p_ctx, p_ctx, hg_lb, norm_g.reshape(1, HG_DV),
      jnp.asarray(lower, BF16), jnp.asarray(upper, BF16), jnp.asarray(diag4, BF16),
      c_rows, w_mod, b_mod.reshape(1, n_mod), *cast_weights)


ATT_QROWS = 8 * ATT_BLOCK


def _attn_kernel(q_ref, kvp_ref, kvc_ref, kvn_ref, ckv_ref, sink_ref, o_ref):
    n = pl.program_id(1)
    n_steps = pl.num_programs(1)
    rows = q_ref.shape[1]
    blk = ATT_BLOCK
    hd = HEAD_DIM
    l_len = ckv_ref.shape[1]
    band = 3 * blk
    n_keys = band + l_len
    ones = jnp.ones((n_keys, hd), BF16)

    def band_rows(cols):
        return jnp.concatenate([kvp_ref[0, rows - blk:, cols], kvc_ref[0, :, cols], kvn_ref[0, :blk, cols]], axis=0)

    kv_slabs = [band_rows(slice(j * hd, (j + 1) * hd)) for j in range(2 * ATT_KV_HEADS)]
    ctx_slabs = [ckv_ref[0, :, j * hd:(j + 1) * hd] for j in range(2 * ATT_KV_HEADS)]
    qi = lax.broadcasted_iota(jnp.int32, (blk, n_keys), 0)
    kj = lax.broadcasted_iota(jnp.int32, (blk, n_keys), 1)
    in_window = jnp.abs(kj - blk - qi) <= WINDOW

    for sub in range(rows // blk):
        lo = sub * blk
        k_pos = n * rows + lo - blk + kj
        ok = (kj >= band) | (in_window & (k_pos >= 0) & (k_pos < n_steps * rows))
        bias = jnp.where(ok, 0.0, NEG_INF)
        k_all = [jnp.concatenate([kv_slabs[j][lo:lo + band], ctx_slabs[j]], axis=0) for j in range(ATT_KV_HEADS)]
        v_aug = [jnp.concatenate([jnp.concatenate([kv_slabs[ATT_KV_HEADS + j][lo:lo + band],
                                                   ctx_slabs[ATT_KV_HEADS + j]], axis=0), ones], axis=1)
                 for j in range(ATT_KV_HEADS)]
        heads = range(ATT_HEADS)
        s = [lax.dot_general(q_ref[0, lo:lo + blk, h * hd:(h + 1) * hd], k_all[h // ATT_GROUP], _NT,
                             preferred_element_type=F32) + bias for h in heads]
        sink = [sink_ref[:, h:h + 1] for h in heads]
        m = [jnp.maximum(jnp.max(s[h], axis=-1, keepdims=True), sink[h]) for h in heads]
        e = [jnp.exp(s[h] - m[h]).astype(BF16) for h in heads]
        oa = [_dot(e[h], v_aug[h // ATT_GROUP]) for h in heads]
        for h in heads:
            o = oa[h][:, :hd] / (oa[h][:, hd:] + jnp.exp(sink[h] - m[h]))
            o_ref[0, lo:lo + blk, h * hd:(h + 1) * hd] = o.astype(o_ref.dtype)


def _window_attention(p_lat, p_ctx, ctx_tiles, sink):
    bsz, t_len, _ = p_lat.shape
    l_len = p_ctx.shape[1]
    nb = t_len // ATT_QROWS
    kvw = 2 * ATT_KV_HEADS * HEAD_DIM
    q_blk = HG_COLS // ATT_WIDTH
    kv_blk = (HG_COLS + ATT_WIDTH) // kvw
    ckv_blk = _packed_col(ctx_tiles, HG_COLS + ATT_WIDTH) // kvw

    def prev(n):
        return jnp.maximum(n - 1, 0)

    def nxt(n):
        return jnp.minimum(n + 1, nb - 1)

    def kv_spec(shift):
        return pl.BlockSpec((1, ATT_QROWS, kvw), lambda b, n: (b, shift(n), kv_blk))

    same = lambda n: n
    return pl.pallas_call(
        _attn_kernel,
        grid=(bsz, nb),
        in_specs=[
            pl.BlockSpec((1, ATT_QROWS, ATT_WIDTH), lambda b, n: (b, n, q_blk)),
            kv_spec(prev), kv_spec(same), kv_spec(nxt),
            pl.BlockSpec((1, l_len, kvw), lambda b, n: (b, 0, ckv_blk)),
            pl.BlockSpec((1, ATT_HEADS), lambda b, n: (0, 0)),
        ],
        out_specs=pl.BlockSpec((1, ATT_QROWS, ATT_WIDTH), lambda b, n: (b, n, 0)),
        out_shape=jax.ShapeDtypeStruct((bsz, t_len, ATT_WIDTH), BF16),
        compiler_params=_params("parallel", "parallel"),
        name="window_gqa",
    )(p_lat, p_lat, p_lat, p_lat, p_ctx, sink.reshape(1, ATT_HEADS))


def _outproj_kernel(x_ref, mod_ref, hg_ref, at_ref, w_ref, o_ref):
    kh = hg_ref.shape[1]
    acc = _dot(hg_ref[...], w_ref[:kh, :]) + _dot(at_ref[...], w_ref[kh:, :])
    o_ref[...] = x_ref[...] + mod_ref[0, 2:3, :] * acc


def _out_projection(x2d, mod, hg2d, at2d, w_out, tokens_per_mod, tm=512):
    m, d = x2d.shape
    kh = hg2d.shape[1]
    tiles_per_mod = tokens_per_mod // tm
    return pl.pallas_call(
        _outproj_kernel,
        grid=(m // tm,),
        in_specs=[
            pl.BlockSpec((tm, d), lambda i: (i, 0)),
            pl.BlockSpec((1, N_MOD, d), lambda i: (i // tiles_per_mod, 0, 0)),
            pl.BlockSpec((tm, kh), lambda i: (i, 0)),
            pl.BlockSpec((tm, kh), lambda i: (i, 0)),
            pl.BlockSpec(w_out.shape, lambda i: (0, 0), pipeline_mode=pl.Buffered(1)),
        ],
        out_specs=pl.BlockSpec((tm, d), lambda i: (i, 0)),
        out_shape=jax.ShapeDtypeStruct((m, d), F32),
        compiler_params=_params("parallel"),
        name="out_proj",
    )(x2d, mod, hg2d, at2d, w_out)


def _ffn_kernel(x_ref, mod_ref, g_ref, wg_ref, wu_ref, wd_ref, o_ref, h_scr):
    j = pl.program_id(1)

    @pl.when(j == 0)
    def _():
        _norm_modulate_into(x_ref, g_ref, mod_ref[0, 3:4, :], mod_ref[0, 4:5, :], h_scr)

    def gated_down():
        h = h_scr[...]
        gate = _dot(h, wg_ref[...])
        up = _dot(h, wu_ref[...])
        act = (_silu(gate) * up).astype(BF16)
        return mod_ref[0, 5:6, :] * _dot(act, wd_ref[...])

    @pl.when(j == 0)
    def _():
        o_ref[...] = x_ref[...] + gated_down()

    @pl.when(j > 0)
    def _():
        o_ref[...] += gated_down()


def _ffn(x2d, mod, norm_g, w_gate_up, w_down, tokens_per_mod, tm=1024, tf=512):
    m, d = x2d.shape
    d_ff = w_down.shape[0]
    nf = d_ff // tf
    tiles_per_mod = tokens_per_mod // tm
    return pl.pallas_call(
        _ffn_kernel,
        grid=(m // tm, nf),
        in_specs=[
            pl.BlockSpec((tm, d), lambda i, j: (i, 0)),
            pl.BlockSpec((1, N_MOD, d), lambda i, j: (i // tiles_per_mod, 0, 0)),
            pl.BlockSpec((1, d), lambda i, j: (0, 0)),
            pl.BlockSpec((d, tf), lambda i, j: (0, j)),
            pl.BlockSpec((d, tf), lambda i, j: (0, nf + j)),
            pl.BlockSpec((tf, d), lambda i, j: (j, 0)),
        ],
        out_specs=pl.BlockSpec((tm, d), lambda i, j: (i, 0)),
        out_shape=jax.ShapeDtypeStruct((m, d), F32),
        scratch_shapes=[pltpu.VMEM((tm, d), BF16)],
        compiler_params=_params("parallel", "arbitrary"),
        name="swiglu_ffn",
    )(x2d, mod, norm_g.reshape(1, d), w_gate_up, w_gate_up, w_down)


def kernel(x, c, ctx, c_ctx, w_mod, b_mod, norm_mix_g, norm_ffn_g, w_in, hg_lb, hg_norm_g,
           q_norm_g, k_norm_g, attn_sink, w_out, w_gate_up, w_down):
    bsz, t_len, d = x.shape
    l_len = ctx.shape[1]
    depth = w_mod.shape[0]
    assert depth == 1, "single-layer block: the context stream never needs its own outputs"
    assert bsz + 1 <= MOD_ROWS

    c_rows = jnp.concatenate([c, c_ctx[None, :], jnp.zeros((MOD_ROWS - bsz - 1, d), F32)], axis=0)
    n_early = 2 * d
    mod_early = _modulation(c_rows, w_mod[0], b_mod[0], n_early)
    mod_mix = mod_early.reshape(MOD_ROWS, 2, d)

    cos, sin = _rope_tables(t_len)
    lat_tiles = _tiles_covering([(0, IN_COLS)])
    hk = HG_HEADS * HG_DK
    ctx_tiles = _tiles_covering([(hk, 3 * hk + HG_WIDTH), (HG_COLS + ATT_WIDTH, IN_COLS)])
    p_lat = _in_projection(x.reshape(bsz * t_len, d), mod_mix[:bsz], norm_mix_g[0], w_in[0], q_norm_g[0],
                           k_norm_g[0], cos, sin, tokens_per_mod=t_len, tm=1024, rope=True, col_tiles=lat_tiles)
    p_ctx = _in_projection(ctx.reshape(bsz * l_len, d), mod_mix[bsz:bsz + 1], norm_mix_g[0], w_in[0], q_norm_g[0],
                           k_norm_g[0], cos, sin, tokens_per_mod=bsz * l_len, tm=bsz * l_len, rope=False,
                           col_tiles=ctx_tiles)
    p_lat = p_lat.reshape(bsz, t_len, -1)
    p_ctx = p_ctx.reshape(bsz, l_len, -1)

    hg, mod_late, w_out16, w_gate_up16, w_down16 = _hgrn_mixer(
        p_lat, p_ctx, ctx_tiles, hg_lb, hg_norm_g[0], c_rows, w_mod[0], b_mod[0], n_early,
        cast_weights=(w_out[0], w_gate_up[0], w_down[0]))
    mod_lat = jnp.concatenate([mod_early, mod_late], axis=1).reshape(MOD_ROWS, N_MOD, d)[:bsz]
    at = _window_attention(p_lat, p_ctx, ctx_tiles, attn_sink[0])

    x1 = _out_projection(x.reshape(bsz * t_len, d), mod_lat, hg.reshape(bsz * t_len, HG_WIDTH),
                         at.reshape(bsz * t_len, ATT_WIDTH), w_out16, tokens_per_mod=t_len)
    x2 = _ffn(x1, mod_lat, norm_ffn_g[0], w_gate_up16, w_down16, tokens_per_mod=t_len)
    return x2.reshape(bsz, t_len, d)
```

```python
import functools

import jax
import jax.numpy as jnp
import numpy as np
from jax import lax
from jax.experimental import pallas as pl
from jax.experimental.pallas import tpu as pltpu

F32 = jnp.float32
BF16 = jnp.bfloat16

GRID_W = 64
HG_HEADS = 8
HG_DK = 128
HG_DV = 128
HG_CHUNK = 32
ATT_HEADS = 8
ATT_KV_HEADS = 2
ATT_GROUP = ATT_HEADS // ATT_KV_HEADS
HEAD_DIM = 128
WINDOW = 128
ATT_BLOCK = 128
ROPE_THETA = 10000.0
N_MOD = 6
RMS_EPS = 1e-6
NEG_INF = -1e30
HG_WIDTH = HG_HEADS * HG_DV
ATT_WIDTH = ATT_HEADS * HEAD_DIM
HG_COLS = 3 * HG_HEADS * HG_DK + 2 * HG_HEADS * HG_DV
ATT_COLS = (ATT_HEADS + 2 * ATT_KV_HEADS) * HEAD_DIM
IN_COLS = HG_COLS + ATT_COLS

VMEM_LIMIT_BYTES = 58 * 1024 * 1024
MOD_ROWS = 8

_NT = (((1,), (1,)), ((), ()))


def _dot(a, b):
    return jnp.dot(a, b, preferred_element_type=F32)


def _silu(x):
    h = 0.5 * x
    return h * jnp.tanh(h) + h


def _params(*sem):
    return pltpu.CompilerParams(dimension_semantics=sem, vmem_limit_bytes=VMEM_LIMIT_BYTES)


def _mod_kernel(c_ref, w_ref, b_ref, o_ref):
    a = _silu(c_ref[...]).astype(BF16)
    o_ref[...] = _dot(a, w_ref[...].astype(BF16)) + b_ref[...]


def _modulation(c_rows, w_mod, b_mod, n_cols, tn=1024):
    d, n = w_mod.shape
    return pl.pallas_call(
        _mod_kernel,
        grid=(n_cols // tn,),
        in_specs=[
            pl.BlockSpec((MOD_ROWS, d), lambda j: (0, 0)),
            pl.BlockSpec((d, tn), lambda j: (0, j)),
            pl.BlockSpec((1, tn), lambda j: (0, j)),
        ],
        out_specs=pl.BlockSpec((MOD_ROWS, tn), lambda j: (0, j)),
        out_shape=jax.ShapeDtypeStruct((MOD_ROWS, n_cols), F32),
        compiler_params=_params("arbitrary"),
        name="adaln_mod",
    )(c_rows, w_mod, b_mod.reshape(1, n))


HEAD_PAIR = 2 * HEAD_DIM


def _rope_tables(t_len):
    half = HEAD_DIM // 2
    nf = half // 2
    pos = np.arange(t_len)
    inv_freq = ROPE_THETA ** (-np.arange(nf, dtype=np.float64) / nf)
    ang_r = (pos // GRID_W)[:, None] * inv_freq
    ang_c = (pos % GRID_W)[:, None] * inv_freq
    cos = np.concatenate([np.cos(ang_r)] * 2 + [np.cos(ang_c)] * 2, axis=-1)
    sin = np.concatenate([-np.sin(ang_r), np.sin(ang_r), -np.sin(ang_c), np.sin(ang_c)], axis=-1)
    pair = lambda a: jnp.asarray(np.tile(a, (1, HEAD_PAIR // HEAD_DIM)), F32)
    return pair(cos), pair(sin)


def _pair_matrices():
    quarter = HEAD_DIM // 4
    lane = np.arange(HEAD_PAIR)
    src = np.where(lane % (2 * quarter) < quarter, lane + quarter, lane - quarter)
    swap = lane[:, None] == src[None, :]
    head_mean = ((lane[:, None] // HEAD_DIM) == (lane[None, :] // HEAD_DIM)) / HEAD_DIM
    return jnp.asarray(swap, BF16), jnp.asarray(head_mean, BF16)


def _norm_modulate_into(x_ref, g_ref, shift, scale, h_scr, rows=128):
    tm = x_ref.shape[0]
    gain = g_ref[...] * (1.0 + scale)

    def body(r, carry):
        sl = pl.ds(pl.multiple_of(r * rows, rows), rows)
        x = x_ref[sl, :]
        ms = jnp.mean(x * x, axis=-1, keepdims=True)
        h_scr[sl, :] = (x * lax.rsqrt(ms + RMS_EPS) * gain + shift).astype(BF16)
        return carry

    lax.fori_loop(0, tm // rows, body, 0, unroll=2)


def _inproj_kernel(tiles_ref, x_ref, mod_ref, g_ref, w_ref, cos_ref, sin_ref, swap_ref, hmean_ref, qg_ref, kg_ref,
                   o_ref, h_scr, *, rope):
    j = pl.program_id(1)
    t = tiles_ref[j]
    tn = o_ref.shape[1]
    hd = HEAD_DIM
    q_tile0 = HG_COLS // tn
    kv_tile = (HG_COLS + ATT_WIDTH) // tn

    @pl.when(j == 0)
    def _():
        _norm_modulate_into(x_ref, g_ref, mod_ref[0, 0:1, :], mod_ref[0, 1:2, :], h_scr)

    def project():
        return _dot(h_scr[...], w_ref[...].astype(BF16))

    def qk_prep(acc, n_heads, g_row, scale):
        cols = [slice(p * HEAD_PAIR, (p + 1) * HEAD_PAIR) for p in range(n_heads * hd // HEAD_PAIR)]
        x = [acc[:, c] for c in cols]
        gain = g_row * scale
        ms = [_dot((xh * xh).astype(BF16), hmean_ref[...]) for xh in x]
        y = [xh * lax.rsqrt(m + RMS_EPS) * gain for xh, m in zip(x, ms)]
        if rope:
            swapped = [_dot(yh.astype(BF16), swap_ref[...]) for yh in y]
            y = [yh * cos_ref[...] + sh * sin_ref[...] for yh, sh in zip(y, swapped)]
        for c, yh in zip(cols, y):
            o_ref[:, c] = yh.astype(o_ref.dtype)

    @pl.when(t < q_tile0)
    def _():
        o_ref[...] = project().astype(o_ref.dtype)

    @pl.when((t >= q_tile0) & (t < kv_tile))
    def _():
        qk_prep(project(), tn // hd, qg_ref[...], hd ** -0.5)

    @pl.when(t == kv_tile)
    def _():
        acc = project()
        qk_prep(acc, ATT_KV_HEADS, kg_ref[...], 1.0)
        o_ref[:, ATT_KV_HEADS * hd:] = acc[:, ATT_KV_HEADS * hd:].astype(o_ref.dtype)


IN_TILE = 512


def _tiles_covering(col_ranges):
    return tuple(t for t in range(IN_COLS // IN_TILE)
                 if any(lo < (t + 1) * IN_TILE and t * IN_TILE < hi for lo, hi in col_ranges))


def _packed_col(col_tiles, col):
    return col_tiles.index(col // IN_TILE) * IN_TILE + col % IN_TILE


def _in_projection(x2d, mod, norm_g, w_in, q_norm_g, k_norm_g, cos, sin, tokens_per_mod, tm, rope, col_tiles):
    m, d = x2d.shape
    tn = IN_TILE
    assert HG_COLS % tn == 0 and ATT_WIDTH % tn == 0 and 2 * ATT_KV_HEADS * HEAD_DIM == tn
    tiles_per_mod = tokens_per_mod // tm
    tiles_per_seq = cos.shape[0] // tm
    vec = pl.BlockSpec((1, HEAD_PAIR), lambda i, j, tiles: (0, 0))
    tab = pl.BlockSpec((tm, HEAD_PAIR), lambda i, j, tiles: (i % tiles_per_seq, 0))
    mat = pl.BlockSpec((HEAD_PAIR, HEAD_PAIR), lambda i, j, tiles: (0, 0))
    per_pair = lambda g: jnp.tile(g.reshape(1, HEAD_DIM), (1, HEAD_PAIR // HEAD_DIM))
    grid_spec = pltpu.PrefetchScalarGridSpec(
        num_scalar_prefetch=1,
        grid=(m // tm, len(col_tiles)),
        in_specs=[
            pl.BlockSpec((tm, d), lambda i, j, tiles: (i, 0)),
            pl.BlockSpec((1, mod.shape[1], d), lambda i, j, tiles: (i // tiles_per_mod, 0, 0)),
            pl.BlockSpec((1, d), lambda i, j, tiles: (0, 0)),
            pl.BlockSpec((d, tn), lambda i, j, tiles: (0, tiles[j])),
            tab, tab, mat, mat, vec, vec,
        ],
        out_specs=pl.BlockSpec((tm, tn), lambda i, j, tiles: (i, j)),
        scratch_shapes=[pltpu.VMEM((tm, d), BF16)],
    )
    return pl.pallas_call(
        functools.partial(_inproj_kernel, rope=rope),
        grid_spec=grid_spec,
        out_shape=jax.ShapeDtypeStruct((m, len(col_tiles) * tn), BF16),
        compiler_params=_params("parallel", "arbitrary"),
        name="in_proj",
    )(jnp.asarray(col_tiles, jnp.int32), x2d, mod, norm_g.reshape(1, d), w_in, cos, sin,
      *_pair_matrices(), per_pair(q_norm_g), per_pair(k_norm_g))


HG_BLOCK = 256
CAST_ROWS = 16


def _cast_rows(src_ref, dst_ref):
    def body(r, carry):
        sl = pl.ds(pl.multiple_of(r * CAST_ROWS, CAST_ROWS), CAST_ROWS)
        dst_ref[sl, :] = src_ref[sl, :].astype(dst_ref.dtype)
        return carry

    lax.fori_loop(0, src_ref.shape[0] // CAST_ROWS, body, 0)


def _chunk_masks():
    r = np.arange(HG_BLOCK)[:, None]
    c = np.arange(HG_BLOCK)[None, :]
    same = (r // HG_CHUNK) == (c // HG_CHUNK)
    return (same & (r >= c)), (same & (r <= c)), same


def _hgrn_kernel(q_ref, ff_ref, fb_ref, i_ref, g_ref, cff_ref, cfb_ref, ci_ref, lb_ref, ng_ref,
                 cumf_ref, cumb_ref, dmask_ref, c_ref, wm_ref, bm_ref, *rest, n_cast):
    cast_src = rest[:n_cast]
    o_ref, mod_ref = rest[n_cast:n_cast + 2]
    cast_dst = rest[n_cast + 2:2 * n_cast + 2]
    of_scr, ob_scr = rest[2 * n_cast + 2:]

    mod_ref[...] = _dot(_silu(c_ref[...]).astype(BF16), wm_ref[...].astype(BF16)) + bm_ref[...]
    for src, dst in zip(cast_src, cast_dst):
        _cast_rows(src, dst)

    t_len = q_ref.shape[1]
    l_len = ci_ref.shape[1]
    c = HG_CHUNK
    blk = HG_BLOCK
    nch = blk // c
    n_blk = t_len // blk
    assert l_len == blk

    a = lb_ref[...]
    e = jnp.exp(a - jnp.max(a, axis=0, keepdims=True))
    lb = e[0] / jnp.sum(e, axis=0)

    def gates(f_pre, lb_d):
        f = 0.5 * (1.0 + lb_d) + (0.5 * (1.0 - lb_d)) * jnp.tanh(0.5 * f_pre.astype(F32))
        log_f = jnp.log2(f)
        hi = lax.bitcast_convert_type(lax.bitcast_convert_type(log_f, jnp.uint32) & jnp.uint32(0xFFFF0000), F32)
        return 1.0 - f, jnp.concatenate([hi.astype(BF16), (log_f - hi).astype(BF16)], axis=1)

    def decays(parts, cum_ref, reverse):
        b = _dot(cum_ref[...], parts)
        b = b[:, 0:128] + b[:, 128:256]
        b3 = b.reshape(nch, c, HG_DK)
        tot = b3[:, 0:1, :] if reverse else b3[:, c - 1:c, :]
        b_tot = jnp.broadcast_to(tot, (nch, c, HG_DK)).reshape(blk, HG_DK)
        return b, b_tot, jnp.exp2(tot)

    def increments(key, b, b_tot, v):
        k_dec = (key * jnp.exp2(b_tot - b)).astype(BF16)
        v_t = v.astype(BF16).T
        upd = []
        for h in range(blk // 128):
            kd = k_dec[h * 128:(h + 1) * 128, :]
            kd4 = jnp.concatenate([kd] * 4, axis=1) * dmask_ref[...]
            u4 = _dot(v_t[:, h * 128:(h + 1) * 128], kd4)
            upd += [u4[:, j * HG_DK:(j + 1) * HG_DK] for j in range(4)]
        return upd

    def scores(q_pre, key, b):
        q_dec = (_silu(q_pre.astype(F32)) * jnp.exp2(b)).astype(BF16)
        k_inv = (key * jnp.exp2(-b)).astype(BF16)
        return q_dec, lax.dot_general(q_dec, k_inv, _NT, preferred_element_type=F32)

    def intra(s, v, cum_ref):
        return _dot(s.astype(BF16) * cum_ref[...], v.astype(BF16))

    def recur_pair(st, chains):
        outs = [[None] * nch for _ in chains]
        st = list(st)
        for step in range(nch):
            for d, (dec, upd, q_dec, o_intra) in enumerate(chains):
                n = nch - 1 - step if d == 1 else step
                if q_dec is not None:
                    rows = slice(n * c, (n + 1) * c)
                    outs[d][n] = o_intra[rows, :] + _dot(q_dec[rows, :], st[d].T.astype(BF16))
                st[d] = st[d] * dec[n] + upd[n]
        return [jnp.concatenate(o, axis=0) if o[0] is not None else None for o in outs], st

    lb_d = (lb[0:1, :], lb[1:2, :])
    cum = (cumf_ref, cumb_ref)
    f_lat = (ff_ref, fb_ref)
    o_scr = (of_scr, ob_scr)

    cv = ci_ref[0]
    ctx_chains = []
    for d, f_ref in enumerate((cff_ref, cfb_ref)):
        key, parts = gates(f_ref[0], lb_d[d])
        b, b_tot, dec = decays(parts, cum[d], d == 1)
        ctx_chains.append((dec, increments(key, b, b_tot, cv), None, None))
    _, st = recur_pair([jnp.zeros((HG_DV, HG_DK), F32)] * 2, ctx_chains)

    blocks_per_step = 2

    def lat_body(r, carry):
        chains = []
        for u in range(blocks_per_step):
            up = r * blocks_per_step + u
            chains.append((0, pl.ds(pl.multiple_of(up * blk, blk), blk)))
            chains.append((1, pl.ds(pl.multiple_of((n_blk - 1 - up) * blk, blk), blk)))
        g = [gates(f_lat[d][0, sl, :], lb_d[d]) for d, sl in chains]
        dk = [decays(g[i][1], cum[d], d == 1) for i, (d, sl) in enumerate(chains)]
        v = [i_ref[0, sl, :] for d, sl in chains]
        upd = [increments(g[i][0], dk[i][0], dk[i][1], v[i]) for i in range(len(chains))]
        qs = [scores(q_ref[0, sl, :], g[i][0], dk[i][0]) for i, (d, sl) in enumerate(chains)]
        o_in = [intra(qs[i][1], v[i], cum[d]) for i, (d, sl) in enumerate(chains)]
        st = list(carry)
        for u in range(blocks_per_step):
            pair = range(2 * u, 2 * u + 2)
            outs, st = recur_pair(st, [(dk[i][2], upd[i], qs[i][0], o_in[i]) for i in pair])
            for i in pair:
                d, sl = chains[i]
                o_scr[d][sl, :] = outs[d]
        return tuple(st)

    lax.fori_loop(0, n_blk // blocks_per_step, lat_body, tuple(st))

    ng = ng_ref[...]

    def fin_body(r, carry):
        sl = pl.ds(pl.multiple_of(r * blk, blk), blk)
        o = of_scr[sl, :] + ob_scr[sl, :]
        y = o * lax.rsqrt(jnp.mean(o * o, axis=-1, keepdims=True) + RMS_EPS) * ng
        o_ref[0, sl, :] = (y * _silu(g_ref[0, sl, :].astype(F32))).astype(o_ref.dtype)
        return carry

    lax.fori_loop(0, n_blk, fin_body, 0, unroll=4)


def _hgrn_mixer(p_lat, p_ctx, ctx_tiles, hg_lb, norm_g, c_rows, w_mod, b_mod, mod_col0, cast_weights):
    bsz, t_len, _ = p_lat.shape
    l_len = p_ctx.shape[1]
    w = HG_DK
    d, n_mod = w_mod.shape
    n_steps = bsz * HG_HEADS
    mod_tn = (n_mod - mod_col0) // n_steps
    assert mod_tn % 128 == 0 and mod_col0 % mod_tn == 0 and mod_tn * n_steps == n_mod - mod_col0
    assert all(m.shape[0] % (n_steps * CAST_ROWS) == 0 for m in cast_weights)

    def slab(m):
        return pl.BlockSpec((m.shape[0] // n_steps, m.shape[1]), lambda b, h: (b * HG_HEADS + h, 0))

    def lat(group):
        return pl.BlockSpec((1, t_len, w), lambda b, h: (b, 0, group * HG_HEADS + h))

    def ctx(group):
        blk0 = _packed_col(ctx_tiles, group * HG_HEADS * w) // w
        return pl.BlockSpec((1, l_len, w), lambda b, h: (b, 0, blk0 + h))

    def const(shape):
        return pl.BlockSpec(shape, lambda b, h: (0,) * len(shape))

    lower, upper, same = _chunk_masks()
    diag4 = np.tile(same[:128, ::HG_CHUNK][:, :4, None], (1, 1, HG_DK)).reshape(128, 4 * HG_DK)
    seq_f32 = pltpu.VMEM((t_len, w), F32)
    return pl.pallas_call(
        functools.partial(_hgrn_kernel, n_cast=len(cast_weights)),
        grid=(bsz, HG_HEADS),
        in_specs=[
            lat(0), lat(1), lat(2), lat(3), lat(4),
            ctx(1), ctx(2), ctx(3),
            pl.BlockSpec((hg_lb.shape[0], 2, w), lambda b, h: (0, 0, h)),
            const((1, HG_DV)),
            const((HG_BLOCK, HG_BLOCK)), const((HG_BLOCK, HG_BLOCK)),
            const((128, 4 * HG_DK)),
            const((MOD_ROWS, d)),
            pl.BlockSpec((d, mod_tn), lambda b, h: (0, mod_col0 // mod_tn + b * HG_HEADS + h)),
            pl.BlockSpec((1, mod_tn), lambda b, h: (0, mod_col0 // mod_tn + b * HG_HEADS + h)),
        ] + [slab(m) for m in cast_weights],
        out_specs=[pl.BlockSpec((1, t_len, HG_DV), lambda b, h: (b, 0, h)),
                   pl.BlockSpec((MOD_ROWS, mod_tn), lambda b, h: (0, b * HG_HEADS + h))]
        + [slab(m) for m in cast_weights],
        out_shape=[jax.ShapeDtypeStruct((bsz, t_len, HG_WIDTH), BF16),
                   jax.ShapeDtypeStruct((MOD_ROWS, n_mod - mod_col0), F32)]
        + [jax.ShapeDtypeStruct(m.shape, BF16) for m in cast_weights],
        scratch_shapes=[seq_f32, seq_f32],
        compiler_params=_params("parallel", "parallel"),
        name="hgrn2_mixer",
    )(p_lat, p_lat, p_lat, p_lat, p_lat, p_ctx, p_ctx, p_ctx, hg_lb, norm_g.reshape(1, HG_DV),
      jnp.asarray(lower, BF16), jnp.asarray(upper, BF16), jnp.asarray(diag4, BF16),
      c_rows, w_mod, b_mod.reshape(1, n_mod), *cast_weights)


ATT_QROWS = 8 * ATT_BLOCK


def _attn_kernel(q_ref, kvp_ref, kvc_ref, kvn_ref, ckv_ref, sink_ref, o_ref):
    n = pl.program_id(1)
    n_steps = pl.num_programs(1)
    rows = q_ref.shape[1]
    blk = ATT_BLOCK
    hd = HEAD_DIM
    l_len = ckv_ref.shape[1]
    band = 3 * blk
    n_keys = band + l_len
    ones = jnp.ones((n_keys, hd), BF16)

    def band_rows(cols):
        return jnp.concatenate([kvp_ref[0, rows - blk:, cols], kvc_ref[0, :, cols], kvn_ref[0, :blk, cols]], axis=0)

    kv_slabs = [band_rows(slice(j * hd, (j + 1) * hd)) for j in range(2 * ATT_KV_HEADS)]
    ctx_slabs = [ckv_ref[0, :, j * hd:(j + 1) * hd] for j in range(2 * ATT_KV_HEADS)]
    qi = lax.broadcasted_iota(jnp.int32, (blk, n_keys), 0)
    kj = lax.broadcasted_iota(jnp.int32, (blk, n_keys), 1)
    in_window = jnp.abs(kj - blk - qi) <= WINDOW

    for sub in range(rows // blk):
        lo = sub * blk
        k_pos = n * rows + lo - blk + kj
        ok = (kj >= band) | (in_window & (k_pos >= 0) & (k_pos < n_steps * rows))
        bias = jnp.where(ok, 0.0, NEG_INF)
        k_all = [jnp.concatenate([kv_slabs[j][lo:lo + band], ctx_slabs[j]], axis=0) for j in range(ATT_KV_HEADS)]
        v_aug = [jnp.concatenate([jnp.concatenate([kv_slabs[ATT_KV_HEADS + j][lo:lo + band],
                                                   ctx_slabs[ATT_KV_HEADS + j]], axis=0), ones], axis=1)
                 for j in range(ATT_KV_HEADS)]
        heads = range(ATT_HEADS)
        s = [lax.dot_general(q_ref[0, lo:lo + blk, h * hd:(h + 1) * hd], k_all[h // ATT_GROUP], _NT,
                             preferred_element_type=F32) + bias for h in heads]
        sink = [sink_ref[:, h:h + 1] for h in heads]
        m = [jnp.maximum(jnp.max(s[h], axis=-1, keepdims=True), sink[h]) for h in heads]
        e = [jnp.exp(s[h] - m[h]).astype(BF16) for h in heads]
        oa = [_dot(e[h], v_aug[h // ATT_GROUP]) for h in heads]
        for h in heads:
            o = oa[h][:, :hd] / (oa[h][:, hd:] + jnp.exp(sink[h] - m[h]))
            o_ref[0, lo:lo + blk, h * hd:(h + 1) * hd] = o.astype(o_ref.dtype)


def _window_attention(p_lat, p_ctx, ctx_tiles, sink):
    bsz, t_len, _ = p_lat.shape
    l_len = p_ctx.shape[1]
    nb = t_len // ATT_QROWS
    kvw = 2 * ATT_KV_HEADS * HEAD_DIM
    q_blk = HG_COLS // ATT_WIDTH
    kv_blk = (HG_COLS + ATT_WIDTH) // kvw
    ckv_blk = _packed_col(ctx_tiles, HG_COLS + ATT_WIDTH) // kvw

    def prev(n):
        return jnp.maximum(n - 1, 0)

    def nxt(n):
        return jnp.minimum(n + 1, nb - 1)

    def kv_spec(shift):
        return pl.BlockSpec((1, ATT_QROWS, kvw), lambda b, n: (b, shift(n), kv_blk))

    same = lambda n: n
    return pl.pallas_call(
        _attn_kernel,
        grid=(bsz, nb),
        in_specs=[
            pl.BlockSpec((1, ATT_QROWS, ATT_WIDTH), lambda b, n: (b, n, q_blk)),
            kv_spec(prev), kv_spec(same), kv_spec(nxt),
            pl.BlockSpec((1, l_len, kvw), lambda b, n: (b, 0, ckv_blk)),
            pl.BlockSpec((1, ATT_HEADS), lambda b, n: (0, 0)),
        ],
        out_specs=pl.BlockSpec((1, ATT_QROWS, ATT_WIDTH), lambda b, n: (b, n, 0)),
        out_shape=jax.ShapeDtypeStruct((bsz, t_len, ATT_WIDTH), BF16),
        compiler_params=_params("parallel", "parallel"),
        name="window_gqa",
    )(p_lat, p_lat, p_lat, p_lat, p_ctx, sink.reshape(1, ATT_HEADS))


def _outproj_kernel(x_ref, mod_ref, hg_ref, at_ref, w_ref, o_ref):
    kh = hg_ref.shape[1]
    acc = _dot(hg_ref[...], w_ref[:kh, :]) + _dot(at_ref[...], w_ref[kh:, :])
    o_ref[...] = x_ref[...] + mod_ref[0, 2:3, :] * acc


def _out_projection(x2d, mod, hg2d, at2d, w_out, tokens_per_mod, tm=512):
    m, d = x2d.shape
    kh = hg2d.shape[1]
    tiles_per_mod = tokens_per_mod // tm
    return pl.pallas_call(
        _outproj_kernel,
        grid=(m // tm,),
        in_specs=[
            pl.BlockSpec((tm, d), lambda i: (i, 0)),
            pl.BlockSpec((1, N_MOD, d), lambda i: (i // tiles_per_mod, 0, 0)),
            pl.BlockSpec((tm, kh), lambda i: (i, 0)),
            pl.BlockSpec((tm, kh), lambda i: (i, 0)),
            pl.BlockSpec(w_out.shape, lambda i: (0, 0), pipeline_mode=pl.Buffered(1)),
        ],
        out_specs=pl.BlockSpec((tm, d), lambda i: (i, 0)),
        out_shape=jax.ShapeDtypeStruct((m, d), F32),
        compiler_params=_params("parallel"),
        name="out_proj",
    )(x2d, mod, hg2d, at2d, w_out)


def _ffn_kernel(x_ref, mod_ref, g_ref, wg_ref, wu_ref, wd_ref, o_ref, h_scr):
    j = pl.program_id(1)

    @pl.when(j == 0)
    def _():
        _norm_modulate_into(x_ref, g_ref, mod_ref[0, 3:4, :], mod_ref[0, 4:5, :], h_scr)

    def gated_down():
        h = h_scr[...]
        gate = _dot(h, wg_ref[...])
        up = _dot(h, wu_ref[...])
        act = (_silu(gate) * up).astype(BF16)
        return mod_ref[0, 5:6, :] * _dot(act, wd_ref[...])

    @pl.when(j == 0)
    def _():
        o_ref[...] = x_ref[...] + gated_down()

    @pl.when(j > 0)
    def _():
        o_ref[...] += gated_down()


def _ffn(x2d, mod, norm_g, w_gate_up, w_down, tokens_per_mod, tm=1024, tf=512):
    m, d = x2d.shape
    d_ff = w_down.shape[0]
    nf = d_ff // tf
    tiles_per_mod = tokens_per_mod // tm
    return pl.pallas_call(
        _ffn_kernel,
        grid=(m // tm, nf),
        in_specs=[
            pl.BlockSpec((tm, d), lambda i, j: (i, 0)),
            pl.BlockSpec((1, N_MOD, d), lambda i, j: (i // tiles_per_mod, 0, 0)),
            pl.BlockSpec((1, d), lambda i, j: (0, 0)),
            pl.BlockSpec((d, tf), lambda i, j: (0, j)),
            pl.BlockSpec((d, tf), lambda i, j: (0, nf + j)),
            pl.BlockSpec((tf, d), lambda i, j: (j, 0)),
        ],
        out_specs=pl.BlockSpec((tm, d), lambda i, j: (i, 0)),
        out_shape=jax.ShapeDtypeStruct((m, d), F32),
        scratch_shapes=[pltpu.VMEM((tm, d), BF16)],
        compiler_params=_params("parallel", "arbitrary"),
        name="swiglu_ffn",
    )(x2d, mod, norm_g.reshape(1, d), w_gate_up, w_gate_up, w_down)


def kernel(x, c, ctx, c_ctx, w_mod, b_mod, norm_mix_g, norm_ffn_g, w_in, hg_lb, hg_norm_g,
           q_norm_g, k_norm_g, attn_sink, w_out, w_gate_up, w_down):
    bsz, t_len, d = x.shape
    l_len = ctx.shape[1]
    depth = w_mod.shape[0]
    assert depth == 1, "single-layer block: the context stream never needs its own outputs"
    assert bsz + 1 <= MOD_ROWS

    c_rows = jnp.concatenate([c, c_ctx[None, :], jnp.zeros((MOD_ROWS - bsz - 1, d), F32)], axis=0)
    n_early = 2 * d
    mod_early = _modulation(c_rows, w_mod[0], b_mod[0], n_early)
    mod_mix = mod_early.reshape(MOD_ROWS, 2, d)

    cos, sin = _rope_tables(t_len)
    lat_tiles = _tiles_covering([(0, IN_COLS)])
    hk = HG_HEADS * HG_DK
    ctx_tiles = _tiles_covering([(hk, 3 * hk + HG_WIDTH), (HG_COLS + ATT_WIDTH, IN_COLS)])
    p_lat = _in_projection(x.reshape(bsz * t_len, d), mod_mix[:bsz], norm_mix_g[0], w_in[0], q_norm_g[0],
                           k_norm_g[0], cos, sin, tokens_per_mod=t_len, tm=1024, rope=True, col_tiles=lat_tiles)
    p_ctx = _in_projection(ctx.reshape(bsz * l_len, d), mod_mix[bsz:bsz + 1], norm_mix_g[0], w_in[0], q_norm_g[0],
                           k_norm_g[0], cos, sin, tokens_per_mod=bsz * l_len, tm=bsz * l_len, rope=False,
                           col_tiles=ctx_tiles)
    p_lat = p_lat.reshape(bsz, t_len, -1)
    p_ctx = p_ctx.reshape(bsz, l_len, -1)

    hg, mod_late, w_out16, w_gate_up16, w_down16 = _hgrn_mixer(
        p_lat, p_ctx, ctx_tiles, hg_lb, hg_norm_g[0], c_rows, w_mod[0], b_mod[0], n_early,
        cast_weights=(w_out[0], w_gate_up[0], w_down[0]))
    mod_lat = jnp.concatenate([mod_early, mod_late], axis=1).reshape(MOD_ROWS, N_MOD, d)[:bsz]
    at = _window_attention(p_lat, p_ctx, ctx_tiles, attn_sink[0])

    x1 = _out_projection(x.reshape(bsz * t_len, d), mod_lat, hg.reshape(bsz * t_len, HG_WIDTH),
                         at.reshape(bsz * t_len, ATT_WIDTH), w_out16, tokens_per_mod=t_len)
    x2 = _ffn(x1, mod_lat, norm_ffn_g[0], w_gate_up16, w_down16, tokens_per_mod=t_len)
    return x2.reshape(bsz, t_len, d)
```

```python
import functools

import jax
import jax.numpy as jnp
import numpy as np
from jax import lax
from jax.experimental import pallas as pl
from jax.experimental.pallas import tpu as pltpu

F32 = jnp.float32
BF16 = jnp.bfloat16

GRID_W = 64
HG_HEADS = 8
HG_DK = 128
HG_DV = 128
HG_CHUNK = 32
ATT_HEADS = 8
ATT_KV_HEADS = 2
ATT_GROUP = ATT_HEADS // ATT_KV_HEADS
HEAD_DIM = 128
WINDOW = 128
ATT_BLOCK = 128
ROPE_THETA = 10000.0
N_MOD = 6
RMS_EPS = 1e-6
NEG_INF = -1e30
HG_WIDTH = HG_HEADS * HG_DV
ATT_WIDTH = ATT_HEADS * HEAD_DIM
HG_COLS = 3 * HG_HEADS * HG_DK + 2 * HG_HEADS * HG_DV
ATT_COLS = (ATT_HEADS + 2 * ATT_KV_HEADS) * HEAD_DIM
IN_COLS = HG_COLS + ATT_COLS

VMEM_LIMIT_BYTES = 58 * 1024 * 1024
MOD_ROWS = 8

_NT = (((1,), (1,)), ((), ()))


def _dot(a, b):
    return jnp.dot(a, b, preferred_element_type=F32)


def _silu(x):
    h = 0.5 * x
    return h * jnp.tanh(h) + h


def _params(*sem):
    return pltpu.CompilerParams(dimension_semantics=sem, vmem_limit_bytes=VMEM_LIMIT_BYTES)


def _mod_kernel(c_ref, w_ref, b_ref, o_ref):
    a = _silu(c_ref[...]).astype(BF16)
    o_ref[...] = _dot(a, w_ref[...].astype(BF16)) + b_ref[...]


def _modulation(c_rows, w_mod, b_mod, n_cols, tn=1024):
    d, n = w_mod.shape
    return pl.pallas_call(
        _mod_kernel,
        grid=(n_cols // tn,),
        in_specs=[
            pl.BlockSpec((MOD_ROWS, d), lambda j: (0, 0)),
            pl.BlockSpec((d, tn), lambda j: (0, j)),
            pl.BlockSpec((1, tn), lambda j: (0, j)),
        ],
        out_specs=pl.BlockSpec((MOD_ROWS, tn), lambda j: (0, j)),
        out_shape=jax.ShapeDtypeStruct((MOD_ROWS, n_cols), F32),
        compiler_params=_params("arbitrary"),
        name="adaln_mod",
    )(c_rows, w_mod, b_mod.reshape(1, n))


HEAD_PAIR = 2 * HEAD_DIM


def _rope_tables(t_len):
    half = HEAD_DIM // 2
    nf = half // 2
    pos = np.arange(t_len)
    inv_freq = ROPE_THETA ** (-np.arange(nf, dtype=np.float64) / nf)
    ang_r = (pos // GRID_W)[:, None] * inv_freq
    ang_c = (pos % GRID_W)[:, None] * inv_freq
    cos = np.concatenate([np.cos(ang_r)] * 2 + [np.cos(ang_c)] * 2, axis=-1)
    sin = np.concatenate([-np.sin(ang_r), np.sin(ang_r), -np.sin(ang_c), np.sin(ang_c)], axis=-1)
    pair = lambda a: jnp.asarray(np.tile(a, (1, HEAD_PAIR // HEAD_DIM)), F32)
    return pair(cos), pair(sin)


def _pair_matrices():
    quarter = HEAD_DIM // 4
    lane = np.arange(HEAD_PAIR)
    src = np.where(lane % (2 * quarter) < quarter, lane + quarter, lane - quarter)
    swap = lane[:, None] == src[None, :]
    head_mean = ((lane[:, None] // HEAD_DIM) == (lane[None, :] // HEAD_DIM)) / HEAD_DIM
    return jnp.asarray(swap, BF16), jnp.asarray(head_mean, BF16)


def _norm_modulate_into(x_ref, g_ref, shift, scale, h_scr, rows=128):
    tm = x_ref.shape[0]
    gain = g_ref[...] * (1.0 + scale)

    def body(r, carry):
        sl = pl.ds(pl.multiple_of(r * rows, rows), rows)
        x = x_ref[sl, :]
        ms = jnp.mean(x * x, axis=-1, keepdims=True)
        h_scr[sl, :] = (x * lax.rsqrt(ms + RMS_EPS) * gain + shift).astype(BF16)
        return carry

    lax.fori_loop(0, tm // rows, body, 0, unroll=2)


def _inproj_kernel(tiles_ref, x_ref, mod_ref, g_ref, w_ref, cos_ref, sin_ref, swap_ref, hmean_ref, qg_ref, kg_ref,
                   o_ref, h_scr, *, rope):
    j = pl.program_id(1)
    t = tiles_ref[j]
    tn = o_ref.shape[1]
    hd = HEAD_DIM
    q_tile0 = HG_COLS // tn
    kv_tile = (HG_COLS + ATT_WIDTH) // tn

    @pl.when(j == 0)
    def _():
        _norm_modulate_into(x_ref, g_ref, mod_ref[0, 0:1, :], mod_ref[0, 1:2, :], h_scr)

    def project():
        return _dot(h_scr[...], w_ref[...].astype(BF16))

    def qk_prep(acc, n_heads, g_row, scale):
        cols = [slice(p * HEAD_PAIR, (p + 1) * HEAD_PAIR) for p in range(n_heads * hd // HEAD_PAIR)]
        x = [acc[:, c] for c in cols]
        gain = g_row * scale
        ms = [_dot((xh * xh).astype(BF16), hmean_ref[...]) for xh in x]
        y = [xh * lax.rsqrt(m + RMS_EPS) * gain for xh, m in zip(x, ms)]
        if rope:
            swapped = [_dot(yh.astype(BF16), swap_ref[...]) for yh in y]
            y = [yh * cos_ref[...] + sh * sin_ref[...] for yh, sh in zip(y, swapped)]
        for c, yh in zip(cols, y):
            o_ref[:, c] = yh.astype(o_ref.dtype)

    @pl.when(t < q_tile0)
    def _():
        o_ref[...] = project().astype(o_ref.dtype)

    @pl.when((t >= q_tile0) & (t < kv_tile))
    def _():
        qk_prep(project(), tn // hd, qg_ref[...], hd ** -0.5)

    @pl.when(t == kv_tile)
    def _():
        acc = project()
        qk_prep(acc, ATT_KV_HEADS, kg_ref[...], 1.0)
        o_ref[:, ATT_KV_HEADS * hd:] = acc[:, ATT_KV_HEADS * hd:].astype(o_ref.dtype)


IN_TILE = 512


def _tiles_covering(col_ranges):
    return tuple(t for t in range(IN_COLS // IN_TILE)
                 if any(lo < (t + 1) * IN_TILE and t * IN_TILE < hi for lo, hi in col_ranges))


def _packed_col(col_tiles, col):
    return col_tiles.index(col // IN_TILE) * IN_TILE + col % IN_TILE


def _in_projection(x2d, mod, norm_g, w_in, q_norm_g, k_norm_g, cos, sin, tokens_per_mod, tm, rope, col_tiles,
                   x_buffers=2):
    m, d = x2d.shape
    tn = IN_TILE
    assert HG_COLS % tn == 0 and ATT_WIDTH % tn == 0 and 2 * ATT_KV_HEADS * HEAD_DIM == tn
    tiles_per_mod = tokens_per_mod // tm
    tiles_per_seq = cos.shape[0] // tm
    vec = pl.BlockSpec((1, HEAD_PAIR), lambda i, j, tiles: (0, 0))
    tab = pl.BlockSpec((tm, HEAD_PAIR), lambda i, j, tiles: (i % tiles_per_seq, 0))
    mat = pl.BlockSpec((HEAD_PAIR, HEAD_PAIR), lambda i, j, tiles: (0, 0))
    per_pair = lambda g: jnp.tile(g.reshape(1, HEAD_DIM), (1, HEAD_PAIR // HEAD_DIM))
    grid_spec = pltpu.PrefetchScalarGridSpec(
        num_scalar_prefetch=1,
        grid=(m // tm, len(col_tiles)),
        in_specs=[
            pl.BlockSpec((tm, d), lambda i, j, tiles: (i, 0), pipeline_mode=pl.Buffered(x_buffers)),
            pl.BlockSpec((1, mod.shape[1], d), lambda i, j, tiles: (i // tiles_per_mod, 0, 0)),
            pl.BlockSpec((1, d), lambda i, j, tiles: (0, 0)),
            pl.BlockSpec((d, tn), lambda i, j, tiles: (0, tiles[j])),
            tab, tab, mat, mat, vec, vec,
        ],
        out_specs=pl.BlockSpec((tm, tn), lambda i, j, tiles: (i, j)),
        scratch_shapes=[pltpu.VMEM((tm, d), BF16)],
    )
    return pl.pallas_call(
        functools.partial(_inproj_kernel, rope=rope),
        grid_spec=grid_spec,
        out_shape=jax.ShapeDtypeStruct((m, len(col_tiles) * tn), BF16),
        compiler_params=_params("parallel", "arbitrary"),
        name="in_proj",
    )(jnp.asarray(col_tiles, jnp.int32), x2d, mod, norm_g.reshape(1, d), w_in, cos, sin,
      *_pair_matrices(), per_pair(q_norm_g), per_pair(k_norm_g))


HG_BLOCK = 256
CAST_ROWS = 16


def _cast_rows(src_ref, dst_ref):
    def body(r, carry):
        sl = pl.ds(pl.multiple_of(r * CAST_ROWS, CAST_ROWS), CAST_ROWS)
        dst_ref[sl, :] = src_ref[sl, :].astype(dst_ref.dtype)
        return carry

    lax.fori_loop(0, src_ref.shape[0] // CAST_ROWS, body, 0)


def _chunk_masks():
    r = np.arange(HG_BLOCK)[:, None]
    c = np.arange(HG_BLOCK)[None, :]
    same = (r // HG_CHUNK) == (c // HG_CHUNK)
    return (same & (r >= c)), (same & (r <= c)), same


def _hgrn_kernel(q_ref, ff_ref, fb_ref, i_ref, g_ref, cff_ref, cfb_ref, ci_ref, lb_ref, ng_ref,
                 cumf_ref, cumb_ref, dmask_ref, c_ref, wm_ref, bm_ref, *rest, n_cast):
    cast_src = rest[:n_cast]
    o_ref, mod_ref = rest[n_cast:n_cast + 2]
    cast_dst = rest[n_cast + 2:2 * n_cast + 2]
    of_scr, ob_scr = rest[2 * n_cast + 2:]

    mod_ref[...] = _dot(_silu(c_ref[...]).astype(BF16), wm_ref[...].astype(BF16)) + bm_ref[...]
    for src, dst in zip(cast_src, cast_dst):
        _cast_rows(src, dst)

    t_len = q_ref.shape[1]
    l_len = ci_ref.shape[1]
    c = HG_CHUNK
    blk = HG_BLOCK
    nch = blk // c
    n_blk = t_len // blk
    assert l_len == blk

    a = lb_ref[...]
    e = jnp.exp(a - jnp.max(a, axis=0, keepdims=True))
    lb = e[0] / jnp.sum(e, axis=0)

    def gates(f_pre, lb_d):
        f = 0.5 * (1.0 + lb_d) + (0.5 * (1.0 - lb_d)) * jnp.tanh(0.5 * f_pre.astype(F32))
        log_f = jnp.log2(f)
        hi = lax.bitcast_convert_type(lax.bitcast_convert_type(log_f, jnp.uint32) & jnp.uint32(0xFFFF0000), F32)
        return 1.0 - f, jnp.concatenate([hi.astype(BF16), (log_f - hi).astype(BF16)], axis=1)

    def decays(parts, cum_ref, reverse):
        b = _dot(cum_ref[...], parts)
        b = b[:, 0:128] + b[:, 128:256]
        b3 = b.reshape(nch, c, HG_DK)
        tot = b3[:, 0:1, :] if reverse else b3[:, c - 1:c, :]
        b_tot = jnp.broadcast_to(tot, (nch, c, HG_DK)).reshape(blk, HG_DK)
        return b, b_tot, jnp.exp2(tot)

    def increments(key, b, b_tot, v):
        k_dec = (key * jnp.exp2(b_tot - b)).astype(BF16)
        v_t = v.astype(BF16).T
        upd = []
        for h in range(blk // 128):
            kd = k_dec[h * 128:(h + 1) * 128, :]
            kd4 = jnp.concatenate([kd] * 4, axis=1) * dmask_ref[...]
            u4 = _dot(v_t[:, h * 128:(h + 1) * 128], kd4)
            upd += [u4[:, j * HG_DK:(j + 1) * HG_DK] for j in range(4)]
        return upd

    def scores(q_pre, key, b):
        q_dec = (_silu(q_pre.astype(F32)) * jnp.exp2(b)).astype(BF16)
        k_inv = (key * jnp.exp2(-b)).astype(BF16)
        return q_dec, lax.dot_general(q_dec, k_inv, _NT, preferred_element_type=F32)

    def intra(s, v, cum_ref):
        return _dot(s.astype(BF16) * cum_ref[...], v.astype(BF16))

    def recur_pair(st, chains):
        outs = [[None] * nch for _ in chains]
        st = list(st)
        for step in range(nch):
            for d, (dec, upd, q_dec, o_intra) in enumerate(chains):
                n = nch - 1 - step if d == 1 else step
                if q_dec is not None:
                    rows = slice(n * c, (n + 1) * c)
                    outs[d][n] = o_intra[rows, :] + _dot(q_dec[rows, :], st[d].T.astype(BF16))
                st[d] = st[d] * dec[n] + upd[n]
        return [jnp.concatenate(o, axis=0) if o[0] is not None else None for o in outs], st

    lb_d = (lb[0:1, :], lb[1:2, :])
    cum = (cumf_ref, cumb_ref)
    f_lat = (ff_ref, fb_ref)
    o_scr = (of_scr, ob_scr)

    cv = ci_ref[0]
    ctx_chains = []
    for d, f_ref in enumerate((cff_ref, cfb_ref)):
        key, parts = gates(f_ref[0], lb_d[d])
        b, b_tot, dec = decays(parts, cum[d], d == 1)
        ctx_chains.append((dec, increments(key, b, b_tot, cv), None, None))
    _, st = recur_pair([jnp.zeros((HG_DV, HG_DK), F32)] * 2, ctx_chains)

    blocks_per_step = 2

    def lat_body(r, carry):
        chains = []
        for u in range(blocks_per_step):
            up = r * blocks_per_step + u
            chains.append((0, pl.ds(pl.multiple_of(up * blk, blk), blk)))
            chains.append((1, pl.ds(pl.multiple_of((n_blk - 1 - up) * blk, blk), blk)))
        g = [gates(f_lat[d][0, sl, :], lb_d[d]) for d, sl in chains]
        dk = [decays(g[i][1], cum[d], d == 1) for i, (d, sl) in enumerate(chains)]
        v = [i_ref[0, sl, :] for d, sl in chains]
        upd = [increments(g[i][0], dk[i][0], dk[i][1], v[i]) for i in range(len(chains))]
        qs = [scores(q_ref[0, sl, :], g[i][0], dk[i][0]) for i, (d, sl) in enumerate(chains)]
        o_in = [intra(qs[i][1], v[i], cum[d]) for i, (d, sl) in enumerate(chains)]
        st = list(carry)
        for u in range(blocks_per_step):
            pair = range(2 * u, 2 * u + 2)
            outs, st = recur_pair(st, [(dk[i][2], upd[i], qs[i][0], o_in[i]) for i in pair])
            for i in pair:
                d, sl = chains[i]
                o_scr[d][sl, :] = outs[d]
        return tuple(st)

    lax.fori_loop(0, n_blk // blocks_per_step, lat_body, tuple(st))

    ng = ng_ref[...]

    def fin_body(r, carry):
        sl = pl.ds(pl.multiple_of(r * blk, blk), blk)
        o = of_scr[sl, :] + ob_scr[sl, :]
        y = o * lax.rsqrt(jnp.mean(o * o, axis=-1, keepdims=True) + RMS_EPS) * ng
        o_ref[0, sl, :] = (y * _silu(g_ref[0, sl, :].astype(F32))).astype(o_ref.dtype)
        return carry

    lax.fori_loop(0, n_blk, fin_body, 0, unroll=4)


def _hgrn_mixer(p_lat, p_ctx, ctx_tiles, hg_lb, norm_g, c_rows, w_mod, b_mod, mod_col0, cast_weights):
    bsz, t_len, _ = p_lat.shape
    l_len = p_ctx.shape[1]
    w = HG_DK
    d, n_mod = w_mod.shape
    n_steps = bsz * HG_HEADS
    mod_tn = (n_mod - mod_col0) // n_steps
    assert mod_tn % 128 == 0 and mod_col0 % mod_tn == 0 and mod_tn * n_steps == n_mod - mod_col0
    assert all(m.shape[0] % (n_steps * CAST_ROWS) == 0 for m in cast_weights)

    def slab(m):
        return pl.BlockSpec((m.shape[0] // n_steps, m.shape[1]), lambda b, h: (b * HG_HEADS + h, 0))

    def lat(group):
        return pl.BlockSpec((1, t_len, w), lambda b, h: (b, 0, group * HG_HEADS + h))

    def ctx(group):
        blk0 = _packed_col(ctx_tiles, group * HG_HEADS * w) // w
        return pl.BlockSpec((1, l_len, w), lambda b, h: (b, 0, blk0 + h))

    def const(shape):
        return pl.BlockSpec(shape, lambda b, h: (0,) * len(shape))

    lower, upper, same = _chunk_masks()
    diag4 = np.tile(same[:128, ::HG_CHUNK][:, :4, None], (1, 1, HG_DK)).reshape(128, 4 * HG_DK)
    seq_f32 = pltpu.VMEM((t_len, w), F32)
    return pl.pallas_call(
        functools.partial(_hgrn_kernel, n_cast=len(cast_weights)),
        grid=(bsz, HG_HEADS),
        in_specs=[
            lat(0), lat(1), lat(2), lat(3), lat(4),
            ctx(1), ctx(2), ctx(3),
            pl.BlockSpec((hg_lb.shape[0], 2, w), lambda b, h: (0, 0, h)),
            const((1, HG_DV)),
            const((HG_BLOCK, HG_BLOCK)), const((HG_BLOCK, HG_BLOCK)),
            const((128, 4 * HG_DK)),
            const((MOD_ROWS, d)),
            pl.BlockSpec((d, mod_tn), lambda b, h: (0, mod_col0 // mod_tn + b * HG_HEADS + h)),
            pl.BlockSpec((1, mod_tn), lambda b, h: (0, mod_col0 // mod_tn + b * HG_HEADS + h)),
        ] + [slab(m) for m in cast_weights],
        out_specs=[pl.BlockSpec((1, t_len, HG_DV), lambda b, h: (b, 0, h)),
                   pl.BlockSpec((MOD_ROWS, mod_tn), lambda b, h: (0, b * HG_HEADS + h))]
        + [slab(m) for m in cast_weights],
        out_shape=[jax.ShapeDtypeStruct((bsz, t_len, HG_WIDTH), BF16),
                   jax.ShapeDtypeStruct((MOD_ROWS, n_mod - mod_col0), F32)]
        + [jax.ShapeDtypeStruct(m.shape, BF16) for m in cast_weights],
        scratch_shapes=[seq_f32, seq_f32],
        compiler_params=_params("parallel", "parallel"),
        name="hgrn2_mixer",
    )(p_lat, p_lat, p_lat, p_lat, p_lat, p_ctx, p_ctx, p_ctx, hg_lb, norm_g.reshape(1, HG_DV),
      jnp.asarray(lower, BF16), jnp.asarray(upper, BF16), jnp.asarray(diag4, BF16),
      c_rows, w_mod, b_mod.reshape(1, n_mod), *cast_weights)


ATT_QROWS = 8 * ATT_BLOCK


def _attn_kernel(q_ref, kvp_ref, kvc_ref, kvn_ref, ckv_ref, sink_ref, o_ref):
    n = pl.program_id(1)
    n_steps = pl.num_programs(1)
    rows = q_ref.shape[1]
    blk = ATT_BLOCK
    hd = HEAD_DIM
    l_len = ckv_ref.shape[1]
    band = 3 * blk
    n_keys = band + l_len
    ones = jnp.ones((n_keys, hd), BF16)

    def band_rows(cols):
        return jnp.concatenate([kvp_ref[0, rows - blk:, cols], kvc_ref[0, :, cols], kvn_ref[0, :blk, cols]], axis=0)

    kv_slabs = [band_rows(slice(j * hd, (j + 1) * hd)) for j in range(2 * ATT_KV_HEADS)]
    ctx_slabs = [ckv_ref[0, :, j * hd:(j + 1) * hd] for j in range(2 * ATT_KV_HEADS)]
    qi = lax.broadcasted_iota(jnp.int32, (blk, n_keys), 0)
    kj = lax.broadcasted_iota(jnp.int32, (blk, n_keys), 1)
    in_window = jnp.abs(kj - blk - qi) <= WINDOW

    for sub in range(rows // blk):
        lo = sub * blk
        k_pos = n * rows + lo - blk + kj
        ok = (kj >= band) | (in_window & (k_pos >= 0) & (k_pos < n_steps * rows))
        bias = jnp.where(ok, 0.0, NEG_INF)
        k_all = [jnp.concatenate([kv_slabs[j][lo:lo + band], ctx_slabs[j]], axis=0) for j in range(ATT_KV_HEADS)]
        v_aug = [jnp.concatenate([jnp.concatenate([kv_slabs[ATT_KV_HEADS + j][lo:lo + band],
                                                   ctx_slabs[ATT_KV_HEADS + j]], axis=0), ones], axis=1)
                 for j in range(ATT_KV_HEADS)]
        heads = range(ATT_HEADS)
        s = [lax.dot_general(q_ref[0, lo:lo + blk, h * hd:(h + 1) * hd], k_all[h // ATT_GROUP], _NT,
                             preferred_element_type=F32) + bias for h in heads]
        sink = [sink_ref[:, h:h + 1] for h in heads]
        m = [jnp.maximum(jnp.max(s[h], axis=-1, keepdims=True), sink[h]) for h in heads]
        e = [jnp.exp(s[h] - m[h]).astype(BF16) for h in heads]
        oa = [_dot(e[h], v_aug[h // ATT_GROUP]) for h in heads]
        for h in heads:
            o = oa[h][:, :hd] / (oa[h][:, hd:] + jnp.exp(sink[h] - m[h]))
            o_ref[0, lo:lo + blk, h * hd:(h + 1) * hd] = o.astype(o_ref.dtype)


def _window_attention(p_lat, p_ctx, ctx_tiles, sink):
    bsz, t_len, _ = p_lat.shape
    l_len = p_ctx.shape[1]
    nb = t_len // ATT_QROWS
    kvw = 2 * ATT_KV_HEADS * HEAD_DIM
    q_blk = HG_COLS // ATT_WIDTH
    kv_blk = (HG_COLS + ATT_WIDTH) // kvw
    ckv_blk = _packed_col(ctx_tiles, HG_COLS + ATT_WIDTH) // kvw

    def prev(n):
        return jnp.maximum(n - 1, 0)

    def nxt(n):
        return jnp.minimum(n + 1, nb - 1)

    def kv_spec(shift):
        return pl.BlockSpec((1, ATT_QROWS, kvw), lambda b, n: (b, shift(n), kv_blk))

    same = lambda n: n
    return pl.pallas_call(
        _attn_kernel,
        grid=(bsz, nb),
        in_specs=[
            pl.BlockSpec((1, ATT_QROWS, ATT_WIDTH), lambda b, n: (b, n, q_blk)),
            kv_spec(prev), kv_spec(same), kv_spec(nxt),
            pl.BlockSpec((1, l_len, kvw), lambda b, n: (b, 0, ckv_blk)),
            pl.BlockSpec((1, ATT_HEADS), lambda b, n: (0, 0)),
        ],
        out_specs=pl.BlockSpec((1, ATT_QROWS, ATT_WIDTH), lambda b, n: (b, n, 0)),
        out_shape=jax.ShapeDtypeStruct((bsz, t_len, ATT_WIDTH), BF16),
        compiler_params=_params("parallel", "parallel"),
        name="window_gqa",
    )(p_lat, p_lat, p_lat, p_lat, p_ctx, sink.reshape(1, ATT_HEADS))


def _outproj_kernel(x_ref, mod_ref, hg_ref, at_ref, w_ref, o_ref):
    kh = hg_ref.shape[1]
    acc = _dot(hg_ref[...], w_ref[:kh, :]) + _dot(at_ref[...], w_ref[kh:, :])
    o_ref[...] = x_ref[...] + mod_ref[0, 2:3, :] * acc


def _out_projection(x2d, mod, hg2d, at2d, w_out, tokens_per_mod, tm=512):
    m, d = x2d.shape
    kh = hg2d.shape[1]
    tiles_per_mod = tokens_per_mod // tm
    return pl.pallas_call(
        _outproj_kernel,
        grid=(m // tm,),
        in_specs=[
            pl.BlockSpec((tm, d), lambda i: (i, 0)),
            pl.BlockSpec((1, N_MOD, d), lambda i: (i // tiles_per_mod, 0, 0)),
            pl.BlockSpec((tm, kh), lambda i: (i, 0)),
            pl.BlockSpec((tm, kh), lambda i: (i, 0)),
            pl.BlockSpec(w_out.shape, lambda i: (0, 0), pipeline_mode=pl.Buffered(1)),
        ],
        out_specs=pl.BlockSpec((tm, d), lambda i: (i, 0)),
        out_shape=jax.ShapeDtypeStruct((m, d), F32),
        compiler_params=_params("parallel"),
        name="out_proj",
    )(x2d, mod, hg2d, at2d, w_out)


def _ffn_kernel(x_ref, mod_ref, g_ref, wg_ref, wu_ref, wd_ref, o_ref, h_scr):
    j = pl.program_id(1)

    @pl.when(j == 0)
    def _():
        _norm_modulate_into(x_ref, g_ref, mod_ref[0, 3:4, :], mod_ref[0, 4:5, :], h_scr)

    def gated_down():
        h = h_scr[...]
        gate = _dot(h, wg_ref[...])
        up = _dot(h, wu_ref[...])
        act = (_silu(gate) * up).astype(BF16)
        return mod_ref[0, 5:6, :] * _dot(act, wd_ref[...])

    @pl.when(j == 0)
    def _():
        o_ref[...] = x_ref[...] + gated_down()

    @pl.when(j > 0)
    def _():
        o_ref[...] += gated_down()


def _ffn(x2d, mod, norm_g, w_gate_up, w_down, tokens_per_mod, tm=1024, tf=512):
    m, d = x2d.shape
    d_ff = w_down.shape[0]
    nf = d_ff // tf
    tiles_per_mod = tokens_per_mod // tm
    return pl.pallas_call(
        _ffn_kernel,
        grid=(m // tm, nf),
        in_specs=[
            pl.BlockSpec((tm, d), lambda i, j: (i, 0)),
            pl.BlockSpec((1, N_MOD, d), lambda i, j: (i // tiles_per_mod, 0, 0)),
            pl.BlockSpec((1, d), lambda i, j: (0, 0)),
            pl.BlockSpec((d, tf), lambda i, j: (0, j)),
            pl.BlockSpec((d, tf), lambda i, j: (0, nf + j)),
            pl.BlockSpec((tf, d), lambda i, j: (j, 0)),
        ],
        out_specs=pl.BlockSpec((tm, d), lambda i, j: (i, 0)),
        out_shape=jax.ShapeDtypeStruct((m, d), F32),
        scratch_shapes=[pltpu.VMEM((tm, d), BF16)],
        compiler_params=_params("parallel", "arbitrary"),
        name="swiglu_ffn",
    )(x2d, mod, norm_g.reshape(1, d), w_gate_up, w_gate_up, w_down)


def kernel(x, c, ctx, c_ctx, w_mod, b_mod, norm_mix_g, norm_ffn_g, w_in, hg_lb, hg_norm_g,
           q_norm_g, k_norm_g, attn_sink, w_out, w_gate_up, w_down):
    bsz, t_len, d = x.shape
    l_len = ctx.shape[1]
    depth = w_mod.shape[0]
    assert depth == 1, "single-layer block: the context stream never needs its own outputs"
    assert bsz + 1 <= MOD_ROWS

    c_rows = jnp.concatenate([c, c_ctx[None, :], jnp.zeros((MOD_ROWS - bsz - 1, d), F32)], axis=0)
    n_early = 2 * d
    mod_early = _modulation(c_rows, w_mod[0], b_mod[0], n_early)
    mod_mix = mod_early.reshape(MOD_ROWS, 2, d)

    cos, sin = _rope_tables(t_len)
    lat_tiles = _tiles_covering([(0, IN_COLS)])
    hk = HG_HEADS * HG_DK
    ctx_tiles = _tiles_covering([(hk, 3 * hk + HG_WIDTH), (HG_COLS + ATT_WIDTH, IN_COLS)])
    p_lat = _in_projection(x.reshape(bsz * t_len, d), mod_mix[:bsz], norm_mix_g[0], w_in[0], q_norm_g[0],
                           k_norm_g[0], cos, sin, tokens_per_mod=t_len, tm=2048, rope=True, col_tiles=lat_tiles,
                           x_buffers=1)
    p_ctx = _in_projection(ctx.reshape(bsz * l_len, d), mod_mix[bsz:bsz + 1], norm_mix_g[0], w_in[0], q_norm_g[0],
                           k_norm_g[0], cos, sin, tokens_per_mod=bsz * l_len, tm=bsz * l_len, rope=False,
                           col_tiles=ctx_tiles)
    p_lat = p_lat.reshape(bsz, t_len, -1)
    p_ctx = p_ctx.reshape(bsz, l_len, -1)

    hg, mod_late, w_out16, w_gate_up16, w_down16 = _hgrn_mixer(
        p_lat, p_ctx, ctx_tiles, hg_lb, hg_norm_g[0], c_rows, w_mod[0], b_mod[0], n_early,
        cast_weights=(w_out[0], w_gate_up[0], w_down[0]))
    mod_lat = jnp.concatenate([mod_early, mod_late], axis=1).reshape(MOD_ROWS, N_MOD, d)[:bsz]
    at = _window_attention(p_lat, p_ctx, ctx_tiles, attn_sink[0])

    x1 = _out_projection(x.reshape(bsz * t_len, d), mod_lat, hg.reshape(bsz * t_len, HG_WIDTH),
                         at.reshape(bsz * t_len, ATT_WIDTH), w_out16, tokens_per_mod=t_len)
    x2 = _ffn(x1, mod_lat, norm_ffn_g[0], w_gate_up16, w_down16, tokens_per_mod=t_len)
    return x2.reshape(bsz, t_len, d)
```

```python
import functools

import jax
import jax.numpy as jnp
import numpy as np
from jax import lax
from jax.experimental import pallas as pl
from jax.experimental.pallas import tpu as pltpu

F32 = jnp.float32
BF16 = jnp.bfloat16

GRID_W = 64
HG_HEADS = 8
HG_DK = 128
HG_DV = 128
HG_CHUNK = 32
ATT_HEADS = 8
ATT_KV_HEADS = 2
ATT_GROUP = ATT_HEADS // ATT_KV_HEADS
HEAD_DIM = 128
WINDOW = 128
ATT_BLOCK = 128
ROPE_THETA = 10000.0
N_MOD = 6
RMS_EPS = 1e-6
NEG_INF = -1e30
HG_WIDTH = HG_HEADS * HG_DV
ATT_WIDTH = ATT_HEADS * HEAD_DIM
HG_COLS = 3 * HG_HEADS * HG_DK + 2 * HG_HEADS * HG_DV
ATT_COLS = (ATT_HEADS + 2 * ATT_KV_HEADS) * HEAD_DIM
IN_COLS = HG_COLS + ATT_COLS

VMEM_LIMIT_BYTES = 58 * 1024 * 1024
MOD_ROWS = 8

_NT = (((1,), (1,)), ((), ()))


def _dot(a, b):
    return jnp.dot(a, b, preferred_element_type=F32)


def _silu(x):
    h = 0.5 * x
    return h * jnp.tanh(h) + h


def _params(*sem):
    return pltpu.CompilerParams(dimension_semantics=sem, vmem_limit_bytes=VMEM_LIMIT_BYTES)


def _mod_kernel(c_ref, w_ref, b_ref, o_ref):
    a = _silu(c_ref[...]).astype(BF16)
    o_ref[...] = _dot(a, w_ref[...].astype(BF16)) + b_ref[...]


def _modulation(c_rows, w_mod, b_mod, n_cols, tn=1024):
    d, n = w_mod.shape
    return pl.pallas_call(
        _mod_kernel,
        grid=(n_cols // tn,),
        in_specs=[
            pl.BlockSpec((MOD_ROWS, d), lambda j: (0, 0)),
            pl.BlockSpec((d, tn), lambda j: (0, j)),
            pl.BlockSpec((1, tn), lambda j: (0, j)),
        ],
        out_specs=pl.BlockSpec((MOD_ROWS, tn), lambda j: (0, j)),
        out_shape=jax.ShapeDtypeStruct((MOD_ROWS, n_cols), F32),
        compiler_params=_params("arbitrary"),
        name="adaln_mod",
    )(c_rows, w_mod, b_mod.reshape(1, n))


HEAD_PAIR = 2 * HEAD_DIM


def _rope_tables(t_len):
    half = HEAD_DIM // 2
    nf = half // 2
    pos = np.arange(t_len)
    inv_freq = ROPE_THETA ** (-np.arange(nf, dtype=np.float64) / nf)
    ang_r = (pos // GRID_W)[:, None] * inv_freq
    ang_c = (pos % GRID_W)[:, None] * inv_freq
    cos = np.concatenate([np.cos(ang_r)] * 2 + [np.cos(ang_c)] * 2, axis=-1)
    sin = np.concatenate([-np.sin(ang_r), np.sin(ang_r), -np.sin(ang_c), np.sin(ang_c)], axis=-1)
    pair = lambda a: jnp.asarray(np.tile(a, (1, HEAD_PAIR // HEAD_DIM)), F32)
    return pair(cos), pair(sin)


def _pair_matrices():
    quarter = HEAD_DIM // 4
    lane = np.arange(HEAD_PAIR)
    src = np.where(lane % (2 * quarter) < quarter, lane + quarter, lane - quarter)
    swap = lane[:, None] == src[None, :]
    head_mean = ((lane[:, None] // HEAD_DIM) == (lane[None, :] // HEAD_DIM)) / HEAD_DIM
    return jnp.asarray(swap, BF16), jnp.asarray(head_mean, BF16)


def _norm_modulate_into(x_ref, g_ref, shift, scale, h_scr, rows=128):
    tm = x_ref.shape[0]
    gain = g_ref[...] * (1.0 + scale)

    def body(r, carry):
        sl = pl.ds(pl.multiple_of(r * rows, rows), rows)
        x = x_ref[sl, :]
        ms = jnp.mean(x * x, axis=-1, keepdims=True)
        h_scr[sl, :] = (x * lax.rsqrt(ms + RMS_EPS) * gain + shift).astype(BF16)
        return carry

    lax.fori_loop(0, tm // rows, body, 0, unroll=2)


def _inproj_kernel(tiles_ref, x_hbm, mod_ref, g_ref, w_ref, cos_ref, sin_ref, swap_ref, hmean_ref, qg_ref, kg_ref,
                   o_ref, x_buf, x_sem, h_scr, *, rope):
    i = pl.program_id(0)
    j = pl.program_id(1)
    tm = x_buf.shape[0]
    t = tiles_ref[j]
    tn = o_ref.shape[1]
    hd = HEAD_DIM
    q_tile0 = HG_COLS // tn
    kv_tile = (HG_COLS + ATT_WIDTH) // tn

    def x_copy(row_tile):
        return pltpu.make_async_copy(x_hbm.at[pl.ds(pl.multiple_of(row_tile * tm, tm), tm), :], x_buf, x_sem)

    @pl.when(j == 0)
    def _():
        @pl.when(i == 0)
        def _():
            x_copy(i).start()

        x_copy(i).wait()
        _norm_modulate_into(x_buf, g_ref, mod_ref[0, 0:1, :], mod_ref[0, 1:2, :], h_scr)

    @pl.when((j == 1) & (i + 1 < pl.num_programs(0)))
    def _():
        x_copy(i + 1).start()

    def project():
        return _dot(h_scr[...], w_ref[...].astype(BF16))

    def qk_prep(acc, n_heads, g_row, scale):
        cols = [slice(p * HEAD_PAIR, (p + 1) * HEAD_PAIR) for p in range(n_heads * hd // HEAD_PAIR)]
        x = [acc[:, c] for c in cols]
        gain = g_row * scale
        ms = [_dot((xh * xh).astype(BF16), hmean_ref[...]) for xh in x]
        y = [xh * lax.rsqrt(m + RMS_EPS) * gain for xh, m in zip(x, ms)]
        if rope:
            swapped = [_dot(yh.astype(BF16), swap_ref[...]) for yh in y]
            y = [yh * cos_ref[...] + sh * sin_ref[...] for yh, sh in zip(y, swapped)]
        for c, yh in zip(cols, y):
            o_ref[:, c] = yh.astype(o_ref.dtype)

    @pl.when(t < q_tile0)
    def _():
        o_ref[...] = project().astype(o_ref.dtype)

    @pl.when((t >= q_tile0) & (t < kv_tile))
    def _():
        qk_prep(project(), tn // hd, qg_ref[...], hd ** -0.5)

    @pl.when(t == kv_tile)
    def _():
        acc = project()
        qk_prep(acc, ATT_KV_HEADS, kg_ref[...], 1.0)
        o_ref[:, ATT_KV_HEADS * hd:] = acc[:, ATT_KV_HEADS * hd:].astype(o_ref.dtype)


IN_TILE = 512


def _tiles_covering(col_ranges):
    return tuple(t for t in range(IN_COLS // IN_TILE)
                 if any(lo < (t + 1) * IN_TILE and t * IN_TILE < hi for lo, hi in col_ranges))


def _packed_col(col_tiles, col):
    return col_tiles.index(col // IN_TILE) * IN_TILE + col % IN_TILE


def _in_projection(x2d, mod, norm_g, w_in, q_norm_g, k_norm_g, cos, sin, tokens_per_mod, tm, rope, col_tiles):
    m, d = x2d.shape
    tn = IN_TILE
    assert HG_COLS % tn == 0 and ATT_WIDTH % tn == 0 and 2 * ATT_KV_HEADS * HEAD_DIM == tn
    assert m % tm == 0 and len(col_tiles) >= 2
    tiles_per_mod = tokens_per_mod // tm
    tiles_per_seq = cos.shape[0] // tm
    vec = pl.BlockSpec((1, HEAD_PAIR), lambda i, j, tiles: (0, 0))
    tab = pl.BlockSpec((tm, HEAD_PAIR), lambda i, j, tiles: (i % tiles_per_seq, 0))
    mat = pl.BlockSpec((HEAD_PAIR, HEAD_PAIR), lambda i, j, tiles: (0, 0))
    per_pair = lambda g: jnp.tile(g.reshape(1, HEAD_DIM), (1, HEAD_PAIR // HEAD_DIM))
    grid_spec = pltpu.PrefetchScalarGridSpec(
        num_scalar_prefetch=1,
        grid=(m // tm, len(col_tiles)),
        in_specs=[
            pl.BlockSpec(memory_space=pl.ANY),
            pl.BlockSpec((1, mod.shape[1], d), lambda i, j, tiles: (i // tiles_per_mod, 0, 0)),
            pl.BlockSpec((1, d), lambda i, j, tiles: (0, 0)),
            pl.BlockSpec((d, tn), lambda i, j, tiles: (0, tiles[j])),
            tab, tab, mat, mat, vec, vec,
        ],
        out_specs=pl.BlockSpec((tm, tn), lambda i, j, tiles: (i, j)),
        scratch_shapes=[pltpu.VMEM((tm, d), jnp.float32), pltpu.SemaphoreType.DMA(()), pltpu.VMEM((tm, d), BF16)],
    )
    return pl.pallas_call(
        functools.partial(_inproj_kernel, rope=rope),
        grid_spec=grid_spec,
        out_shape=jax.ShapeDtypeStruct((m, len(col_tiles) * tn), BF16),
        compiler_params=_params("arbitrary", "arbitrary"),
        name="in_proj",
    )(jnp.asarray(col_tiles, jnp.int32), x2d, mod, norm_g.reshape(1, d), w_in, cos, sin,
      *_pair_matrices(), per_pair(q_norm_g), per_pair(k_norm_g))


HG_BLOCK = 256
CAST_ROWS = 16


def _cast_rows(src_ref, dst_ref):
    def body(r, carry):
        sl = pl.ds(pl.multiple_of(r * CAST_ROWS, CAST_ROWS), CAST_ROWS)
        dst_ref[sl, :] = src_ref[sl, :].astype(dst_ref.dtype)
        return carry

    lax.fori_loop(0, src_ref.shape[0] // CAST_ROWS, body, 0)


def _chunk_masks():
    r = np.arange(HG_BLOCK)[:, None]
    c = np.arange(HG_BLOCK)[None, :]
    same = (r // HG_CHUNK) == (c // HG_CHUNK)
    return (same & (r >= c)), (same & (r <= c)), same


def _hgrn_kernel(q_ref, ff_ref, fb_ref, i_ref, g_ref, cff_ref, cfb_ref, ci_ref, lb_ref, ng_ref,
                 cumf_ref, cumb_ref, dmask_ref, c_ref, wm_ref, bm_ref, *rest, n_cast):
    cast_src = rest[:n_cast]
    o_ref, mod_ref = rest[n_cast:n_cast + 2]
    cast_dst = rest[n_cast + 2:2 * n_cast + 2]
    of_scr, ob_scr = rest[2 * n_cast + 2:]

    mod_ref[...] = _dot(_silu(c_ref[...]).astype(BF16), wm_ref[...].astype(BF16)) + bm_ref[...]
    for src, dst in zip(cast_src, cast_dst):
        _cast_rows(src, dst)

    t_len = q_ref.shape[1]
    l_len = ci_ref.shape[1]
    c = HG_CHUNK
    blk = HG_BLOCK
    nch = blk // c
    n_blk = t_len // blk
    assert l_len == blk

    a = lb_ref[...]
    e = jnp.exp(a - jnp.max(a, axis=0, keepdims=True))
    lb = e[0] / jnp.sum(e, axis=0)

    def gates(f_pre, lb_d):
        f = 0.5 * (1.0 + lb_d) + (0.5 * (1.0 - lb_d)) * jnp.tanh(0.5 * f_pre.astype(F32))
        log_f = jnp.log2(f)
        hi = lax.bitcast_convert_type(lax.bitcast_convert_type(log_f, jnp.uint32) & jnp.uint32(0xFFFF0000), F32)
        return 1.0 - f, jnp.concatenate([hi.astype(BF16), (log_f - hi).astype(BF16)], axis=1)

    def decays(parts, cum_ref, reverse):
        b = _dot(cum_ref[...], parts)
        b = b[:, 0:128] + b[:, 128:256]
        b3 = b.reshape(nch, c, HG_DK)
        tot = b3[:, 0:1, :] if reverse else b3[:, c - 1:c, :]
        b_tot = jnp.broadcast_to(tot, (nch, c, HG_DK)).reshape(blk, HG_DK)
        return b, b_tot, jnp.exp2(tot)

    def increments(key, b, b_tot, v):
        k_dec = (key * jnp.exp2(b_tot - b)).astype(BF16)
        v_t = v.astype(BF16).T
        upd = []
        for h in range(blk // 128):
            kd = k_dec[h * 128:(h + 1) * 128, :]
            kd4 = jnp.concatenate([kd] * 4, axis=1) * dmask_ref[...]
            u4 = _dot(v_t[:, h * 128:(h + 1) * 128], kd4)
            upd += [u4[:, j * HG_DK:(j + 1) * HG_DK] for j in range(4)]
        return upd

    def scores(q_pre, key, b):
        q_dec = (_silu(q_pre.astype(F32)) * jnp.exp2(b)).astype(BF16)
        k_inv = (key * jnp.exp2(-b)).astype(BF16)
        return q_dec, lax.dot_general(q_dec, k_inv, _NT, preferred_element_type=F32)

    def intra(s, v, cum_ref):
        return _dot(s.astype(BF16) * cum_ref[...], v.astype(BF16))

    def recur_pair(st, chains):
        outs = [[None] * nch for _ in chains]
        st = list(st)
        for step in range(nch):
            for d, (dec, upd, q_dec, o_intra) in enumerate(chains):
                n = nch - 1 - step if d == 1 else step
                if q_dec is not None:
                    rows = slice(n * c, (n + 1) * c)
                    outs[d][n] = o_intra[rows, :] + _dot(q_dec[rows, :], st[d].T.astype(BF16))
                st[d] = st[d] * dec[n] + upd[n]
        return [jnp.concatenate(o, axis=0) if o[0] is not None else None for o in outs], st

    lb_d = (lb[0:1, :], lb[1:2, :])
    cum = (cumf_ref, cumb_ref)
    f_lat = (ff_ref, fb_ref)
    o_scr = (of_scr, ob_scr)

    cv = ci_ref[0]
    ctx_chains = []
    for d, f_ref in enumerate((cff_ref, cfb_ref)):
        key, parts = gates(f_ref[0], lb_d[d])
        b, b_tot, dec = decays(parts, cum[d], d == 1)
        ctx_chains.append((dec, increments(key, b, b_tot, cv), None, None))
    _, st = recur_pair([jnp.zeros((HG_DV, HG_DK), F32)] * 2, ctx_chains)

    blocks_per_step = 2

    def lat_body(r, carry):
        chains = []
        for u in range(blocks_per_step):
            up = r * blocks_per_step + u
            chains.append((0, pl.ds(pl.multiple_of(up * blk, blk), blk)))
            chains.append((1, pl.ds(pl.multiple_of((n_blk - 1 - up) * blk, blk), blk)))
        g = [gates(f_lat[d][0, sl, :], lb_d[d]) for d, sl in chains]
        dk = [decays(g[i][1], cum[d], d == 1) for i, (d, sl) in enumerate(chains)]
        v = [i_ref[0, sl, :] for d, sl in chains]
        upd = [increments(g[i][0], dk[i][0], dk[i][1], v[i]) for i in range(len(chains))]
        qs = [scores(q_ref[0, sl, :], g[i][0], dk[i][0]) for i, (d, sl) in enumerate(chains)]
        o_in = [intra(qs[i][1], v[i], cum[d]) for i, (d, sl) in enumerate(chains)]
        st = list(carry)
        for u in range(blocks_per_step):
            pair = range(2 * u, 2 * u + 2)
            outs, st = recur_pair(st, [(dk[i][2], upd[i], qs[i][0], o_in[i]) for i in pair])
            for i in pair:
                d, sl = chains[i]
                o_scr[d][sl, :] = outs[d]
        return tuple(st)

    lax.fori_loop(0, n_blk // blocks_per_step, lat_body, tuple(st))

    ng = ng_ref[...]

    def fin_body(r, carry):
        sl = pl.ds(pl.multiple_of(r * blk, blk), blk)
        o = of_scr[sl, :] + ob_scr[sl, :]
        y = o * lax.rsqrt(jnp.mean(o * o, axis=-1, keepdims=True) + RMS_EPS) * ng
        o_ref[0, sl, :] = (y * _silu(g_ref[0, sl, :].astype(F32))).astype(o_ref.dtype)
        return carry

    lax.fori_loop(0, n_blk, fin_body, 0, unroll=4)


def _hgrn_mixer(p_lat, p_ctx, ctx_tiles, hg_lb, norm_g, c_rows, w_mod, b_mod, mod_col0, cast_weights):
    bsz, t_len, _ = p_lat.shape
    l_len = p_ctx.shape[1]
    w = HG_DK
    d, n_mod = w_mod.shape
    n_steps = bsz * HG_HEADS
    mod_tn = (n_mod - mod_col0) // n_steps
    assert mod_tn % 128 == 0 and mod_col0 % mod_tn == 0 and mod_tn * n_steps == n_mod - mod_col0
    assert all(m.shape[0] % (n_steps * CAST_ROWS) == 0 for m in cast_weights)

    def slab(m):
        return pl.BlockSpec((m.shape[0] // n_steps, m.shape[1]), lambda b, h: (b * HG_HEADS + h, 0))

    def lat(group):
        return pl.BlockSpec((1, t_len, w), lambda b, h: (b, 0, group * HG_HEADS + h))

    def ctx(group):
        blk0 = _packed_col(ctx_tiles, group * HG_HEADS * w) // w
        return pl.BlockSpec((1, l_len, w), lambda b, h: (b, 0, blk0 + h))

    def const(shape):
        return pl.BlockSpec(shape, lambda b, h: (0,) * len(shape))

    lower, upper, same = _chunk_masks()
    diag4 = np.tile(same[:128, ::HG_CHUNK][:, :4, None], (1, 1, HG_DK)).reshape(128, 4 * HG_DK)
    seq_f32 = pltpu.VMEM((t_len, w), F32)
    return pl.pallas_call(
        functools.partial(_hgrn_kernel, n_cast=len(cast_weights)),
        grid=(bsz, HG_HEADS),
        in_specs=[
            lat(0), lat(1), lat(2), lat(3), lat(4),
            ctx(1), ctx(2), ctx(3),
            pl.BlockSpec((hg_lb.shape[0], 2, w), lambda b, h: (0, 0, h)),
            const((1, HG_DV)),
            const((HG_BLOCK, HG_BLOCK)), const((HG_BLOCK, HG_BLOCK)),
            const((128, 4 * HG_DK)),
            const((MOD_ROWS, d)),
            pl.BlockSpec((d, mod_tn), lambda b, h: (0, mod_col0 // mod_tn + b * HG_HEADS + h)),
            pl.BlockSpec((1, mod_tn), lambda b, h: (0, mod_col0 // mod_tn + b * HG_HEADS + h)),
        ] + [slab(m) for m in cast_weights],
        out_specs=[pl.BlockSpec((1, t_len, HG_DV), lambda b, h: (b, 0, h)),
                   pl.BlockSpec((MOD_ROWS, mod_tn), lambda b, h: (0, b * HG_HEADS + h))]
        + [slab(m) for m in cast_weights],
        out_shape=[jax.ShapeDtypeStruct((bsz, t_len, HG_WIDTH), BF16),
                   jax.ShapeDtypeStruct((MOD_ROWS, n_mod - mod_col0), F32)]
        + [jax.ShapeDtypeStruct(m.shape, BF16) for m in cast_weights],
        scratch_shapes=[seq_f32, seq_f32],
        compiler_params=_params("parallel", "parallel"),
        name="hgrn2_mixer",
    )(p_lat, p_lat, p_lat, p_lat, p_lat, p_ctx, p_ctx, p_ctx, hg_lb, norm_g.reshape(1, HG_DV),
      jnp.asarray(lower, BF16), jnp.asarray(upper, BF16), jnp.asarray(diag4, BF16),
      c_rows, w_mod, b_mod.reshape(1, n_mod), *cast_weights)


ATT_QROWS = 8 * ATT_BLOCK


def _attn_kernel(q_ref, kvp_ref, kvc_ref, kvn_ref, ckv_ref, sink_ref, o_ref):
    n = pl.program_id(1)
    n_steps = pl.num_programs(1)
    rows = q_ref.shape[1]
    blk = ATT_BLOCK
    hd = HEAD_DIM
    l_len = ckv_ref.shape[1]
    band = 3 * blk
    n_keys = band + l_len
    ones = jnp.ones((n_keys, hd), BF16)

    def band_rows(cols):
        return jnp.concatenate([kvp_ref[0, rows - blk:, cols], kvc_ref[0, :, cols], kvn_ref[0, :blk, cols]], axis=0)

    kv_slabs = [band_rows(slice(j * hd, (j + 1) * hd)) for j in range(2 * ATT_KV_HEADS)]
    ctx_slabs = [ckv_ref[0, :, j * hd:(j + 1) * hd] for j in range(2 * ATT_KV_HEADS)]
    qi = lax.broadcasted_iota(jnp.int32, (blk, n_keys), 0)
    kj = lax.broadcasted_iota(jnp.int32, (blk, n_keys), 1)
    in_window = jnp.abs(kj - blk - qi) <= WINDOW

    for sub in range(rows // blk):
        lo = sub * blk
        k_pos = n * rows + lo - blk + kj
        ok = (kj >= band) | (in_window & (k_pos >= 0) & (k_pos < n_steps * rows))
        bias = jnp.where(ok, 0.0, NEG_INF)
        k_all = [jnp.concatenate([kv_slabs[j][lo:lo + band], ctx_slabs[j]], axis=0) for j in range(ATT_KV_HEADS)]
        v_aug = [jnp.concatenate([jnp.concatenate([kv_slabs[ATT_KV_HEADS + j][lo:lo + band],
                                                   ctx_slabs[ATT_KV_HEADS + j]], axis=0), ones], axis=1)
                 for j in range(ATT_KV_HEADS)]
        heads = range(ATT_HEADS)
        s = [lax.dot_general(q_ref[0, lo:lo + blk, h * hd:(h + 1) * hd], k_all[h // ATT_GROUP], _NT,
                             preferred_element_type=F32) + bias for h in heads]
        sink = [sink_ref[:, h:h + 1] for h in heads]
        m = [jnp.maximum(jnp.max(s[h], axis=-1, keepdims=True), sink[h]) for h in heads]
        e = [jnp.exp(s[h] - m[h]).astype(BF16) for h in heads]
        oa = [_dot(e[h], v_aug[h // ATT_GROUP]) for h in heads]
        for h in heads:
            o = oa[h][:, :hd] / (oa[h][:, hd:] + jnp.exp(sink[h] - m[h]))
            o_ref[0, lo:lo + blk, h * hd:(h + 1) * hd] = o.astype(o_ref.dtype)


def _window_attention(p_lat, p_ctx, ctx_tiles, sink):
    bsz, t_len, _ = p_lat.shape
    l_len = p_ctx.shape[1]
    nb = t_len // ATT_QROWS
    kvw = 2 * ATT_KV_HEADS * HEAD_DIM
    q_blk = HG_COLS // ATT_WIDTH
    kv_blk = (HG_COLS + ATT_WIDTH) // kvw
    ckv_blk = _packed_col(ctx_tiles, HG_COLS + ATT_WIDTH) // kvw

    def prev(n):
        return jnp.maximum(n - 1, 0)

    def nxt(n):
        return jnp.minimum(n + 1, nb - 1)

    def kv_spec(shift):
        return pl.BlockSpec((1, ATT_QROWS, kvw), lambda b, n: (b, shift(n), kv_blk))

    same = lambda n: n
    return pl.pallas_call(
        _attn_kernel,
        grid=(bsz, nb),
        in_specs=[
            pl.BlockSpec((1, ATT_QROWS, ATT_WIDTH), lambda b, n: (b, n, q_blk)),
            kv_spec(prev), kv_spec(same), kv_spec(nxt),
            pl.BlockSpec((1, l_len, kvw), lambda b, n: (b, 0, ckv_blk)),
            pl.BlockSpec((1, ATT_HEADS), lambda b, n: (0, 0)),
        ],
        out_specs=pl.BlockSpec((1, ATT_QROWS, ATT_WIDTH), lambda b, n: (b, n, 0)),
        out_shape=jax.ShapeDtypeStruct((bsz, t_len, ATT_WIDTH), BF16),
        compiler_params=_params("parallel", "parallel"),
        name="window_gqa",
    )(p_lat, p_lat, p_lat, p_lat, p_ctx, sink.reshape(1, ATT_HEADS))


def _outproj_kernel(x_ref, mod_ref, hg_ref, at_ref, w_ref, o_ref):
    kh = hg_ref.shape[1]
    acc = _dot(hg_ref[...], w_ref[:kh, :]) + _dot(at_ref[...], w_ref[kh:, :])
    o_ref[...] = x_ref[...] + mod_ref[0, 2:3, :] * acc


def _out_projection(x2d, mod, hg2d, at2d, w_out, tokens_per_mod, tm=512):
    m, d = x2d.shape
    kh = hg2d.shape[1]
    tiles_per_mod = tokens_per_mod // tm
    return pl.pallas_call(
        _outproj_kernel,
        grid=(m // tm,),
        in_specs=[
            pl.BlockSpec((tm, d), lambda i: (i, 0)),
            pl.BlockSpec((1, N_MOD, d), lambda i: (i // tiles_per_mod, 0, 0)),
            pl.BlockSpec((tm, kh), lambda i: (i, 0)),
            pl.BlockSpec((tm, kh), lambda i: (i, 0)),
            pl.BlockSpec(w_out.shape, lambda i: (0, 0), pipeline_mode=pl.Buffered(1)),
        ],
        out_specs=pl.BlockSpec((tm, d), lambda i: (i, 0)),
        out_shape=jax.ShapeDtypeStruct((m, d), F32),
        compiler_params=_params("parallel"),
        name="out_proj",
    )(x2d, mod, hg2d, at2d, w_out)


def _ffn_kernel(x_ref, mod_ref, g_ref, wg_ref, wu_ref, wd_ref, o_ref, h_scr):
    j = pl.program_id(1)

    @pl.when(j == 0)
    def _():
        _norm_modulate_into(x_ref, g_ref, mod_ref[0, 3:4, :], mod_ref[0, 4:5, :], h_scr)

    def gated_down():
        h = h_scr[...]
        gate = _dot(h, wg_ref[...])
        up = _dot(h, wu_ref[...])
        act = (_silu(gate) * up).astype(BF16)
        return mod_ref[0, 5:6, :] * _dot(act, wd_ref[...])

    @pl.when(j == 0)
    def _():
        o_ref[...] = x_ref[...] + gated_down()

    @pl.when(j > 0)
    def _():
        o_ref[...] += gated_down()


def _ffn(x2d, mod, norm_g, w_gate_up, w_down, tokens_per_mod, tm=1024, tf=512):
    m, d = x2d.shape
    d_ff = w_down.shape[0]
    nf = d_ff // tf
    tiles_per_mod = tokens_per_mod // tm
    return pl.pallas_call(
        _ffn_kernel,
        grid=(m // tm, nf),
        in_specs=[
            pl.BlockSpec((tm, d), lambda i, j: (i, 0)),
            pl.BlockSpec((1, N_MOD, d), lambda i, j: (i // tiles_per_mod, 0, 0)),
            pl.BlockSpec((1, d), lambda i, j: (0, 0)),
            pl.BlockSpec((d, tf), lambda i, j: (0, j)),
            pl.BlockSpec((d, tf), lambda i, j: (0, nf + j)),
            pl.BlockSpec((tf, d), lambda i, j: (j, 0)),
        ],
        out_specs=pl.BlockSpec((tm, d), lambda i, j: (i, 0)),
        out_shape=jax.ShapeDtypeStruct((m, d), F32),
        scratch_shapes=[pltpu.VMEM((tm, d), BF16)],
        compiler_params=_params("parallel", "arbitrary"),
        name="swiglu_ffn",
    )(x2d, mod, norm_g.reshape(1, d), w_gate_up, w_gate_up, w_down)


def kernel(x, c, ctx, c_ctx, w_mod, b_mod, norm_mix_g, norm_ffn_g, w_in, hg_lb, hg_norm_g,
           q_norm_g, k_norm_g, attn_sink, w_out, w_gate_up, w_down):
    bsz, t_len, d = x.shape
    l_len = ctx.shape[1]
    depth = w_mod.shape[0]
    assert depth == 1, "single-layer block: the context stream never needs its own outputs"
    assert bsz + 1 <= MOD_ROWS

    c_rows = jnp.concatenate([c, c_ctx[None, :], jnp.zeros((MOD_ROWS - bsz - 1, d), F32)], axis=0)
    n_early = 2 * d
    mod_early = _modulation(c_rows, w_mod[0], b_mod[0], n_early)
    mod_mix = mod_early.reshape(MOD_ROWS, 2, d)

    cos, sin = _rope_tables(t_len)
    lat_tiles = _tiles_covering([(0, IN_COLS)])
    hk = HG_HEADS * HG_DK
    ctx_tiles = _tiles_covering([(hk, 3 * hk + HG_WIDTH), (HG_COLS + ATT_WIDTH, IN_COLS)])
    p_lat = _in_projection(x.reshape(bsz * t_len, d), mod_mix[:bsz], norm_mix_g[0], w_in[0], q_norm_g[0],
                           k_norm_g[0], cos, sin, tokens_per_mod=t_len, tm=2048, rope=True, col_tiles=lat_tiles)
    p_ctx = _in_projection(ctx.reshape(bsz * l_len, d), mod_mix[bsz:bsz + 1], norm_mix_g[0], w_in[0], q_norm_g[0],
                           k_norm_g[0], cos, sin, tokens_per_mod=bsz * l_len, tm=bsz * l_len, rope=False,
                           col_tiles=ctx_tiles)
    p_lat = p_lat.reshape(bsz, t_len, -1)
    p_ctx = p_ctx.reshape(bsz, l_len, -1)

    hg, mod_late, w_out16, w_gate_up16, w_down16 = _hgrn_mixer(
        p_lat, p_ctx, ctx_tiles, hg_lb, hg_norm_g[0], c_rows, w_mod[0], b_mod[0], n_early,
        cast_weights=(w_out[0], w_gate_up[0], w_down[0]))
    mod_lat = jnp.concatenate([mod_early, mod_late], axis=1).reshape(MOD_ROWS, N_MOD, d)[:bsz]
    at = _window_attention(p_lat, p_ctx, ctx_tiles, attn_sink[0])

    x1 = _out_projection(x.reshape(bsz * t_len, d), mod_lat, hg.reshape(bsz * t_len, HG_WIDTH),
                         at.reshape(bsz * t_len, ATT_WIDTH), w_out16, tokens_per_mod=t_len)
    x2 = _ffn(x1, mod_lat, norm_ffn_g[0], w_gate_up16, w_down16, tokens_per_mod=t_len)
    return x2.reshape(bsz, t_len, d)
```

```python
import functools

import jax
import jax.numpy as jnp
import numpy as np
from jax import lax
from jax.experimental import pallas as pl
from jax.experimental.pallas import tpu as pltpu

F32 = jnp.float32
BF16 = jnp.bfloat16

GRID_W = 64
HG_HEADS = 8
HG_DK = 128
HG_DV = 128
HG_CHUNK = 32
ATT_HEADS = 8
ATT_KV_HEADS = 2
ATT_GROUP = ATT_HEADS // ATT_KV_HEADS
HEAD_DIM = 128
WINDOW = 128
ATT_BLOCK = 128
ROPE_THETA = 10000.0
N_MOD = 6
RMS_EPS = 1e-6
NEG_INF = -1e30
HG_WIDTH = HG_HEADS * HG_DV
ATT_WIDTH = ATT_HEADS * HEAD_DIM
HG_COLS = 3 * HG_HEADS * HG_DK + 2 * HG_HEADS * HG_DV
ATT_COLS = (ATT_HEADS + 2 * ATT_KV_HEADS) * HEAD_DIM
IN_COLS = HG_COLS + ATT_COLS

VMEM_LIMIT_BYTES = 58 * 1024 * 1024
MOD_ROWS = 8

_NT = (((1,), (1,)), ((), ()))


def _dot(a, b):
    return jnp.dot(a, b, preferred_element_type=F32)


def _silu(x):
    h = 0.5 * x
    return h * jnp.tanh(h) + h


def _params(*sem):
    return pltpu.CompilerParams(dimension_semantics=sem, vmem_limit_bytes=VMEM_LIMIT_BYTES)


def _mod_kernel(c_ref, w_ref, b_ref, o_ref):
    a = _silu(c_ref[...]).astype(BF16)
    o_ref[...] = _dot(a, w_ref[...].astype(BF16)) + b_ref[...]


def _modulation(c_rows, w_mod, b_mod, n_cols, tn=1024):
    d, n = w_mod.shape
    return pl.pallas_call(
        _mod_kernel,
        grid=(n_cols // tn,),
        in_specs=[
            pl.BlockSpec((MOD_ROWS, d), lambda j: (0, 0)),
            pl.BlockSpec((d, tn), lambda j: (0, j)),
            pl.BlockSpec((1, tn), lambda j: (0, j)),
        ],
        out_specs=pl.BlockSpec((MOD_ROWS, tn), lambda j: (0, j)),
        out_shape=jax.ShapeDtypeStruct((MOD_ROWS, n_cols), F32),
        compiler_params=_params("arbitrary"),
        name="adaln_mod",
    )(c_rows, w_mod, b_mod.reshape(1, n))


HEAD_PAIR = 2 * HEAD_DIM


def _rope_tables(t_len):
    half = HEAD_DIM // 2
    nf = half // 2
    pos = np.arange(t_len)
    inv_freq = ROPE_THETA ** (-np.arange(nf, dtype=np.float64) / nf)
    ang_r = (pos // GRID_W)[:, None] * inv_freq
    ang_c = (pos % GRID_W)[:, None] * inv_freq
    cos = np.concatenate([np.cos(ang_r)] * 2 + [np.cos(ang_c)] * 2, axis=-1)
    sin = np.concatenate([-np.sin(ang_r), np.sin(ang_r), -np.sin(ang_c), np.sin(ang_c)], axis=-1)
    pair = lambda a: jnp.asarray(np.tile(a, (1, HEAD_PAIR // HEAD_DIM)), F32)
    return pair(cos), pair(sin)


def _pair_matrices():
    quarter = HEAD_DIM // 4
    lane = np.arange(HEAD_PAIR)
    src = np.where(lane % (2 * quarter) < quarter, lane + quarter, lane - quarter)
    swap = lane[:, None] == src[None, :]
    head_mean = ((lane[:, None] // HEAD_DIM) == (lane[None, :] // HEAD_DIM)) / HEAD_DIM
    return jnp.asarray(swap, BF16), jnp.asarray(head_mean, BF16)


def _norm_modulate_into(x_ref, g_ref, shift, scale, h_scr, rows=128):
    tm = x_ref.shape[0]
    gain = g_ref[...] * (1.0 + scale)

    def body(r, carry):
        sl = pl.ds(pl.multiple_of(r * rows, rows), rows)
        x = x_ref[sl, :]
        ms = jnp.mean(x * x, axis=-1, keepdims=True)
        h_scr[sl, :] = (x * lax.rsqrt(ms + RMS_EPS) * gain + shift).astype(BF16)
        return carry

    lax.fori_loop(0, tm // rows, body, 0, unroll=2)


def _inproj_kernel(slot_ref, isctx_ref, x_hbm, cx_hbm, mod_ref, cmod_ref, g_ref, w_ref, cos_ref, sin_ref, swap_ref,
                   hmean_ref, qg_ref, kg_ref, o_ref, oc_ref, x_buf, x_sem, h_scr, hc_scr, *, ctx_step0):
    i = pl.program_id(0)
    t = pl.program_id(1)
    last = i == pl.num_programs(0) - 1
    ctx_step = last & (isctx_ref[t] == 1)
    tm = x_buf.shape[0]
    cx_buf = x_buf.at[pl.ds(0, hc_scr.shape[0]), :]
    tn = o_ref.shape[1]
    hd = HEAD_DIM
    q_tile0 = HG_COLS // tn
    kv_tile = (HG_COLS + ATT_WIDTH) // tn

    def x_copy(row_tile):
        return pltpu.make_async_copy(x_hbm.at[pl.ds(pl.multiple_of(row_tile * tm, tm), tm), :], x_buf, x_sem)

    def cx_copy():
        return pltpu.make_async_copy(cx_hbm, cx_buf, x_sem)

    @pl.when(t == 0)
    def _():
        @pl.when(i == 0)
        def _():
            x_copy(i).start()

        x_copy(i).wait()
        _norm_modulate_into(x_buf, g_ref, mod_ref[0, 0:1, :], mod_ref[0, 1:2, :], h_scr)

    @pl.when(t == 1)
    def _():
        @pl.when(jnp.logical_not(last))
        def _():
            x_copy(i + 1).start()

        @pl.when(last)
        def _():
            cx_copy().start()

    @pl.when(last & (t == ctx_step0))
    def _():
        cx_copy().wait()
        _norm_modulate_into(cx_buf, g_ref, cmod_ref[0, 0:1, :], cmod_ref[0, 1:2, :], hc_scr)

    def project(h_ref):
        return _dot(h_ref[...], w_ref[...].astype(BF16))

    def qk_prep(acc, n_heads, g_row, scale, o_ref, rope):
        cols = [slice(p * HEAD_PAIR, (p + 1) * HEAD_PAIR) for p in range(n_heads * hd // HEAD_PAIR)]
        x = [acc[:, c] for c in cols]
        gain = g_row * scale
        ms = [_dot((xh * xh).astype(BF16), hmean_ref[...]) for xh in x]
        y = [xh * lax.rsqrt(m + RMS_EPS) * gain for xh, m in zip(x, ms)]
        if rope:
            swapped = [_dot(yh.astype(BF16), swap_ref[...]) for yh in y]
            y = [yh * cos_ref[...] + sh * sin_ref[...] for yh, sh in zip(y, swapped)]
        for c, yh in zip(cols, y):
            o_ref[:, c] = yh.astype(o_ref.dtype)

    @pl.when(t < q_tile0)
    def _():
        o_ref[...] = project(h_scr).astype(o_ref.dtype)

        @pl.when(ctx_step)
        def _():
            oc_ref[...] = project(hc_scr).astype(oc_ref.dtype)

    @pl.when((t >= q_tile0) & (t < kv_tile))
    def _():
        qk_prep(project(h_scr), tn // hd, qg_ref[...], hd ** -0.5, o_ref, True)

    @pl.when(t == kv_tile)
    def _():
        def keys_values(h_ref, out_ref, rope):
            acc = project(h_ref)
            qk_prep(acc, ATT_KV_HEADS, kg_ref[...], 1.0, out_ref, rope)
            out_ref[:, ATT_KV_HEADS * hd:] = acc[:, ATT_KV_HEADS * hd:].astype(out_ref.dtype)

        keys_values(h_scr, o_ref, True)

        @pl.when(ctx_step)
        def _():
            keys_values(hc_scr, oc_ref, False)


IN_TILE = 512


def _tiles_covering(col_ranges):
    return tuple(t for t in range(IN_COLS // IN_TILE)
                 if any(lo < (t + 1) * IN_TILE and t * IN_TILE < hi for lo, hi in col_ranges))


def _packed_col(col_tiles, col):
    return col_tiles.index(col // IN_TILE) * IN_TILE + col % IN_TILE


def _in_projection(x2d, cx2d, mod, cmod, norm_g, w_in, q_norm_g, k_norm_g, cos, sin, tokens_per_mod, tm, ctx_tiles):
    m, d = x2d.shape
    mc = cx2d.shape[0]
    tn = IN_TILE
    n_tiles = IN_COLS // tn
    n_rows = m // tm
    assert HG_COLS % tn == 0 and ATT_WIDTH % tn == 0 and 2 * ATT_KV_HEADS * HEAD_DIM == tn
    assert m % tm == 0 and mc <= tm
    assert all(t < HG_COLS // tn or t >= (HG_COLS + ATT_WIDTH) // tn for t in ctx_tiles)
    assert ctx_tiles[0] >= 2
    is_ctx = [int(t in ctx_tiles) for t in range(n_tiles)]
    slots = [max(sum(is_ctx[:t + 1]) - 1, 0) for t in range(n_tiles)]
    tiles_per_mod = tokens_per_mod // tm
    tiles_per_seq = cos.shape[0] // tm
    fixed = lambda *shape: pl.BlockSpec(shape, lambda i, t, slot, flag: (0,) * len(shape))
    tab = pl.BlockSpec((tm, HEAD_PAIR), lambda i, t, slot, flag: (i % tiles_per_seq, 0))
    per_pair = lambda g: jnp.tile(g.reshape(1, HEAD_DIM), (1, HEAD_PAIR // HEAD_DIM))
    grid_spec = pltpu.PrefetchScalarGridSpec(
        num_scalar_prefetch=2,
        grid=(n_rows, n_tiles),
        in_specs=[
            pl.BlockSpec(memory_space=pl.ANY),
            pl.BlockSpec(memory_space=pl.ANY),
            pl.BlockSpec((1, mod.shape[1], d), lambda i, t, slot, flag: (i // tiles_per_mod, 0, 0)),
            fixed(1, cmod.shape[1], d),
            fixed(1, d),
            pl.BlockSpec((d, tn), lambda i, t, slot, flag: (0, t)),
            tab, tab, fixed(HEAD_PAIR, HEAD_PAIR), fixed(HEAD_PAIR, HEAD_PAIR),
            fixed(1, HEAD_PAIR), fixed(1, HEAD_PAIR),
        ],
        out_specs=[
            pl.BlockSpec((tm, tn), lambda i, t, slot, flag: (i, t)),
            pl.BlockSpec((mc, tn), lambda i, t, slot, flag: (0, jnp.where(i == n_rows - 1, slot[t], 0))),
        ],
        scratch_shapes=[pltpu.VMEM((tm, d), jnp.float32), pltpu.SemaphoreType.DMA(()), pltpu.VMEM((tm, d), BF16),
                        pltpu.VMEM((mc, d), BF16)],
    )
    return pl.pallas_call(
        functools.partial(_inproj_kernel, ctx_step0=ctx_tiles[0]),
        grid_spec=grid_spec,
        out_shape=(jax.ShapeDtypeStruct((m, IN_COLS), BF16), jax.ShapeDtypeStruct((mc, len(ctx_tiles) * tn), BF16)),
        compiler_params=_params("arbitrary", "arbitrary"),
        name="in_proj",
    )(jnp.asarray(slots, jnp.int32), jnp.asarray(is_ctx, jnp.int32), x2d, cx2d, mod, cmod, norm_g.reshape(1, d), w_in,
      cos, sin, *_pair_matrices(), per_pair(q_norm_g), per_pair(k_norm_g))


HG_BLOCK = 256
CAST_ROWS = 16


def _cast_rows(src_ref, dst_ref):
    def body(r, carry):
        sl = pl.ds(pl.multiple_of(r * CAST_ROWS, CAST_ROWS), CAST_ROWS)
        dst_ref[sl, :] = src_ref[sl, :].astype(dst_ref.dtype)
        return carry

    lax.fori_loop(0, src_ref.shape[0] // CAST_ROWS, body, 0)


def _chunk_masks():
    r = np.arange(HG_BLOCK)[:, None]
    c = np.arange(HG_BLOCK)[None, :]
    same = (r // HG_CHUNK) == (c // HG_CHUNK)
    return (same & (r >= c)), (same & (r <= c)), same


def _hgrn_kernel(q_ref, ff_ref, fb_ref, i_ref, g_ref, cff_ref, cfb_ref, ci_ref, lb_ref, ng_ref,
                 cumf_ref, cumb_ref, dmask_ref, c_ref, wm_ref, bm_ref, *rest, n_cast):
    cast_src = rest[:n_cast]
    o_ref, mod_ref = rest[n_cast:n_cast + 2]
    cast_dst = rest[n_cast + 2:2 * n_cast + 2]
    of_scr, ob_scr = rest[2 * n_cast + 2:]

    mod_ref[...] = _dot(_silu(c_ref[...]).astype(BF16), wm_ref[...].astype(BF16)) + bm_ref[...]
    for src, dst in zip(cast_src, cast_dst):
        _cast_rows(src, dst)

    t_len = q_ref.shape[1]
    l_len = ci_ref.shape[1]
    c = HG_CHUNK
    blk = HG_BLOCK
    nch = blk // c
    n_blk = t_len // blk
    assert l_len == blk

    a = lb_ref[...]
    e = jnp.exp(a - jnp.max(a, axis=0, keepdims=True))
    lb = e[0] / jnp.sum(e, axis=0)

    def gates(f_pre, lb_d):
        f = 0.5 * (1.0 + lb_d) + (0.5 * (1.0 - lb_d)) * jnp.tanh(0.5 * f_pre.astype(F32))
        log_f = jnp.log2(f)
        hi = lax.bitcast_convert_type(lax.bitcast_convert_type(log_f, jnp.uint32) & jnp.uint32(0xFFFF0000), F32)
        return 1.0 - f, jnp.concatenate([hi.astype(BF16), (log_f - hi).astype(BF16)], axis=1)

    def decays(parts, cum_ref, reverse):
        b = _dot(cum_ref[...], parts)
        b = b[:, 0:128] + b[:, 128:256]
        b3 = b.reshape(nch, c, HG_DK)
        tot = b3[:, 0:1, :] if reverse else b3[:, c - 1:c, :]
        b_tot = jnp.broadcast_to(tot, (nch, c, HG_DK)).reshape(blk, HG_DK)
        return b, b_tot, jnp.exp2(tot)

    def increments(key, b, b_tot, v):
        k_dec = (key * jnp.exp2(b_tot - b)).astype(BF16)
        v_t = v.astype(BF16).T
        upd = []
        for h in range(blk // 128):
            kd = k_dec[h * 128:(h + 1) * 128, :]
            kd4 = jnp.concatenate([kd] * 4, axis=1) * dmask_ref[...]
            u4 = _dot(v_t[:, h * 128:(h + 1) * 128], kd4)
            upd += [u4[:, j * HG_DK:(j + 1) * HG_DK] for j in range(4)]
        return upd

    def scores(q_pre, key, b):
        q_dec = (_silu(q_pre.astype(F32)) * jnp.exp2(b)).astype(BF16)
        k_inv = (key * jnp.exp2(-b)).astype(BF16)
        return q_dec, lax.dot_general(q_dec, k_inv, _NT, preferred_element_type=F32)

    def intra(s, v, cum_ref):
        return _dot(s.astype(BF16) * cum_ref[...], v.astype(BF16))

    def recur_pair(st, chains):
        outs = [[None] * nch for _ in chains]
        st = list(st)
        for step in range(nch):
            for d, (dec, upd, q_dec, o_intra) in enumerate(chains):
                n = nch - 1 - step if d == 1 else step
                if q_dec is not None:
                    rows = slice(n * c, (n + 1) * c)
                    outs[d][n] = o_intra[rows, :] + _dot(q_dec[rows, :], st[d].T.astype(BF16))
                st[d] = st[d] * dec[n] + upd[n]
        return [jnp.concatenate(o, axis=0) if o[0] is not None else None for o in outs], st

    lb_d = (lb[0:1, :], lb[1:2, :])
    cum = (cumf_ref, cumb_ref)
    f_lat = (ff_ref, fb_ref)
    o_scr = (of_scr, ob_scr)

    cv = ci_ref[0]
    ctx_chains = []
    for d, f_ref in enumerate((cff_ref, cfb_ref)):
        key, parts = gates(f_ref[0], lb_d[d])
        b, b_tot, dec = decays(parts, cum[d], d == 1)
        ctx_chains.append((dec, increments(key, b, b_tot, cv), None, None))
    _, st = recur_pair([jnp.zeros((HG_DV, HG_DK), F32)] * 2, ctx_chains)

    blocks_per_step = 2

    def lat_body(r, carry):
        chains = []
        for u in range(blocks_per_step):
            up = r * blocks_per_step + u
            chains.append((0, pl.ds(pl.multiple_of(up * blk, blk), blk)))
            chains.append((1, pl.ds(pl.multiple_of((n_blk - 1 - up) * blk, blk), blk)))
        g = [gates(f_lat[d][0, sl, :], lb_d[d]) for d, sl in chains]
        dk = [decays(g[i][1], cum[d], d == 1) for i, (d, sl) in enumerate(chains)]
        v = [i_ref[0, sl, :] for d, sl in chains]
        upd = [increments(g[i][0], dk[i][0], dk[i][1], v[i]) for i in range(len(chains))]
        qs = [scores(q_ref[0, sl, :], g[i][0], dk[i][0]) for i, (d, sl) in enumerate(chains)]
        o_in = [intra(qs[i][1], v[i], cum[d]) for i, (d, sl) in enumerate(chains)]
        st = list(carry)
        for u in range(blocks_per_step):
            pair = range(2 * u, 2 * u + 2)
            outs, st = recur_pair(st, [(dk[i][2], upd[i], qs[i][0], o_in[i]) for i in pair])
            for i in pair:
                d, sl = chains[i]
                o_scr[d][sl, :] = outs[d]
        return tuple(st)

    lax.fori_loop(0, n_blk // blocks_per_step, lat_body, tuple(st))

    ng = ng_ref[...]

    def fin_body(r, carry):
        sl = pl.ds(pl.multiple_of(r * blk, blk), blk)
        o = of_scr[sl, :] + ob_scr[sl, :]
        y = o * lax.rsqrt(jnp.mean(o * o, axis=-1, keepdims=True) + RMS_EPS) * ng
        o_ref[0, sl, :] = (y * _silu(g_ref[0, sl, :].astype(F32))).astype(o_ref.dtype)
        return carry

    lax.fori_loop(0, n_blk, fin_body, 0, unroll=4)


def _hgrn_mixer(p_lat, p_ctx, ctx_tiles, hg_lb, norm_g, c_rows, w_mod, b_mod, mod_col0, cast_weights):
    bsz, t_len, _ = p_lat.shape
    l_len = p_ctx.shape[1]
    w = HG_DK
    d, n_mod = w_mod.shape
    n_steps = bsz * HG_HEADS
    mod_tn = (n_mod - mod_col0) // n_steps
    assert mod_tn % 128 == 0 and mod_col0 % mod_tn == 0 and mod_tn * n_steps == n_mod - mod_col0
    assert all(m.shape[0] % (n_steps * CAST_ROWS) == 0 for m in cast_weights)

    def slab(m):
        return pl.BlockSpec((m.shape[0] // n_steps, m.shape[1]), lambda b, h: (b * HG_HEADS + h, 0))

    def lat(group):
        return pl.BlockSpec((1, t_len, w), lambda b, h: (b, 0, group * HG_HEADS + h))

    def ctx(group):
        blk0 = _packed_col(ctx_tiles, group * HG_HEADS * w) // w
        return pl.BlockSpec((1, l_len, w), lambda b, h: (b, 0, blk0 + h))

    def const(shape):
        return pl.BlockSpec(shape, lambda b, h: (0,) * len(shape))

    lower, upper, same = _chunk_masks()
    diag4 = np.tile(same[:128, ::HG_CHUNK][:, :4, None], (1, 1, HG_DK)).reshape(128, 4 * HG_DK)
    seq_f32 = pltpu.VMEM((t_len, w), F32)
    return pl.pallas_call(
        functools.partial(_hgrn_kernel, n_cast=len(cast_weights)),
        grid=(bsz, HG_HEADS),
        in_specs=[
            lat(0), lat(1), lat(2), lat(3), lat(4),
            ctx(1), ctx(2), ctx(3),
            pl.BlockSpec((hg_lb.shape[0], 2, w), lambda b, h: (0, 0, h)),
            const((1, HG_DV)),
            const((HG_BLOCK, HG_BLOCK)), const((HG_BLOCK, HG_BLOCK)),
            const((128, 4 * HG_DK)),
            const((MOD_ROWS, d)),
            pl.BlockSpec((d, mod_tn), lambda b, h: (0, mod_col0 // mod_tn + b * HG_HEADS + h)),
            pl.BlockSpec((1, mod_tn), lambda b, h: (0, mod_col0 // mod_tn + b * HG_HEADS + h)),
        ] + [slab(m) for m in cast_weights],
        out_specs=[pl.BlockSpec((1, t_len, HG_DV), lambda b, h: (b, 0, h)),
                   pl.BlockSpec((MOD_ROWS, mod_tn), lambda b, h: (0, b * HG_HEADS + h))]
        + [slab(m) for m in cast_weights],
        out_shape=[jax.ShapeDtypeStruct((bsz, t_len, HG_WIDTH), BF16),
                   jax.ShapeDtypeStruct((MOD_ROWS, n_mod - mod_col0), F32)]
        + [jax.ShapeDtypeStruct(m.shape, BF16) for m in cast_weights],
        scratch_shapes=[seq_f32, seq_f32],
        compiler_params=_params("parallel", "parallel"),
        name="hgrn2_mixer",
    )(p_lat, p_lat, p_lat, p_lat, p_lat, p_ctx, p_ctx, p_ctx, hg_lb, norm_g.reshape(1, HG_DV),
      jnp.asarray(lower, BF16), jnp.asarray(upper, BF16), jnp.asarray(diag4, BF16),
      c_rows, w_mod, b_mod.reshape(1, n_mod), *cast_weights)


ATT_QROWS = 8 * ATT_BLOCK


def _attn_kernel(q_ref, kvp_ref, kvc_ref, kvn_ref, ckv_ref, sink_ref, o_ref):
    n = pl.program_id(1)
    n_steps = pl.num_programs(1)
    rows = q_ref.shape[1]
    blk = ATT_BLOCK
    hd = HEAD_DIM
    l_len = ckv_ref.shape[1]
    band = 3 * blk
    n_keys = band + l_len
    ones = jnp.ones((n_keys, hd), BF16)

    def band_rows(cols):
        return jnp.concatenate([kvp_ref[0, rows - blk:, cols], kvc_ref[0, :, cols], kvn_ref[0, :blk, cols]], axis=0)

    kv_slabs = [band_rows(slice(j * hd, (j + 1) * hd)) for j in range(2 * ATT_KV_HEADS)]
    ctx_slabs = [ckv_ref[0, :, j * hd:(j + 1) * hd] for j in range(2 * ATT_KV_HEADS)]
    qi = lax.broadcasted_iota(jnp.int32, (blk, n_keys), 0)
    kj = lax.broadcasted_iota(jnp.int32, (blk, n_keys), 1)
    in_window = jnp.abs(kj - blk - qi) <= WINDOW

    for sub in range(rows // blk):
        lo = sub * blk
        k_pos = n * rows + lo - blk + kj
        ok = (kj >= band) | (in_window & (k_pos >= 0) & (k_pos < n_steps * rows))
        bias = jnp.where(ok, 0.0, NEG_INF)
        k_all = [jnp.concatenate([kv_slabs[j][lo:lo + band], ctx_slabs[j]], axis=0) for j in range(ATT_KV_HEADS)]
        v_aug = [jnp.concatenate([jnp.concatenate([kv_slabs[ATT_KV_HEADS + j][lo:lo + band],
                                                   ctx_slabs[ATT_KV_HEADS + j]], axis=0), ones], axis=1)
                 for j in range(ATT_KV_HEADS)]
        heads = range(ATT_HEADS)
        s = [lax.dot_general(q_ref[0, lo:lo + blk, h * hd:(h + 1) * hd], k_all[h // ATT_GROUP], _NT,
                             preferred_element_type=F32) + bias for h in heads]
        sink = [sink_ref[:, h:h + 1] for h in heads]
        m = [jnp.maximum(jnp.max(s[h], axis=-1, keepdims=True), sink[h]) for h in heads]
        e = [jnp.exp(s[h] - m[h]).astype(BF16) for h in heads]
        oa = [_dot(e[h], v_aug[h // ATT_GROUP]) for h in heads]
        for h in heads:
            o = oa[h][:, :hd] / (oa[h][:, hd:] + jnp.exp(sink[h] - m[h]))
            o_ref[0, lo:lo + blk, h * hd:(h + 1) * hd] = o.astype(o_ref.dtype)


def _window_attention(p_lat, p_ctx, ctx_tiles, sink):
    bsz, t_len, _ = p_lat.shape
    l_len = p_ctx.shape[1]
    nb = t_len // ATT_QROWS
    kvw = 2 * ATT_KV_HEADS * HEAD_DIM
    q_blk = HG_COLS // ATT_WIDTH
    kv_blk = (HG_COLS + ATT_WIDTH) // kvw
    ckv_blk = _packed_col(ctx_tiles, HG_COLS + ATT_WIDTH) // kvw

    def prev(n):
        return jnp.maximum(n - 1, 0)

    def nxt(n):
        return jnp.minimum(n + 1, nb - 1)

    def kv_spec(shift):
        return pl.BlockSpec((1, ATT_QROWS, kvw), lambda b, n: (b, shift(n), kv_blk))

    same = lambda n: n
    return pl.pallas_call(
        _attn_kernel,
        grid=(bsz, nb),
        in_specs=[
            pl.BlockSpec((1, ATT_QROWS, ATT_WIDTH), lambda b, n: (b, n, q_blk)),
            kv_spec(prev), kv_spec(same), kv_spec(nxt),
            pl.BlockSpec((1, l_len, kvw), lambda b, n: (b, 0, ckv_blk)),
            pl.BlockSpec((1, ATT_HEADS), lambda b, n: (0, 0)),
        ],
        out_specs=pl.BlockSpec((1, ATT_QROWS, ATT_WIDTH), lambda b, n: (b, n, 0)),
        out_shape=jax.ShapeDtypeStruct((bsz, t_len, ATT_WIDTH), BF16),
        compiler_params=_params("parallel", "parallel"),
        name="window_gqa",
    )(p_lat, p_lat, p_lat, p_lat, p_ctx, sink.reshape(1, ATT_HEADS))


def _outproj_kernel(x_ref, mod_ref, hg_ref, at_ref, w_ref, o_ref):
    kh = hg_ref.shape[1]
    acc = _dot(hg_ref[...], w_ref[:kh, :]) + _dot(at_ref[...], w_ref[kh:, :])
    o_ref[...] = x_ref[...] + mod_ref[0, 2:3, :] * acc


def _out_projection(x2d, mod, hg2d, at2d, w_out, tokens_per_mod, tm=512):
    m, d = x2d.shape
    kh = hg2d.shape[1]
    tiles_per_mod = tokens_per_mod // tm
    return pl.pallas_call(
        _outproj_kernel,
        grid=(m // tm,),
        in_specs=[
            pl.BlockSpec((tm, d), lambda i: (i, 0)),
            pl.BlockSpec((1, N_MOD, d), lambda i: (i // tiles_per_mod, 0, 0)),
            pl.BlockSpec((tm, kh), lambda i: (i, 0)),
            pl.BlockSpec((tm, kh), lambda i: (i, 0)),
            pl.BlockSpec(w_out.shape, lambda i: (0, 0), pipeline_mode=pl.Buffered(1)),
        ],
        out_specs=pl.BlockSpec((tm, d), lambda i: (i, 0)),
        out_shape=jax.ShapeDtypeStruct((m, d), F32),
        compiler_params=_params("parallel"),
        name="out_proj",
    )(x2d, mod, hg2d, at2d, w_out)


def _ffn_kernel(x_ref, mod_ref, g_ref, wg_ref, wu_ref, wd_ref, o_ref, h_scr):
    j = pl.program_id(1)

    @pl.when(j == 0)
    def _():
        _norm_modulate_into(x_ref, g_ref, mod_ref[0, 3:4, :], mod_ref[0, 4:5, :], h_scr)

    def gated_down():
        h = h_scr[...]
        gate = _dot(h, wg_ref[...])
        up = _dot(h, wu_ref[...])
        act = (_silu(gate) * up).astype(BF16)
        return mod_ref[0, 5:6, :] * _dot(act, wd_ref[...])

    @pl.when(j == 0)
    def _():
        o_ref[...] = x_ref[...] + gated_down()

    @pl.when(j > 0)
    def _():
        o_ref[...] += gated_down()


def _ffn(x2d, mod, norm_g, w_gate_up, w_down, tokens_per_mod, tm=1024, tf=512):
    m, d = x2d.shape
    d_ff = w_down.shape[0]
    nf = d_ff // tf
    tiles_per_mod = tokens_per_mod // tm
    return pl.pallas_call(
        _ffn_kernel,
        grid=(m // tm, nf),
        in_specs=[
            pl.BlockSpec((tm, d), lambda i, j: (i, 0)),
            pl.BlockSpec((1, N_MOD, d), lambda i, j: (i // tiles_per_mod, 0, 0)),
            pl.BlockSpec((1, d), lambda i, j: (0, 0)),
            pl.BlockSpec((d, tf), lambda i, j: (0, j)),
            pl.BlockSpec((d, tf), lambda i, j: (0, nf + j)),
            pl.BlockSpec((tf, d), lambda i, j: (j, 0)),
        ],
        out_specs=pl.BlockSpec((tm, d), lambda i, j: (i, 0)),
        out_shape=jax.ShapeDtypeStruct((m, d), F32),
        scratch_shapes=[pltpu.VMEM((tm, d), BF16)],
        compiler_params=_params("parallel", "arbitrary"),
        name="swiglu_ffn",
    )(x2d, mod, norm_g.reshape(1, d), w_gate_up, w_gate_up, w_down)


def kernel(x, c, ctx, c_ctx, w_mod, b_mod, norm_mix_g, norm_ffn_g, w_in, hg_lb, hg_norm_g,
           q_norm_g, k_norm_g, attn_sink, w_out, w_gate_up, w_down):
    bsz, t_len, d = x.shape
    l_len = ctx.shape[1]
    depth = w_mod.shape[0]
    assert depth == 1, "single-layer block: the context stream never needs its own outputs"
    assert bsz + 1 <= MOD_ROWS

    c_rows = jnp.concatenate([c, c_ctx[None, :], jnp.zeros((MOD_ROWS - bsz - 1, d), F32)], axis=0)
    n_early = 2 * d
    mod_early = _modulation(c_rows, w_mod[0], b_mod[0], n_early)
    mod_mix = mod_early.reshape(MOD_ROWS, 2, d)

    cos, sin = _rope_tables(t_len)
    hk = HG_HEADS * HG_DK
    ctx_tiles = _tiles_covering([(hk, 3 * hk + HG_WIDTH), (HG_COLS + ATT_WIDTH, IN_COLS)])
    p_lat, p_ctx = _in_projection(x.reshape(bsz * t_len, d), ctx.reshape(bsz * l_len, d), mod_mix[:bsz],
                                  mod_mix[bsz:bsz + 1], norm_mix_g[0], w_in[0], q_norm_g[0], k_norm_g[0], cos, sin,
                                  tokens_per_mod=t_len, tm=2048, ctx_tiles=ctx_tiles)
    p_lat = p_lat.reshape(bsz, t_len, -1)
    p_ctx = p_ctx.reshape(bsz, l_len, -1)

    hg, mod_late, w_out16, w_gate_up16, w_down16 = _hgrn_mixer(
        p_lat, p_ctx, ctx_tiles, hg_lb, hg_norm_g[0], c_rows, w_mod[0], b_mod[0], n_early,
        cast_weights=(w_out[0], w_gate_up[0], w_down[0]))
    mod_lat = jnp.concatenate([mod_early, mod_late], axis=1).reshape(MOD_ROWS, N_MOD, d)[:bsz]
    at = _window_attention(p_lat, p_ctx, ctx_tiles, attn_sink[0])

    x1 = _out_projection(x.reshape(bsz * t_len, d), mod_lat, hg.reshape(bsz * t_len, HG_WIDTH),
                         at.reshape(bsz * t_len, ATT_WIDTH), w_out16, tokens_per_mod=t_len)
    x2 = _ffn(x1, mod_lat, norm_ffn_g[0], w_gate_up16, w_down16, tokens_per_mod=t_len)
    return x2.reshape(bsz, t_len, d)
```

```python
import functools

import jax
import jax.numpy as jnp
import numpy as np
from jax import lax
from jax.experimental import pallas as pl
from jax.experimental.pallas import tpu as pltpu

F32 = jnp.float32
BF16 = jnp.bfloat16

GRID_W = 64
HG_HEADS = 8
HG_DK = 128
HG_DV = 128
HG_CHUNK = 32
ATT_HEADS = 8
ATT_KV_HEADS = 2
ATT_GROUP = ATT_HEADS // ATT_KV_HEADS
HEAD_DIM = 128
WINDOW = 128
ATT_BLOCK = 128
ROPE_THETA = 10000.0
N_MOD = 6
RMS_EPS = 1e-6
NEG_INF = -1e30
HG_WIDTH = HG_HEADS * HG_DV
ATT_WIDTH = ATT_HEADS * HEAD_DIM
HG_COLS = 3 * HG_HEADS * HG_DK + 2 * HG_HEADS * HG_DV
ATT_COLS = (ATT_HEADS + 2 * ATT_KV_HEADS) * HEAD_DIM
IN_COLS = HG_COLS + ATT_COLS

VMEM_LIMIT_BYTES = 58 * 1024 * 1024
MOD_ROWS = 8
N_MOD_EARLY = 2

_NT = (((1,), (1,)), ((), ()))


def _dot(a, b):
    return jnp.dot(a, b, preferred_element_type=F32)


def _silu(x):
    h = 0.5 * x
    return h * jnp.tanh(h) + h


def _params(*sem):
    return pltpu.CompilerParams(dimension_semantics=sem, vmem_limit_bytes=VMEM_LIMIT_BYTES)


def _mod_vector(mod_ref, row, k, d):
    return mod_ref[pl.ds(row, 1), pl.ds(k * d, d)]


def _mod_kernel(c_ref, w_ref, b_ref, o_ref):
    a = _silu(c_ref[...]).astype(BF16)
    o_ref[...] = _dot(a, w_ref[...].astype(BF16)) + b_ref[...]


def _modulation(c_rows, w_mod, b_mod, n_cols, tn=1024):
    d, n = w_mod.shape
    return pl.pallas_call(
        _mod_kernel,
        grid=(n_cols // tn,),
        in_specs=[
            pl.BlockSpec((MOD_ROWS, d), lambda j: (0, 0)),
            pl.BlockSpec((d, tn), lambda j: (0, j)),
            pl.BlockSpec((1, tn), lambda j: (0, j)),
        ],
        out_specs=pl.BlockSpec((MOD_ROWS, tn), lambda j: (0, j)),
        out_shape=jax.ShapeDtypeStruct((MOD_ROWS, n_cols), F32),
        compiler_params=_params("arbitrary"),
        name="adaln_mod",
    )(c_rows, w_mod, b_mod.reshape(1, n))


HEAD_PAIR = 2 * HEAD_DIM


def _rope_tables(t_len):
    half = HEAD_DIM // 2
    nf = half // 2
    pos = np.arange(t_len)
    inv_freq = ROPE_THETA ** (-np.arange(nf, dtype=np.float64) / nf)
    ang_r = (pos // GRID_W)[:, None] * inv_freq
    ang_c = (pos % GRID_W)[:, None] * inv_freq
    cos = np.concatenate([np.cos(ang_r)] * 2 + [np.cos(ang_c)] * 2, axis=-1)
    sin = np.concatenate([-np.sin(ang_r), np.sin(ang_r), -np.sin(ang_c), np.sin(ang_c)], axis=-1)
    pair = lambda a: jnp.asarray(np.tile(a, (1, HEAD_PAIR // HEAD_DIM)), F32)
    return pair(cos), pair(sin)


def _pair_matrices():
    quarter = HEAD_DIM // 4
    lane = np.arange(HEAD_PAIR)
    src = np.where(lane % (2 * quarter) < quarter, lane + quarter, lane - quarter)
    swap = lane[:, None] == src[None, :]
    head_mean = ((lane[:, None] // HEAD_DIM) == (lane[None, :] // HEAD_DIM)) / HEAD_DIM
    return jnp.asarray(swap, BF16), jnp.asarray(head_mean, BF16)


def _norm_modulate_into(x_ref, g_ref, shift, scale, h_scr, rows=128):
    tm = x_ref.shape[0]
    gain = g_ref[...] * (1.0 + scale)

    def body(r, carry):
        sl = pl.ds(pl.multiple_of(r * rows, rows), rows)
        x = x_ref[sl, :]
        ms = jnp.mean(x * x, axis=-1, keepdims=True)
        h_scr[sl, :] = (x * lax.rsqrt(ms + RMS_EPS) * gain + shift).astype(BF16)
        return carry

    lax.fori_loop(0, tm // rows, body, 0, unroll=2)


def _inproj_kernel(slot_ref, isctx_ref, x_hbm, cx_hbm, mod_ref, g_ref, w_ref, cos_ref, sin_ref, swap_ref, hmean_ref,
                   qg_ref, kg_ref, o_ref, oc_ref, x_buf, x_sem, h_scr, hc_scr, *, ctx_step0, tiles_per_mod, ctx_mod_row):
    i = pl.program_id(0)
    t = pl.program_id(1)
    last = i == pl.num_programs(0) - 1
    ctx_step = last & (isctx_ref[t] == 1)
    tm, d = x_buf.shape
    cx_buf = x_buf.at[pl.ds(0, hc_scr.shape[0]), :]
    tn = o_ref.shape[1]
    hd = HEAD_DIM
    q_tile0 = HG_COLS // tn
    kv_tile = (HG_COLS + ATT_WIDTH) // tn

    def x_copy(row_tile):
        return pltpu.make_async_copy(x_hbm.at[pl.ds(pl.multiple_of(row_tile * tm, tm), tm), :], x_buf, x_sem)

    def cx_copy():
        return pltpu.make_async_copy(cx_hbm, cx_buf, x_sem)

    @pl.when(t == 0)
    def _():
        @pl.when(i == 0)
        def _():
            x_copy(i).start()

        x_copy(i).wait()
        row = i // tiles_per_mod
        _norm_modulate_into(x_buf, g_ref, _mod_vector(mod_ref, row, 0, d), _mod_vector(mod_ref, row, 1, d), h_scr)

    @pl.when(t == 1)
    def _():
        @pl.when(jnp.logical_not(last))
        def _():
            x_copy(i + 1).start()

        @pl.when(last)
        def _():
            cx_copy().start()

    @pl.when(last & (t == ctx_step0))
    def _():
        cx_copy().wait()
        _norm_modulate_into(cx_buf, g_ref, _mod_vector(mod_ref, ctx_mod_row, 0, d),
                            _mod_vector(mod_ref, ctx_mod_row, 1, d), hc_scr)

    def project(h_ref):
        return _dot(h_ref[...], w_ref[...].astype(BF16))

    def qk_prep(acc, n_heads, g_row, scale, o_ref, rope):
        cols = [slice(p * HEAD_PAIR, (p + 1) * HEAD_PAIR) for p in range(n_heads * hd // HEAD_PAIR)]
        x = [acc[:, c] for c in cols]
        gain = jnp.concatenate([g_row] * (HEAD_PAIR // hd), axis=1) * scale
        ms = [_dot((xh * xh).astype(BF16), hmean_ref[...]) for xh in x]
        y = [xh * lax.rsqrt(m + RMS_EPS) * gain for xh, m in zip(x, ms)]
        if rope:
            swapped = [_dot(yh.astype(BF16), swap_ref[...]) for yh in y]
            y = [yh * cos_ref[...] + sh * sin_ref[...] for yh, sh in zip(y, swapped)]
        for c, yh in zip(cols, y):
            o_ref[:, c] = yh.astype(o_ref.dtype)

    @pl.when(t < q_tile0)
    def _():
        o_ref[...] = project(h_scr).astype(o_ref.dtype)

        @pl.when(ctx_step)
        def _():
            oc_ref[...] = project(hc_scr).astype(oc_ref.dtype)

    @pl.when((t >= q_tile0) & (t < kv_tile))
    def _():
        qk_prep(project(h_scr), tn // hd, qg_ref[...], hd ** -0.5, o_ref, True)

    @pl.when(t == kv_tile)
    def _():
        def keys_values(h_ref, out_ref, rope):
            acc = project(h_ref)
            qk_prep(acc, ATT_KV_HEADS, kg_ref[...], 1.0, out_ref, rope)
            out_ref[:, ATT_KV_HEADS * hd:] = acc[:, ATT_KV_HEADS * hd:].astype(out_ref.dtype)

        keys_values(h_scr, o_ref, True)

        @pl.when(ctx_step)
        def _():
            keys_values(hc_scr, oc_ref, False)


IN_TILE = 512


def _tiles_covering(col_ranges):
    return tuple(t for t in range(IN_COLS // IN_TILE)
                 if any(lo < (t + 1) * IN_TILE and t * IN_TILE < hi for lo, hi in col_ranges))


def _packed_col(col_tiles, col):
    return col_tiles.index(col // IN_TILE) * IN_TILE + col % IN_TILE


def _in_projection(x2d, cx2d, mod, norm_g, w_in, q_norm_g, k_norm_g, cos, sin, tokens_per_mod, ctx_mod_row, tm,
                   ctx_tiles):
    m, d = x2d.shape
    mc = cx2d.shape[0]
    tn = IN_TILE
    n_tiles = IN_COLS // tn
    n_rows = m // tm
    assert HG_COLS % tn == 0 and ATT_WIDTH % tn == 0 and 2 * ATT_KV_HEADS * HEAD_DIM == tn
    assert m % tm == 0 and mc <= tm
    assert all(t < HG_COLS // tn or t >= (HG_COLS + ATT_WIDTH) // tn for t in ctx_tiles)
    assert ctx_tiles[0] >= 2
    is_ctx = [int(t in ctx_tiles) for t in range(n_tiles)]
    slots = [max(sum(is_ctx[:t + 1]) - 1, 0) for t in range(n_tiles)]
    tiles_per_seq = cos.shape[0] // tm
    fixed = lambda *shape: pl.BlockSpec(shape, lambda i, t, slot, flag: (0,) * len(shape))
    tab = pl.BlockSpec((tm, HEAD_PAIR), lambda i, t, slot, flag: (i % tiles_per_seq, 0))
    grid_spec = pltpu.PrefetchScalarGridSpec(
        num_scalar_prefetch=2,
        grid=(n_rows, n_tiles),
        in_specs=[
            pl.BlockSpec(memory_space=pl.ANY),
            pl.BlockSpec(memory_space=pl.ANY),
            fixed(*mod.shape),
            fixed(1, d),
            pl.BlockSpec((d, tn), lambda i, t, slot, flag: (0, t)),
            tab, tab, fixed(HEAD_PAIR, HEAD_PAIR), fixed(HEAD_PAIR, HEAD_PAIR),
            fixed(1, HEAD_DIM), fixed(1, HEAD_DIM),
        ],
        out_specs=[
            pl.BlockSpec((tm, tn), lambda i, t, slot, flag: (i, t)),
            pl.BlockSpec((mc, tn), lambda i, t, slot, flag: (0, jnp.where(i == n_rows - 1, slot[t], 0))),
        ],
        scratch_shapes=[pltpu.VMEM((tm, d), jnp.float32), pltpu.SemaphoreType.DMA(()), pltpu.VMEM((tm, d), BF16),
                        pltpu.VMEM((mc, d), BF16)],
    )
    return pl.pallas_call(
        functools.partial(_inproj_kernel, ctx_step0=ctx_tiles[0], tiles_per_mod=tokens_per_mod // tm,
                          ctx_mod_row=ctx_mod_row),
        grid_spec=grid_spec,
        out_shape=(jax.ShapeDtypeStruct((m, IN_COLS), BF16), jax.ShapeDtypeStruct((mc, len(ctx_tiles) * tn), BF16)),
        compiler_params=_params("arbitrary", "arbitrary"),
        name="in_proj",
    )(jnp.asarray(slots, jnp.int32), jnp.asarray(is_ctx, jnp.int32), x2d, cx2d, mod, norm_g.reshape(1, d), w_in,
      cos, sin, *_pair_matrices(), q_norm_g.reshape(1, HEAD_DIM), k_norm_g.reshape(1, HEAD_DIM))


HG_BLOCK = 256
CAST_ROWS = 16


def _cast_rows(src_ref, dst_ref):
    def body(r, carry):
        sl = pl.ds(pl.multiple_of(r * CAST_ROWS, CAST_ROWS), CAST_ROWS)
        dst_ref[sl, :] = src_ref[sl, :].astype(dst_ref.dtype)
        return carry

    lax.fori_loop(0, src_ref.shape[0] // CAST_ROWS, body, 0)


def _chunk_masks():
    r = np.arange(HG_BLOCK)[:, None]
    c = np.arange(HG_BLOCK)[None, :]
    same = (r // HG_CHUNK) == (c // HG_CHUNK)
    return (same & (r >= c)), (same & (r <= c)), same


def _hgrn_kernel(q_ref, ff_ref, fb_ref, i_ref, g_ref, cff_ref, cfb_ref, ci_ref, lb_ref, ng_ref,
                 cumf_ref, cumb_ref, dmask_ref, c_ref, wm_ref, bm_ref, *rest, n_cast):
    cast_src = rest[:n_cast]
    o_ref, mod_ref = rest[n_cast:n_cast + 2]
    cast_dst = rest[n_cast + 2:2 * n_cast + 2]
    of_scr, ob_scr = rest[2 * n_cast + 2:]

    mod_ref[...] = _dot(_silu(c_ref[...]).astype(BF16), wm_ref[...].astype(BF16)) + bm_ref[...]
    for src, dst in zip(cast_src, cast_dst):
        _cast_rows(src, dst)

    t_len = q_ref.shape[1]
    l_len = ci_ref.shape[1]
    c = HG_CHUNK
    blk = HG_BLOCK
    nch = blk // c
    n_blk = t_len // blk
    assert l_len == blk

    a = lb_ref[...]
    e = jnp.exp(a - jnp.max(a, axis=0, keepdims=True))
    lb = e[0] / jnp.sum(e, axis=0)

    def gates(f_pre, lb_d):
        f = 0.5 * (1.0 + lb_d) + (0.5 * (1.0 - lb_d)) * jnp.tanh(0.5 * f_pre.astype(F32))
        log_f = jnp.log2(f)
        hi = lax.bitcast_convert_type(lax.bitcast_convert_type(log_f, jnp.uint32) & jnp.uint32(0xFFFF0000), F32)
        return 1.0 - f, jnp.concatenate([hi.astype(BF16), (log_f - hi).astype(BF16)], axis=1)

    def decays(parts, cum_ref, reverse):
        b = _dot(cum_ref[...], parts)
        b = b[:, 0:128] + b[:, 128:256]
        b3 = b.reshape(nch, c, HG_DK)
        tot = b3[:, 0:1, :] if reverse else b3[:, c - 1:c, :]
        b_tot = jnp.broadcast_to(tot, (nch, c, HG_DK)).reshape(blk, HG_DK)
        return b, b_tot, jnp.exp2(tot)

    def increments(key, b, b_tot, v):
        k_dec = (key * jnp.exp2(b_tot - b)).astype(BF16)
        v_t = v.astype(BF16).T
        upd = []
        for h in range(blk // 128):
            kd = k_dec[h * 128:(h + 1) * 128, :]
            kd4 = jnp.concatenate([kd] * 4, axis=1) * dmask_ref[...]
            u4 = _dot(v_t[:, h * 128:(h + 1) * 128], kd4)
            upd += [u4[:, j * HG_DK:(j + 1) * HG_DK] for j in range(4)]
        return upd

    def scores(q_pre, key, b):
        q_dec = (_silu(q_pre.astype(F32)) * jnp.exp2(b)).astype(BF16)
        k_inv = (key * jnp.exp2(-b)).astype(BF16)
        return q_dec, lax.dot_general(q_dec, k_inv, _NT, preferred_element_type=F32)

    def intra(s, v, cum_ref):
        return _dot(s.astype(BF16) * cum_ref[...], v.astype(BF16))

    def recur_pair(st, chains):
        outs = [[None] * nch for _ in chains]
        st = list(st)
        for step in range(nch):
            for d, (dec, upd, q_dec, o_intra) in enumerate(chains):
                n = nch - 1 - step if d == 1 else step
                if q_dec is not None:
                    rows = slice(n * c, (n + 1) * c)
                    outs[d][n] = o_intra[rows, :] + _dot(q_dec[rows, :], st[d].T.astype(BF16))
                st[d] = st[d] * dec[n] + upd[n]
        return [jnp.concatenate(o, axis=0) if o[0] is not None else None for o in outs], st

    lb_d = (lb[0:1, :], lb[1:2, :])
    cum = (cumf_ref, cumb_ref)
    f_lat = (ff_ref, fb_ref)
    o_scr = (of_scr, ob_scr)

    cv = ci_ref[0]
    ctx_chains = []
    for d, f_ref in enumerate((cff_ref, cfb_ref)):
        key, parts = gates(f_ref[0], lb_d[d])
        b, b_tot, dec = decays(parts, cum[d], d == 1)
        ctx_chains.append((dec, increments(key, b, b_tot, cv), None, None))
    _, st = recur_pair([jnp.zeros((HG_DV, HG_DK), F32)] * 2, ctx_chains)

    blocks_per_step = 2

    def lat_body(r, carry):
        chains = []
        for u in range(blocks_per_step):
            up = r * blocks_per_step + u
            chains.append((0, pl.ds(pl.multiple_of(up * blk, blk), blk)))
            chains.append((1, pl.ds(pl.multiple_of((n_blk - 1 - up) * blk, blk), blk)))
        g = [gates(f_lat[d][0, sl, :], lb_d[d]) for d, sl in chains]
        dk = [decays(g[i][1], cum[d], d == 1) for i, (d, sl) in enumerate(chains)]
        v = [i_ref[0, sl, :] for d, sl in chains]
        upd = [increments(g[i][0], dk[i][0], dk[i][1], v[i]) for i in range(len(chains))]
        qs = [scores(q_ref[0, sl, :], g[i][0], dk[i][0]) for i, (d, sl) in enumerate(chains)]
        o_in = [intra(qs[i][1], v[i], cum[d]) for i, (d, sl) in enumerate(chains)]
        st = list(carry)
        for u in range(blocks_per_step):
            pair = range(2 * u, 2 * u + 2)
            outs, st = recur_pair(st, [(dk[i][2], upd[i], qs[i][0], o_in[i]) for i in pair])
            for i in pair:
                d, sl = chains[i]
                o_scr[d][sl, :] = outs[d]
        return tuple(st)

    lax.fori_loop(0, n_blk // blocks_per_step, lat_body, tuple(st))

    ng = ng_ref[...]

    def fin_body(r, carry):
        sl = pl.ds(pl.multiple_of(r * blk, blk), blk)
        o = of_scr[sl, :] + ob_scr[sl, :]
        y = o * lax.rsqrt(jnp.mean(o * o, axis=-1, keepdims=True) + RMS_EPS) * ng
        o_ref[0, sl, :] = (y * _silu(g_ref[0, sl, :].astype(F32))).astype(o_ref.dtype)
        return carry

    lax.fori_loop(0, n_blk, fin_body, 0, unroll=4)


def _hgrn_mixer(p_lat, p_ctx, ctx_tiles, hg_lb, norm_g, c_rows, w_mod, b_mod, mod_col0, cast_weights):
    bsz, t_len, _ = p_lat.shape
    l_len = p_ctx.shape[1]
    w = HG_DK
    d, n_mod = w_mod.shape
    n_steps = bsz * HG_HEADS
    mod_tn = (n_mod - mod_col0) // n_steps
    assert mod_tn % 128 == 0 and mod_col0 % mod_tn == 0 and mod_tn * n_steps == n_mod - mod_col0
    assert all(m.shape[0] % (n_steps * CAST_ROWS) == 0 for m in cast_weights)

    def slab(m):
        return pl.BlockSpec((m.shape[0] // n_steps, m.shape[1]), lambda b, h: (b * HG_HEADS + h, 0))

    def lat(group):
        return pl.BlockSpec((1, t_len, w), lambda b, h: (b, 0, group * HG_HEADS + h))

    def ctx(group):
        blk0 = _packed_col(ctx_tiles, group * HG_HEADS * w) // w
        return pl.BlockSpec((1, l_len, w), lambda b, h: (b, 0, blk0 + h))

    def const(shape):
        return pl.BlockSpec(shape, lambda b, h: (0,) * len(shape))

    lower, upper, same = _chunk_masks()
    diag4 = np.tile(same[:128, ::HG_CHUNK][:, :4, None], (1, 1, HG_DK)).reshape(128, 4 * HG_DK)
    seq_f32 = pltpu.VMEM((t_len, w), F32)
    return pl.pallas_call(
        functools.partial(_hgrn_kernel, n_cast=len(cast_weights)),
        grid=(bsz, HG_HEADS),
        in_specs=[
            lat(0), lat(1), lat(2), lat(3), lat(4),
            ctx(1), ctx(2), ctx(3),
            pl.BlockSpec((hg_lb.shape[0], 2, w), lambda b, h: (0, 0, h)),
            const((1, HG_DV)),
            const((HG_BLOCK, HG_BLOCK)), const((HG_BLOCK, HG_BLOCK)),
            const((128, 4 * HG_DK)),
            const((MOD_ROWS, d)),
            pl.BlockSpec((d, mod_tn), lambda b, h: (0, mod_col0 // mod_tn + b * HG_HEADS + h)),
            pl.BlockSpec((1, mod_tn), lambda b, h: (0, mod_col0 // mod_tn + b * HG_HEADS + h)),
        ] + [slab(m) for m in cast_weights],
        out_specs=[pl.BlockSpec((1, t_len, HG_DV), lambda b, h: (b, 0, h)),
                   pl.BlockSpec((MOD_ROWS, mod_tn), lambda b, h: (0, b * HG_HEADS + h))]
        + [slab(m) for m in cast_weights],
        out_shape=[jax.ShapeDtypeStruct((bsz, t_len, HG_WIDTH), BF16),
                   jax.ShapeDtypeStruct((MOD_ROWS, n_mod - mod_col0), F32)]
        + [jax.ShapeDtypeStruct(m.shape, BF16) for m in cast_weights],
        scratch_shapes=[seq_f32, seq_f32],
        compiler_params=_params("parallel", "parallel"),
        name="hgrn2_mixer",
    )(p_lat, p_lat, p_lat, p_lat, p_lat, p_ctx, p_ctx, p_ctx, hg_lb, norm_g.reshape(1, HG_DV),
      jnp.asarray(lower, BF16), jnp.asarray(upper, BF16), jnp.asarray(diag4, BF16),
      c_rows, w_mod, b_mod.reshape(1, n_mod), *cast_weights)


ATT_QROWS = 8 * ATT_BLOCK


def _attn_kernel(q_ref, kvp_ref, kvc_ref, kvn_ref, ckv_ref, sink_ref, o_ref):
    n = pl.program_id(1)
    n_steps = pl.num_programs(1)
    rows = q_ref.shape[1]
    blk = ATT_BLOCK
    hd = HEAD_DIM
    l_len = ckv_ref.shape[1]
    band = 3 * blk
    n_keys = band + l_len
    ones = jnp.ones((n_keys, hd), BF16)

    def band_rows(cols):
        return jnp.concatenate([kvp_ref[0, rows - blk:, cols], kvc_ref[0, :, cols], kvn_ref[0, :blk, cols]], axis=0)

    kv_slabs = [band_rows(slice(j * hd, (j + 1) * hd)) for j in range(2 * ATT_KV_HEADS)]
    ctx_slabs = [ckv_ref[0, :, j * hd:(j + 1) * hd] for j in range(2 * ATT_KV_HEADS)]
    qi = lax.broadcasted_iota(jnp.int32, (blk, n_keys), 0)
    kj = lax.broadcasted_iota(jnp.int32, (blk, n_keys), 1)
    in_window = jnp.abs(kj - blk - qi) <= WINDOW

    n_sub = rows // blk
    window_bias = jnp.where((kj >= band) | in_window, 0.0, NEG_INF)
    for sub in range(n_sub):
        lo = sub * blk
        if sub in (0, n_sub - 1):
            k_pos = n * rows + lo - blk + kj
            ok = (kj >= band) | (in_window & (k_pos >= 0) & (k_pos < n_steps * rows))
            bias = jnp.where(ok, 0.0, NEG_INF)
        else:
            bias = window_bias
        k_all = [jnp.concatenate([kv_slabs[j][lo:lo + band], ctx_slabs[j]], axis=0) for j in range(ATT_KV_HEADS)]
        v_aug = [jnp.concatenate([jnp.concatenate([kv_slabs[ATT_KV_HEADS + j][lo:lo + band],
                                                   ctx_slabs[ATT_KV_HEADS + j]], axis=0), ones], axis=1)
                 for j in range(ATT_KV_HEADS)]
        heads = range(ATT_HEADS)
        s = [lax.dot_general(q_ref[0, lo:lo + blk, h * hd:(h + 1) * hd], k_all[h // ATT_GROUP], _NT,
                             preferred_element_type=F32) + bias for h in heads]
        sink = [sink_ref[:, h:h + 1] for h in heads]
        m = [jnp.maximum(jnp.max(s[h], axis=-1, keepdims=True), sink[h]) for h in heads]
        e = [jnp.exp(s[h] - m[h]).astype(BF16) for h in heads]
        oa = [_dot(e[h], v_aug[h // ATT_GROUP]) for h in heads]
        for h in heads:
            o = oa[h][:, :hd] / (oa[h][:, hd:] + jnp.exp(sink[h] - m[h]))
            o_ref[0, lo:lo + blk, h * hd:(h + 1) * hd] = o.astype(o_ref.dtype)


def _window_attention(p_lat, p_ctx, ctx_tiles, sink):
    bsz, t_len, _ = p_lat.shape
    l_len = p_ctx.shape[1]
    nb = t_len // ATT_QROWS
    kvw = 2 * ATT_KV_HEADS * HEAD_DIM
    q_blk = HG_COLS // ATT_WIDTH
    kv_blk = (HG_COLS + ATT_WIDTH) // kvw
    ckv_blk = _packed_col(ctx_tiles, HG_COLS + ATT_WIDTH) // kvw

    def prev(n):
        return jnp.maximum(n - 1, 0)

    def nxt(n):
        return jnp.minimum(n + 1, nb - 1)

    def kv_spec(shift):
        return pl.BlockSpec((1, ATT_QROWS, kvw), lambda b, n: (b, shift(n), kv_blk))

    same = lambda n: n
    return pl.pallas_call(
        _attn_kernel,
        grid=(bsz, nb),
        in_specs=[
            pl.BlockSpec((1, ATT_QROWS, ATT_WIDTH), lambda b, n: (b, n, q_blk)),
            kv_spec(prev), kv_spec(same), kv_spec(nxt),
            pl.BlockSpec((1, l_len, kvw), lambda b, n: (b, 0, ckv_blk)),
            pl.BlockSpec((1, ATT_HEADS), lambda b, n: (0, 0)),
        ],
        out_specs=pl.BlockSpec((1, ATT_QROWS, ATT_WIDTH), lambda b, n: (b, n, 0)),
        out_shape=jax.ShapeDtypeStruct((bsz, t_len, ATT_WIDTH), BF16),
        compiler_params=_params("parallel", "parallel"),
        name="window_gqa",
    )(p_lat, p_lat, p_lat, p_lat, p_ctx, sink.reshape(1, ATT_HEADS))


def _outproj_kernel(x_ref, mod_ref, hg_ref, at_ref, w_ref, o_ref, *, tiles_per_mod):
    kh = hg_ref.shape[1]
    gate = _mod_vector(mod_ref, pl.program_id(0) // tiles_per_mod, 2 - N_MOD_EARLY, x_ref.shape[1])
    acc = _dot(hg_ref[...], w_ref[:kh, :]) + _dot(at_ref[...], w_ref[kh:, :])
    o_ref[...] = x_ref[...] + gate * acc


def _out_projection(x2d, mod, hg2d, at2d, w_out, tokens_per_mod, tm=512):
    m, d = x2d.shape
    kh = hg2d.shape[1]
    return pl.pallas_call(
        functools.partial(_outproj_kernel, tiles_per_mod=tokens_per_mod // tm),
        grid=(m // tm,),
        in_specs=[
            pl.BlockSpec((tm, d), lambda i: (i, 0)),
            pl.BlockSpec(mod.shape, lambda i: (0, 0)),
            pl.BlockSpec((tm, kh), lambda i: (i, 0)),
            pl.BlockSpec((tm, kh), lambda i: (i, 0)),
            pl.BlockSpec(w_out.shape, lambda i: (0, 0), pipeline_mode=pl.Buffered(1)),
        ],
        out_specs=pl.BlockSpec((tm, d), lambda i: (i, 0)),
        out_shape=jax.ShapeDtypeStruct((m, d), F32),
        compiler_params=_params("parallel"),
        name="out_proj",
    )(x2d, mod, hg2d, at2d, w_out)


def _ffn_kernel(x_ref, mod_ref, g_ref, wg_ref, wu_ref, wd_ref, o_ref, h_scr, *, tiles_per_mod):
    j = pl.program_id(1)
    d = x_ref.shape[1]
    shift, scale, out_gate = (_mod_vector(mod_ref, pl.program_id(0) // tiles_per_mod, k - N_MOD_EARLY, d)
                              for k in (3, 4, 5))

    @pl.when(j == 0)
    def _():
        _norm_modulate_into(x_ref, g_ref, shift, scale, h_scr)

    def gated_down():
        h = h_scr[...]
        gate = _dot(h, wg_ref[...])
        up = _dot(h, wu_ref[...])
        act = (_silu(gate) * up).astype(BF16)
        return out_gate * _dot(act, wd_ref[...])

    @pl.when(j == 0)
    def _():
        o_ref[...] = x_ref[...] + gated_down()

    @pl.when(j > 0)
    def _():
        o_ref[...] += gated_down()


def _ffn(x2d, mod, norm_g, w_gate_up, w_down, tokens_per_mod, tm=1024, tf=512):
    m, d = x2d.shape
    d_ff = w_down.shape[0]
    nf = d_ff // tf
    return pl.pallas_call(
        functools.partial(_ffn_kernel, tiles_per_mod=tokens_per_mod // tm),
        grid=(m // tm, nf),
        in_specs=[
            pl.BlockSpec((tm, d), lambda i, j: (i, 0)),
            pl.BlockSpec(mod.shape, lambda i, j: (0, 0)),
            pl.BlockSpec((1, d), lambda i, j: (0, 0)),
            pl.BlockSpec((d, tf), lambda i, j: (0, j)),
            pl.BlockSpec((d, tf), lambda i, j: (0, nf + j)),
            pl.BlockSpec((tf, d), lambda i, j: (j, 0)),
        ],
        out_specs=pl.BlockSpec((tm, d), lambda i, j: (i, 0)),
        out_shape=jax.ShapeDtypeStruct((m, d), F32),
        scratch_shapes=[pltpu.VMEM((tm, d), BF16)],
        compiler_params=_params("parallel", "arbitrary"),
        name="swiglu_ffn",
    )(x2d, mod, norm_g.reshape(1, d), w_gate_up, w_gate_up, w_down)


def kernel(x, c, ctx, c_ctx, w_mod, b_mod, norm_mix_g, norm_ffn_g, w_in, hg_lb, hg_norm_g,
           q_norm_g, k_norm_g, attn_sink, w_out, w_gate_up, w_down):
    bsz, t_len, d = x.shape
    l_len = ctx.shape[1]
    depth = w_mod.shape[0]
    assert depth == 1, "single-layer block: the context stream never needs its own outputs"
    assert bsz + 1 <= MOD_ROWS

    c_rows = jnp.concatenate([c, c_ctx[None, :], jnp.zeros((MOD_ROWS - bsz - 1, d), F32)], axis=0)
    assert w_mod.shape[-1] == N_MOD * d
    n_early = N_MOD_EARLY * d
    mod_early = _modulation(c_rows, w_mod[0], b_mod[0], n_early)

    cos, sin = _rope_tables(t_len)
    hk = HG_HEADS * HG_DK
    ctx_tiles = _tiles_covering([(hk, 3 * hk + HG_WIDTH), (HG_COLS + ATT_WIDTH, IN_COLS)])
    p_lat, p_ctx = _in_projection(x.reshape(bsz * t_len, d), ctx.reshape(bsz * l_len, d), mod_early, norm_mix_g[0],
                                  w_in[0], q_norm_g[0], k_norm_g[0], cos, sin, tokens_per_mod=t_len, ctx_mod_row=bsz,
                                  tm=2048, ctx_tiles=ctx_tiles)
    p_lat = p_lat.reshape(bsz, t_len, -1)
    p_ctx = p_ctx.reshape(bsz, l_len, -1)

    hg, mod_late, w_out16, w_gate_up16, w_down16 = _hgrn_mixer(
        p_lat, p_ctx, ctx_tiles, hg_lb, hg_norm_g[0], c_rows, w_mod[0], b_mod[0], n_early,
        cast_weights=(w_out[0], w_gate_up[0], w_down[0]))
    at = _window_attention(p_lat, p_ctx, ctx_tiles, attn_sink[0])

    x1 = _out_projection(x.reshape(bsz * t_len, d), mod_late, hg.reshape(bsz * t_len, HG_WIDTH),
                         at.reshape(bsz * t_len, ATT_WIDTH), w_out16, tokens_per_mod=t_len)
    x2 = _ffn(x1, mod_late, norm_ffn_g[0], w_gate_up16, w_down16, tokens_per_mod=t_len)
    return x2.reshape(bsz, t_len, d)
```

```python
import functools

import jax
import jax.numpy as jnp
import numpy as np
from jax import lax
from jax.experimental import pallas as pl
from jax.experimental.pallas import tpu as pltpu

F32 = jnp.float32
BF16 = jnp.bfloat16

GRID_W = 64
HG_HEADS = 8
HG_DK = 128
HG_DV = 128
HG_CHUNK = 32
ATT_HEADS = 8
ATT_KV_HEADS = 2
ATT_GROUP = ATT_HEADS // ATT_KV_HEADS
HEAD_DIM = 128
WINDOW = 128
ATT_BLOCK = 128
ROPE_THETA = 10000.0
N_MOD = 6
RMS_EPS = 1e-6
NEG_INF = -1e30
HG_WIDTH = HG_HEADS * HG_DV
ATT_WIDTH = ATT_HEADS * HEAD_DIM
HG_COLS = 3 * HG_HEADS * HG_DK + 2 * HG_HEADS * HG_DV
ATT_COLS = (ATT_HEADS + 2 * ATT_KV_HEADS) * HEAD_DIM
IN_COLS = HG_COLS + ATT_COLS

VMEM_LIMIT_BYTES = 58 * 1024 * 1024
MOD_ROWS = 8
N_MOD_EARLY = 2

_NT = (((1,), (1,)), ((), ()))


def _dot(a, b):
    return jnp.dot(a, b, preferred_element_type=F32)


def _silu(x):
    h = 0.5 * x
    return h * jnp.tanh(h) + h


def _params(*sem):
    return pltpu.CompilerParams(dimension_semantics=sem, vmem_limit_bytes=VMEM_LIMIT_BYTES)


def _mod_vector(mod_ref, row, k, d):
    return mod_ref[pl.ds(row, 1), pl.ds(k * d, d)]


MOD_RING = 4


def _mod_kernel(c_ref, w_hbm, b_ref, o_ref, w_buf, w_sem, *, tn):
    n_chunks = o_ref.shape[1] // tn
    a = _silu(c_ref[...]).astype(BF16)

    def w_copy(c):
        slot = c % MOD_RING
        return pltpu.make_async_copy(w_hbm.at[:, pl.ds(c * tn, tn)], w_buf.at[slot], w_sem.at[slot])

    for c in range(min(MOD_RING, n_chunks)):
        w_copy(c).start()
    for c in range(n_chunks):
        w_copy(c).wait()
        cols = slice(c * tn, (c + 1) * tn)
        o_ref[:, cols] = _dot(a, w_buf[c % MOD_RING].astype(BF16)) + b_ref[:, cols]
        if c + MOD_RING < n_chunks:
            w_copy(c + MOD_RING).start()


def _modulation(c_rows, w_mod, b_mod, n_cols, tn=512):
    d, n = w_mod.shape
    assert n_cols % tn == 0
    whole = pl.BlockSpec(memory_space=pltpu.VMEM)
    return pl.pallas_call(
        functools.partial(_mod_kernel, tn=tn),
        in_specs=[whole, pl.BlockSpec(memory_space=pl.ANY), whole],
        out_specs=whole,
        out_shape=jax.ShapeDtypeStruct((MOD_ROWS, n_cols), F32),
        scratch_shapes=[pltpu.VMEM((MOD_RING, d, tn), F32), pltpu.SemaphoreType.DMA((MOD_RING,))],
        compiler_params=_params(),
        name="adaln_mod",
    )(c_rows, w_mod, b_mod.reshape(1, n))


HEAD_PAIR = 2 * HEAD_DIM


def _rope_tables(t_len):
    half = HEAD_DIM // 2
    nf = half // 2
    pos = np.arange(t_len)
    inv_freq = ROPE_THETA ** (-np.arange(nf, dtype=np.float64) / nf)
    ang_r = (pos // GRID_W)[:, None] * inv_freq
    ang_c = (pos % GRID_W)[:, None] * inv_freq
    cos = np.concatenate([np.cos(ang_r)] * 2 + [np.cos(ang_c)] * 2, axis=-1)
    sin = np.concatenate([-np.sin(ang_r), np.sin(ang_r), -np.sin(ang_c), np.sin(ang_c)], axis=-1)
    pair = lambda a: jnp.asarray(np.tile(a, (1, HEAD_PAIR // HEAD_DIM)), F32)
    return pair(cos), pair(sin)


def _pair_matrices():
    quarter = HEAD_DIM // 4
    lane = np.arange(HEAD_PAIR)
    src = np.where(lane % (2 * quarter) < quarter, lane + quarter, lane - quarter)
    swap = lane[:, None] == src[None, :]
    head_mean = ((lane[:, None] // HEAD_DIM) == (lane[None, :] // HEAD_DIM)) / HEAD_DIM
    return jnp.asarray(swap, BF16), jnp.asarray(head_mean, BF16)


def _norm_modulate_into(x_ref, g_ref, shift, scale, h_scr, rows=128):
    tm = x_ref.shape[0]
    gain = g_ref[...] * (1.0 + scale)

    def body(r, carry):
        sl = pl.ds(pl.multiple_of(r * rows, rows), rows)
        x = x_ref[sl, :]
        ms = jnp.mean(x * x, axis=-1, keepdims=True)
        h_scr[sl, :] = (x * lax.rsqrt(ms + RMS_EPS) * gain + shift).astype(BF16)
        return carry

    lax.fori_loop(0, tm // rows, body, 0, unroll=2)


def _inproj_kernel(slot_ref, isctx_ref, x_hbm, cx_hbm, mod_ref, g_ref, w_ref, cos_ref, sin_ref, swap_ref, hmean_ref,
                   qg_ref, kg_ref, o_ref, oc_ref, x_buf, x_sem, h_scr, hc_scr, *, ctx_step0, tiles_per_mod, ctx_mod_row):
    i = pl.program_id(0)
    t = pl.program_id(1)
    last = i == pl.num_programs(0) - 1
    ctx_step = last & (isctx_ref[t] == 1)
    tm, d = x_buf.shape
    cx_buf = x_buf.at[pl.ds(0, hc_scr.shape[0]), :]
    tn = o_ref.shape[1]
    hd = HEAD_DIM
    q_tile0 = HG_COLS // tn
    kv_tile = (HG_COLS + ATT_WIDTH) // tn

    def x_copy(row_tile):
        return pltpu.make_async_copy(x_hbm.at[pl.ds(pl.multiple_of(row_tile * tm, tm), tm), :], x_buf, x_sem)

    def cx_copy():
        return pltpu.make_async_copy(cx_hbm, cx_buf, x_sem)

    @pl.when(t == 0)
    def _():
        @pl.when(i == 0)
        def _():
            x_copy(i).start()

        x_copy(i).wait()
        row = i // tiles_per_mod
        _norm_modulate_into(x_buf, g_ref, _mod_vector(mod_ref, row, 0, d), _mod_vector(mod_ref, row, 1, d), h_scr)

    @pl.when(t == 1)
    def _():
        @pl.when(jnp.logical_not(last))
        def _():
            x_copy(i + 1).start()

        @pl.when(last)
        def _():
            cx_copy().start()

    @pl.when(last & (t == ctx_step0))
    def _():
        cx_copy().wait()
        _norm_modulate_into(cx_buf, g_ref, _mod_vector(mod_ref, ctx_mod_row, 0, d),
                            _mod_vector(mod_ref, ctx_mod_row, 1, d), hc_scr)

    def project(h_ref):
        return _dot(h_ref[...], w_ref[...].astype(BF16))

    def qk_prep(acc, n_heads, g_row, scale, o_ref, rope):
        cols = [slice(p * HEAD_PAIR, (p + 1) * HEAD_PAIR) for p in range(n_heads * hd // HEAD_PAIR)]
        x = [acc[:, c] for c in cols]
        gain = jnp.concatenate([g_row] * (HEAD_PAIR // hd), axis=1) * scale
        ms = [_dot((xh * xh).astype(BF16), hmean_ref[...]) for xh in x]
        y = [xh * lax.rsqrt(m + RMS_EPS) * gain for xh, m in zip(x, ms)]
        if rope:
            swapped = [_dot(yh.astype(BF16), swap_ref[...]) for yh in y]
            y = [yh * cos_ref[...] + sh * sin_ref[...] for yh, sh in zip(y, swapped)]
        for c, yh in zip(cols, y):
            o_ref[:, c] = yh.astype(o_ref.dtype)

    @pl.when(t < q_tile0)
    def _():
        o_ref[...] = project(h_scr).astype(o_ref.dtype)

        @pl.when(ctx_step)
        def _():
            oc_ref[...] = project(hc_scr).astype(oc_ref.dtype)

    @pl.when((t >= q_tile0) & (t < kv_tile))
    def _():
        qk_prep(project(h_scr), tn // hd, qg_ref[...], hd ** -0.5, o_ref, True)

    @pl.when(t == kv_tile)
    def _():
        def keys_values(h_ref, out_ref, rope):
            acc = project(h_ref)
            qk_prep(acc, ATT_KV_HEADS, kg_ref[...], 1.0, out_ref, rope)
            out_ref[:, ATT_KV_HEADS * hd:] = acc[:, ATT_KV_HEADS * hd:].astype(out_ref.dtype)

        keys_values(h_scr, o_ref, True)

        @pl.when(ctx_step)
        def _():
            keys_values(hc_scr, oc_ref, False)


IN_TILE = 512


def _tiles_covering(col_ranges):
    return tuple(t for t in range(IN_COLS // IN_TILE)
                 if any(lo < (t + 1) * IN_TILE and t * IN_TILE < hi for lo, hi in col_ranges))


def _packed_col(col_tiles, col):
    return col_tiles.index(col // IN_TILE) * IN_TILE + col % IN_TILE


def _in_projection(x2d, cx2d, mod, norm_g, w_in, q_norm_g, k_norm_g, cos, sin, tokens_per_mod, ctx_mod_row, tm,
                   ctx_tiles):
    m, d = x2d.shape
    mc = cx2d.shape[0]
    tn = IN_TILE
    n_tiles = IN_COLS // tn
    n_rows = m // tm
    assert HG_COLS % tn == 0 and ATT_WIDTH % tn == 0 and 2 * ATT_KV_HEADS * HEAD_DIM == tn
    assert m % tm == 0 and mc <= tm
    assert all(t < HG_COLS // tn or t >= (HG_COLS + ATT_WIDTH) // tn for t in ctx_tiles)
    assert ctx_tiles[0] >= 2
    is_ctx = [int(t in ctx_tiles) for t in range(n_tiles)]
    slots = [max(sum(is_ctx[:t + 1]) - 1, 0) for t in range(n_tiles)]
    tiles_per_seq = cos.shape[0] // tm
    fixed = lambda *shape: pl.BlockSpec(shape, lambda i, t, slot, flag: (0,) * len(shape))
    tab = pl.BlockSpec((tm, HEAD_PAIR), lambda i, t, slot, flag: (i % tiles_per_seq, 0))
    grid_spec = pltpu.PrefetchScalarGridSpec(
        num_scalar_prefetch=2,
        grid=(n_rows, n_tiles),
        in_specs=[
            pl.BlockSpec(memory_space=pl.ANY),
            pl.BlockSpec(memory_space=pl.ANY),
            fixed(*mod.shape),
            fixed(1, d),
            pl.BlockSpec((d, tn), lambda i, t, slot, flag: (0, t)),
            tab, tab, fixed(HEAD_PAIR, HEAD_PAIR), fixed(HEAD_PAIR, HEAD_PAIR),
            fixed(1, HEAD_DIM), fixed(1, HEAD_DIM),
        ],
        out_specs=[
            pl.BlockSpec((tm, tn), lambda i, t, slot, flag: (i, t)),
            pl.BlockSpec((mc, tn), lambda i, t, slot, flag: (0, jnp.where(i == n_rows - 1, slot[t], 0))),
        ],
        scratch_shapes=[pltpu.VMEM((tm, d), jnp.float32), pltpu.SemaphoreType.DMA(()), pltpu.VMEM((tm, d), BF16),
                        pltpu.VMEM((mc, d), BF16)],
    )
    return pl.pallas_call(
        functools.partial(_inproj_kernel, ctx_step0=ctx_tiles[0], tiles_per_mod=tokens_per_mod // tm,
                          ctx_mod_row=ctx_mod_row),
        grid_spec=grid_spec,
        out_shape=(jax.ShapeDtypeStruct((m, IN_COLS), BF16), jax.ShapeDtypeStruct((mc, len(ctx_tiles) * tn), BF16)),
        compiler_params=_params("arbitrary", "arbitrary"),
        name="in_proj",
    )(jnp.asarray(slots, jnp.int32), jnp.asarray(is_ctx, jnp.int32), x2d, cx2d, mod, norm_g.reshape(1, d), w_in,
      cos, sin, *_pair_matrices(), q_norm_g.reshape(1, HEAD_DIM), k_norm_g.reshape(1, HEAD_DIM))


HG_BLOCK = 256
CAST_ROWS = 16


def _cast_rows(src_ref, dst_ref):
    def body(r, carry):
        sl = pl.ds(pl.multiple_of(r * CAST_ROWS, CAST_ROWS), CAST_ROWS)
        dst_ref[sl, :] = src_ref[sl, :].astype(dst_ref.dtype)
        return carry

    lax.fori_loop(0, src_ref.shape[0] // CAST_ROWS, body, 0)


def _chunk_masks():
    r = np.arange(HG_BLOCK)[:, None]
    c = np.arange(HG_BLOCK)[None, :]
    same = (r // HG_CHUNK) == (c // HG_CHUNK)
    return (same & (r >= c)), (same & (r <= c)), same


def _hgrn_kernel(q_ref, ff_ref, fb_ref, i_ref, g_ref, cff_ref, cfb_ref, ci_ref, lb_ref, ng_ref,
                 cumf_ref, cumb_ref, dmask_ref, c_ref, wm_ref, bm_ref, *rest, n_cast):
    cast_src = rest[:n_cast]
    o_ref, mod_ref = rest[n_cast:n_cast + 2]
    cast_dst = rest[n_cast + 2:2 * n_cast + 2]
    of_scr, ob_scr = rest[2 * n_cast + 2:]

    mod_ref[...] = _dot(_silu(c_ref[...]).astype(BF16), wm_ref[...].astype(BF16)) + bm_ref[...]
    for src, dst in zip(cast_src, cast_dst):
        _cast_rows(src, dst)

    t_len = q_ref.shape[1]
    l_len = ci_ref.shape[1]
    c = HG_CHUNK
    blk = HG_BLOCK
    nch = blk // c
    n_blk = t_len // blk
    assert l_len == blk

    a = lb_ref[...]
    e = jnp.exp(a - jnp.max(a, axis=0, keepdims=True))
    lb = e[0] / jnp.sum(e, axis=0)

    def gates(f_pre, lb_d):
        f = 0.5 * (1.0 + lb_d) + (0.5 * (1.0 - lb_d)) * jnp.tanh(0.5 * f_pre.astype(F32))
        log_f = jnp.log2(f)
        hi = lax.bitcast_convert_type(lax.bitcast_convert_type(log_f, jnp.uint32) & jnp.uint32(0xFFFF0000), F32)
        return 1.0 - f, jnp.concatenate([hi.astype(BF16), (log_f - hi).astype(BF16)], axis=1)

    def decays(parts, cum_ref, reverse):
        b = _dot(cum_ref[...], parts)
        b = b[:, 0:128] + b[:, 128:256]
        b3 = b.reshape(nch, c, HG_DK)
        tot = b3[:, 0:1, :] if reverse else b3[:, c - 1:c, :]
        b_tot = jnp.broadcast_to(tot, (nch, c, HG_DK)).reshape(blk, HG_DK)
        return b, b_tot, jnp.exp2(tot)

    def increments(key, b, b_tot, v):
        k_dec = (key * jnp.exp2(b_tot - b)).astype(BF16)
        v_t = v.astype(BF16).T
        upd = []
        for h in range(blk // 128):
            kd = k_dec[h * 128:(h + 1) * 128, :]
            kd4 = jnp.concatenate([kd] * 4, axis=1) * dmask_ref[...]
            u4 = _dot(v_t[:, h * 128:(h + 1) * 128], kd4)
            upd += [u4[:, j * HG_DK:(j + 1) * HG_DK] for j in range(4)]
        return upd

    def scores(q_pre, key, b):
        q_dec = (_silu(q_pre.astype(F32)) * jnp.exp2(b)).astype(BF16)
        k_inv = (key * jnp.exp2(-b)).astype(BF16)
        return q_dec, lax.dot_general(q_dec, k_inv, _NT, preferred_element_type=F32)

    def intra(s, v, cum_ref):
        return _dot(s.astype(BF16) * cum_ref[...], v.astype(BF16))

    def recur_pair(st, chains):
        outs = [[None] * nch for _ in chains]
        st = list(st)
        for step in range(nch):
            for d, (dec, upd, q_dec, o_intra) in enumerate(chains):
                n = nch - 1 - step if d == 1 else step
                if q_dec is not None:
                    rows = slice(n * c, (n + 1) * c)
                    outs[d][n] = o_intra[rows, :] + _dot(q_dec[rows, :], st[d].T.astype(BF16))
                st[d] = st[d] * dec[n] + upd[n]
        return [jnp.concatenate(o, axis=0) if o[0] is not None else None for o in outs], st

    lb_d = (lb[0:1, :], lb[1:2, :])
    cum = (cumf_ref, cumb_ref)
    f_lat = (ff_ref, fb_ref)
    o_scr = (of_scr, ob_scr)

    cv = ci_ref[0]
    ctx_chains = []
    for d, f_ref in enumerate((cff_ref, cfb_ref)):
        key, parts = gates(f_ref[0], lb_d[d])
        b, b_tot, dec = decays(parts, cum[d], d == 1)
        ctx_chains.append((dec, increments(key, b, b_tot, cv), None, None))
    _, st = recur_pair([jnp.zeros((HG_DV, HG_DK), F32)] * 2, ctx_chains)

    blocks_per_step = 2

    def lat_body(r, carry):
        chains = []
        for u in range(blocks_per_step):
            up = r * blocks_per_step + u
            chains.append((0, pl.ds(pl.multiple_of(up * blk, blk), blk)))
            chains.append((1, pl.ds(pl.multiple_of((n_blk - 1 - up) * blk, blk), blk)))
        g = [gates(f_lat[d][0, sl, :], lb_d[d]) for d, sl in chains]
        dk = [decays(g[i][1], cum[d], d == 1) for i, (d, sl) in enumerate(chains)]
        v = [i_ref[0, sl, :] for d, sl in chains]
        upd = [increments(g[i][0], dk[i][0], dk[i][1], v[i]) for i in range(len(chains))]
        qs = [scores(q_ref[0, sl, :], g[i][0], dk[i][0]) for i, (d, sl) in enumerate(chains)]
        o_in = [intra(qs[i][1], v[i], cum[d]) for i, (d, sl) in enumerate(chains)]
        st = list(carry)
        for u in range(blocks_per_step):
            pair = range(2 * u, 2 * u + 2)
            outs, st = recur_pair(st, [(dk[i][2], upd[i], qs[i][0], o_in[i]) for i in pair])
            for i in pair:
                d, sl = chains[i]
                o_scr[d][sl, :] = outs[d]
        return tuple(st)

    lax.fori_loop(0, n_blk // blocks_per_step, lat_body, tuple(st))

    ng = ng_ref[...]

    def fin_body(r, carry):
        sl = pl.ds(pl.multiple_of(r * blk, blk), blk)
        o = of_scr[sl, :] + ob_scr[sl, :]
        y = o * lax.rsqrt(jnp.mean(o * o, axis=-1, keepdims=True) + RMS_EPS) * ng
        o_ref[0, sl, :] = (y * _silu(g_ref[0, sl, :].astype(F32))).astype(o_ref.dtype)
        return carry

    lax.fori_loop(0, n_blk, fin_body, 0, unroll=4)


def _hgrn_mixer(p_lat, p_ctx, ctx_tiles, hg_lb, norm_g, c_rows, w_mod, b_mod, mod_col0, cast_weights):
    bsz, t_len, _ = p_lat.shape
    l_len = p_ctx.shape[1]
    w = HG_DK
    d, n_mod = w_mod.shape
    n_steps = bsz * HG_HEADS
    mod_tn = (n_mod - mod_col0) // n_steps
    assert mod_tn % 128 == 0 and mod_col0 % mod_tn == 0 and mod_tn * n_steps == n_mod - mod_col0
    assert all(m.shape[0] % (n_steps * CAST_ROWS) == 0 for m in cast_weights)

    def slab(m):
        return pl.BlockSpec((m.shape[0] // n_steps, m.shape[1]), lambda b, h: (b * HG_HEADS + h, 0))

    def lat(group):
        return pl.BlockSpec((1, t_len, w), lambda b, h: (b, 0, group * HG_HEADS + h))

    def ctx(group):
        blk0 = _packed_col(ctx_tiles, group * HG_HEADS * w) // w
        return pl.BlockSpec((1, l_len, w), lambda b, h: (b, 0, blk0 + h))

    def const(shape):
        return pl.BlockSpec(shape, lambda b, h: (0,) * len(shape))

    lower, upper, same = _chunk_masks()
    diag4 = np.tile(same[:128, ::HG_CHUNK][:, :4, None], (1, 1, HG_DK)).reshape(128, 4 * HG_DK)
    seq_f32 = pltpu.VMEM((t_len, w), F32)
    return pl.pallas_call(
        functools.partial(_hgrn_kernel, n_cast=len(cast_weights)),
        grid=(bsz, HG_HEADS),
        in_specs=[
            lat(0), lat(1), lat(2), lat(3), lat(4),
            ctx(1), ctx(2), ctx(3),
            pl.BlockSpec((hg_lb.shape[0], 2, w), lambda b, h: (0, 0, h)),
            const((1, HG_DV)),
            const((HG_BLOCK, HG_BLOCK)), const((HG_BLOCK, HG_BLOCK)),
            const((128, 4 * HG_DK)),
            const((MOD_ROWS, d)),
            pl.BlockSpec((d, mod_tn), lambda b, h: (0, mod_col0 // mod_tn + b * HG_HEADS + h)),
            pl.BlockSpec((1, mod_tn), lambda b, h: (0, mod_col0 // mod_tn + b * HG_HEADS + h)),
        ] + [slab(m) for m in cast_weights],
        out_specs=[pl.BlockSpec((1, t_len, HG_DV), lambda b, h: (b, 0, h)),
                   pl.BlockSpec((MOD_ROWS, mod_tn), lambda b, h: (0, b * HG_HEADS + h))]
        + [slab(m) for m in cast_weights],
        out_shape=[jax.ShapeDtypeStruct((bsz, t_len, HG_WIDTH), BF16),
                   jax.ShapeDtypeStruct((MOD_ROWS, n_mod - mod_col0), F32)]
        + [jax.ShapeDtypeStruct(m.shape, BF16) for m in cast_weights],
        scratch_shapes=[seq_f32, seq_f32],
        compiler_params=_params("parallel", "parallel"),
        name="hgrn2_mixer",
    )(p_lat, p_lat, p_lat, p_lat, p_lat, p_ctx, p_ctx, p_ctx, hg_lb, norm_g.reshape(1, HG_DV),
      jnp.asarray(lower, BF16), jnp.asarray(upper, BF16), jnp.asarray(diag4, BF16),
      c_rows, w_mod, b_mod.reshape(1, n_mod), *cast_weights)


ATT_QROWS = 8 * ATT_BLOCK


def _attn_kernel(q_ref, kvp_ref, kvc_ref, kvn_ref, ckv_ref, sink_ref, o_ref):
    n = pl.program_id(1)
    n_steps = pl.num_programs(1)
    rows = q_ref.shape[1]
    blk = ATT_BLOCK
    hd = HEAD_DIM
    l_len = ckv_ref.shape[1]
    band = 3 * blk
    n_keys = band + l_len
    ones = jnp.ones((n_keys, hd), BF16)

    def band_rows(cols):
        return jnp.concatenate([kvp_ref[0, rows - blk:, cols], kvc_ref[0, :, cols], kvn_ref[0, :blk, cols]], axis=0)

    kv_slabs = [band_rows(slice(j * hd, (j + 1) * hd)) for j in range(2 * ATT_KV_HEADS)]
    ctx_slabs = [ckv_ref[0, :, j * hd:(j + 1) * hd] for j in range(2 * ATT_KV_HEADS)]
    qi = lax.broadcasted_iota(jnp.int32, (blk, n_keys), 0)
    kj = lax.broadcasted_iota(jnp.int32, (blk, n_keys), 1)
    in_window = jnp.abs(kj - blk - qi) <= WINDOW

    n_sub = rows // blk
    window_bias = jnp.where((kj >= band) | in_window, 0.0, NEG_INF)
    for sub in range(n_sub):
        lo = sub * blk
        if sub in (0, n_sub - 1):
            k_pos = n * rows + lo - blk + kj
            ok = (kj >= band) | (in_window & (k_pos >= 0) & (k_pos < n_steps * rows))
            bias = jnp.where(ok, 0.0, NEG_INF)
        else:
            bias = window_bias
        k_all = [jnp.concatenate([kv_slabs[j][lo:lo + band], ctx_slabs[j]], axis=0) for j in range(ATT_KV_HEADS)]
        v_aug = [jnp.concatenate([jnp.concatenate([kv_slabs[ATT_KV_HEADS + j][lo:lo + band],
                                                   ctx_slabs[ATT_KV_HEADS + j]], axis=0), ones], axis=1)
                 for j in range(ATT_KV_HEADS)]
        heads = range(ATT_HEADS)
        s = [lax.dot_general(q_ref[0, lo:lo + blk, h * hd:(h + 1) * hd], k_all[h // ATT_GROUP], _NT,
                             preferred_element_type=F32) + bias for h in heads]
        sink = [sink_ref[:, h:h + 1] for h in heads]
        m = [jnp.maximum(jnp.max(s[h], axis=-1, keepdims=True), sink[h]) for h in heads]
        e = [jnp.exp(s[h] - m[h]).astype(BF16) for h in heads]
        oa = [_dot(e[h], v_aug[h // ATT_GROUP]) for h in heads]
        for h in heads:
            o = oa[h][:, :hd] / (oa[h][:, hd:] + jnp.exp(sink[h] - m[h]))
            o_ref[0, lo:lo + blk, h * hd:(h + 1) * hd] = o.astype(o_ref.dtype)


def _window_attention(p_lat, p_ctx, ctx_tiles, sink):
    bsz, t_len, _ = p_lat.shape
    l_len = p_ctx.shape[1]
    nb = t_len // ATT_QROWS
    kvw = 2 * ATT_KV_HEADS * HEAD_DIM
    q_blk = HG_COLS // ATT_WIDTH
    kv_blk = (HG_COLS + ATT_WIDTH) // kvw
    ckv_blk = _packed_col(ctx_tiles, HG_COLS + ATT_WIDTH) // kvw

    def prev(n):
        return jnp.maximum(n - 1, 0)

    def nxt(n):
        return jnp.minimum(n + 1, nb - 1)

    def kv_spec(shift):
        return pl.BlockSpec((1, ATT_QROWS, kvw), lambda b, n: (b, shift(n), kv_blk))

    same = lambda n: n
    return pl.pallas_call(
        _attn_kernel,
        grid=(bsz, nb),
        in_specs=[
            pl.BlockSpec((1, ATT_QROWS, ATT_WIDTH), lambda b, n: (b, n, q_blk)),
            kv_spec(prev), kv_spec(same), kv_spec(nxt),
            pl.BlockSpec((1, l_len, kvw), lambda b, n: (b, 0, ckv_blk)),
            pl.BlockSpec((1, ATT_HEADS), lambda b, n: (0, 0)),
        ],
        out_specs=pl.BlockSpec((1, ATT_QROWS, ATT_WIDTH), lambda b, n: (b, n, 0)),
        out_shape=jax.ShapeDtypeStruct((bsz, t_len, ATT_WIDTH), BF16),
        compiler_params=_params("parallel", "parallel"),
        name="window_gqa",
    )(p_lat, p_lat, p_lat, p_lat, p_ctx, sink.reshape(1, ATT_HEADS))


def _outproj_kernel(x_ref, mod_ref, hg_ref, at_ref, w_ref, o_ref, *, tiles_per_mod):
    kh = hg_ref.shape[1]
    gate = _mod_vector(mod_ref, pl.program_id(0) // tiles_per_mod, 2 - N_MOD_EARLY, x_ref.shape[1])
    acc = _dot(hg_ref[...], w_ref[:kh, :]) + _dot(at_ref[...], w_ref[kh:, :])
    o_ref[...] = x_ref[...] + gate * acc


def _out_projection(x2d, mod, hg2d, at2d, w_out, tokens_per_mod, tm=512):
    m, d = x2d.shape
    kh = hg2d.shape[1]
    return pl.pallas_call(
        functools.partial(_outproj_kernel, tiles_per_mod=tokens_per_mod // tm),
        grid=(m // tm,),
        in_specs=[
            pl.BlockSpec((tm, d), lambda i: (i, 0)),
            pl.BlockSpec(mod.shape, lambda i: (0, 0)),
            pl.BlockSpec((tm, kh), lambda i: (i, 0)),
            pl.BlockSpec((tm, kh), lambda i: (i, 0)),
            pl.BlockSpec(w_out.shape, lambda i: (0, 0), pipeline_mode=pl.Buffered(1)),
        ],
        out_specs=pl.BlockSpec((tm, d), lambda i: (i, 0)),
        out_shape=jax.ShapeDtypeStruct((m, d), F32),
        compiler_params=_params("parallel"),
        name="out_proj",
    )(x2d, mod, hg2d, at2d, w_out)


def _ffn_kernel(x_ref, mod_ref, g_ref, wg_ref, wu_ref, wd_ref, o_ref, h_scr, *, tiles_per_mod):
    j = pl.program_id(1)
    d = x_ref.shape[1]
    shift, scale, out_gate = (_mod_vector(mod_ref, pl.program_id(0) // tiles_per_mod, k - N_MOD_EARLY, d)
                              for k in (3, 4, 5))

    @pl.when(j == 0)
    def _():
        _norm_modulate_into(x_ref, g_ref, shift, scale, h_scr)

    def gated_down():
        h = h_scr[...]
        gate = _dot(h, wg_ref[...])
        up = _dot(h, wu_ref[...])
        act = (_silu(gate) * up).astype(BF16)
        return out_gate * _dot(act, wd_ref[...])

    @pl.when(j == 0)
    def _():
        o_ref[...] = x_ref[...] + gated_down()

    @pl.when(j > 0)
    def _():
        o_ref[...] += gated_down()


def _ffn(x2d, mod, norm_g, w_gate_up, w_down, tokens_per_mod, tm=1024, tf=512):
    m, d = x2d.shape
    d_ff = w_down.shape[0]
    nf = d_ff // tf
    return pl.pallas_call(
        functools.partial(_ffn_kernel, tiles_per_mod=tokens_per_mod // tm),
        grid=(m // tm, nf),
        in_specs=[
            pl.BlockSpec((tm, d), lambda i, j: (i, 0)),
            pl.BlockSpec(mod.shape, lambda i, j: (0, 0)),
            pl.BlockSpec((1, d), lambda i, j: (0, 0)),
            pl.BlockSpec((d, tf), lambda i, j: (0, j)),
            pl.BlockSpec((d, tf), lambda i, j: (0, nf + j)),
            pl.BlockSpec((tf, d), lambda i, j: (j, 0)),
        ],
        out_specs=pl.BlockSpec((tm, d), lambda i, j: (i, 0)),
        out_shape=jax.ShapeDtypeStruct((m, d), F32),
        scratch_shapes=[pltpu.VMEM((tm, d), BF16)],
        compiler_params=_params("parallel", "arbitrary"),
        name="swiglu_ffn",
    )(x2d, mod, norm_g.reshape(1, d), w_gate_up, w_gate_up, w_down)


def kernel(x, c, ctx, c_ctx, w_mod, b_mod, norm_mix_g, norm_ffn_g, w_in, hg_lb, hg_norm_g,
           q_norm_g, k_norm_g, attn_sink, w_out, w_gate_up, w_down):
    bsz, t_len, d = x.shape
    l_len = ctx.shape[1]
    depth = w_mod.shape[0]
    assert depth == 1, "single-layer block: the context stream never needs its own outputs"
    assert bsz + 1 <= MOD_ROWS

    c_rows = jnp.concatenate([c, c_ctx[None, :], jnp.zeros((MOD_ROWS - bsz - 1, d), F32)], axis=0)
    assert w_mod.shape[-1] == N_MOD * d
    n_early = N_MOD_EARLY * d
    mod_early = _modulation(c_rows, w_mod[0], b_mod[0], n_early)

    cos, sin = _rope_tables(t_len)
    hk = HG_HEADS * HG_DK
    ctx_tiles = _tiles_covering([(hk, 3 * hk + HG_WIDTH), (HG_COLS + ATT_WIDTH, IN_COLS)])
    p_lat, p_ctx = _in_projection(x.reshape(bsz * t_len, d), ctx.reshape(bsz * l_len, d), mod_early, norm_mix_g[0],
                                  w_in[0], q_norm_g[0], k_norm_g[0], cos, sin, tokens_per_mod=t_len, ctx_mod_row=bsz,
                                  tm=2048, ctx_tiles=ctx_tiles)
    p_lat = p_lat.reshape(bsz, t_len, -1)
    p_ctx = p_ctx.reshape(bsz, l_len, -1)

    hg, mod_late, w_out16, w_gate_up16, w_down16 = _hgrn_mixer(
        p_lat, p_ctx, ctx_tiles, hg_lb, hg_norm_g[0], c_rows, w_mod[0], b_mod[0], n_early,
        cast_weights=(w_out[0], w_gate_up[0], w_down[0]))
    at = _window_attention(p_lat, p_ctx, ctx_tiles, attn_sink[0])

    x1 = _out_projection(x.reshape(bsz * t_len, d), mod_late, hg.reshape(bsz * t_len, HG_WIDTH),
                         at.reshape(bsz * t_len, ATT_WIDTH), w_out16, tokens_per_mod=t_len)
    x2 = _ffn(x1, mod_late, norm_ffn_g[0], w_gate_up16, w_down16, tokens_per_mod=t_len)
    return x2.reshape(bsz, t_len, d)
```
